```python
import math, functools
import jax, jax.numpy as jnp
from jax import lax
import numpy as np

D_MODEL = 1024
BATCH = 8
SEQ = 2048
DEPTH = 1
DEC_BATCH = 128
DEC_SEQ = 1
PAST_LEN = 16384
PAGE_SIZE = 128

N_META = 16
MIX_WIDTH = D_MODEL
CONV_CH = MIX_WIDTH // 2
CONV_WIDTH = 3
RET_HEADS = 4
QK_DIM = (MIX_WIDTH // 2) // RET_HEADS
V_DIM = (MIX_WIDTH // 2) // RET_HEADS
RET_WIDTH = RET_HEADS * V_DIM
CHUNK = 128
D_FF = 4 * D_MODEL
ROPE_BASE = 10000.0
EPS = 1e-6
IN_WIDTHS = (CONV_CH, CONV_CH, CONV_CH, RET_HEADS * QK_DIM, RET_HEADS * QK_DIM, RET_WIDTH, RET_WIDTH)
IN_TOTAL = sum(IN_WIDTHS)
SPLITS = tuple(int(s) for s in np.cumsum(IN_WIDTHS)[:-1])

kernel_name = "hymba_conv_retention_decode_step"


def rmsnorm(x, g):
    xf = x.astype(jnp.float32)
    y = xf * lax.rsqrt(jnp.mean(xf * xf, axis=-1, keepdims=True) + EPS)
    return (y * g.astype(jnp.float32)).astype(x.dtype)


def head_rmsnorm(o):
    of = o.astype(jnp.float32)
    y = of * lax.rsqrt(jnp.mean(of * of, axis=-1, keepdims=True) + EPS)
    return y.astype(o.dtype)


def rotary(x, pos):
    dh = x.shape[-1]
    inv_freq = 1.0 / (ROPE_BASE ** (jnp.arange(0, dh, 2, dtype=jnp.float32) / dh))
    ang = pos[:, None] * inv_freq[None, :]
    cos = jnp.cos(ang)[None, :, None, :].astype(x.dtype)
    sin = jnp.sin(ang)[None, :, None, :].astype(x.dtype)
    x1, x2 = jnp.split(x, 2, axis=-1)
    return jnp.concatenate([x1 * cos - x2 * sin, x1 * sin + x2 * cos], axis=-1)


def causal_conv3(u_prev, u, w):
    full = jnp.concatenate([u_prev.astype(u.dtype), u], axis=1)
    y = w[0] * full[:, :-2] + w[1] * full[:, 1:-1] + w[2] * full[:, 2:]
    return y, full[:, -(CONV_WIDTH - 1):]


def retention_block(q, k, v, s_prev, log_gamma):
    L = q.shape[1]
    idx = jnp.arange(L, dtype=jnp.float32)
    diff = idx[:, None] - idx[None, :]
    decay = jnp.where(diff >= 0,
                      jnp.exp(log_gamma[:, None, None] * jnp.maximum(diff, 0.0)),
                      0.0).astype(q.dtype)
    scores = jnp.einsum('bihd,bjhd->bhij', q, k) * decay
    inner = jnp.einsum('bhij,bjhe->bihe', scores, v)
    q_decay = jnp.exp((idx[:, None] + 1.0) * log_gamma[None, :]).astype(q.dtype)
    cross = jnp.einsum('bihd,bhde->bihe', q, s_prev) * q_decay[None, :, :, None]
    k_decay = jnp.exp((L - 1.0 - idx)[:, None] * log_gamma[None, :]).astype(q.dtype)
    s_new = (s_prev * jnp.exp(L * log_gamma).astype(q.dtype)[None, :, None, None]
             + jnp.einsum('bjhd,bjhe->bhde', k * k_decay[None, :, :, None], v))
    return inner + cross, s_new


def retention_prompt(q, k, v, s_prev, log_gamma):
    o_meta, s = retention_block(q[:, :N_META], k[:, :N_META], v[:, :N_META], s_prev, log_gamma)
    Bsz, T, H, dk = q.shape
    n_chunks = (T - N_META) // CHUNK

    def to_chunks(a):
        a = a[:, N_META:].reshape(Bsz, n_chunks, CHUNK, H, a.shape[-1])
        return jnp.moveaxis(a, 1, 0)

    def step(s_c, blk):
        qc, kc, vc = blk
        o_c, s_n = retention_block(qc, kc, vc, s_c, log_gamma)
        return s_n, o_c

    s_final, o_chunks = lax.scan(step, s, (to_chunks(q), to_chunks(k), to_chunks(v)))
    o_real = jnp.moveaxis(o_chunks, 0, 1).reshape(Bsz, n_chunks * CHUNK, H, v.shape[-1])
    return jnp.concatenate([o_meta, o_real], axis=1), s_final


def layer(h, pos, conv_prev, ret_prev, norm_mix, w_in, conv_w, w_out,
          norm_mlp, w_up, w_down, ret_fn):
    Bsz, T, _ = h.shape
    hn = rmsnorm(h, norm_mix)
    z = hn @ w_in
    u_in, c_gate, b_gate, q, k, v, g = jnp.split(z, SPLITS, axis=-1)
    conv_y, conv_state = causal_conv3(conv_prev, c_gate * u_in, conv_w)
    y_conv = b_gate * conv_y
    q = rotary(q.reshape(Bsz, T, RET_HEADS, QK_DIM), pos)
    k = rotary(k.reshape(Bsz, T, RET_HEADS, QK_DIM), pos) * (QK_DIM ** -0.5)
    v = v.reshape(Bsz, T, RET_HEADS, V_DIM)
    o, ret_state = ret_fn(q, k, v, ret_prev.astype(q.dtype))
    y_ret = head_rmsnorm(o).reshape(Bsz, T, RET_WIDTH) * jax.nn.silu(g)
    h = h + jnp.concatenate([y_conv, y_ret], axis=-1) @ w_out
    hn2 = rmsnorm(h, norm_mlp)
    h = h + jnp.square(jax.nn.relu(hn2 @ w_up)) @ w_down
    return h, conv_state, ret_state


def setup_inputs(seed: int = 0) -> dict:
    key = jax.random.key(seed)
    ks = jax.random.split(key, 16)
    f32 = jnp.float32
    return {
        "x_prompt": jax.random.normal(ks[0], (BATCH, SEQ, D_MODEL), f32),
        "x_sample": jax.random.normal(ks[1], (DEC_BATCH, DEC_SEQ, D_MODEL), f32),
        "cache_conv": jax.random.normal(ks[2], (DEPTH, DEC_BATCH, CONV_WIDTH - 1, CONV_CH), f32),
        "state_ret": 0.5 * jax.random.normal(ks[3], (DEPTH, DEC_BATCH, RET_HEADS, QK_DIM, V_DIM), f32),
        "meta_tokens": jax.random.normal(ks[4], (N_META, D_MODEL), f32),
        "norm_mix": 1.0 + 0.02 * jax.random.normal(ks[5], (DEPTH, D_MODEL), f32),
        "w_in": jax.random.normal(ks[6], (DEPTH, D_MODEL, IN_TOTAL), f32) * D_MODEL ** -0.5,
        "conv_w": jax.random.normal(ks[7], (DEPTH, CONV_WIDTH, CONV_CH), f32) * CONV_WIDTH ** -0.5,
        "w_out": jax.random.normal(ks[8], (DEPTH, MIX_WIDTH, D_MODEL), f32) * MIX_WIDTH ** -0.5,
        "norm_mlp": 1.0 + 0.02 * jax.random.normal(ks[9], (DEPTH, D_MODEL), f32),
        "w_up": jax.random.normal(ks[10], (DEPTH, D_MODEL, D_FF), f32) * D_MODEL ** -0.5,
        "w_down": jax.random.normal(ks[11], (DEPTH, D_FF, D_MODEL), f32) * D_FF ** -0.5,
        "norm_final": 1.0 + 0.02 * jax.random.normal(ks[12], (D_MODEL,), f32),
    }


def reference(x_prompt, x_sample, cache_conv, state_ret, meta_tokens, norm_mix, w_in,
              conv_w, w_out, norm_mlp, w_up, w_down, norm_final):
    log_gamma = jnp.log(1.0 - 2.0 ** (-5.0 - jnp.arange(RET_HEADS, dtype=jnp.float32)))
    ret_prompt_fn = functools.partial(retention_prompt, log_gamma=log_gamma)
    ret_sample_fn = functools.partial(retention_block, log_gamma=log_gamma)

    meta = jnp.broadcast_to(meta_tokens.astype(x_prompt.dtype)[None], (BATCH, N_META, D_MODEL))
    hp = jnp.concatenate([meta, x_prompt], axis=1)
    pos_p = jnp.arange(SEQ + N_META, dtype=jnp.float32)
    hs = x_sample
    pos_s = PAST_LEN + jnp.arange(DEC_SEQ, dtype=jnp.float32)

    conv_zero = jnp.zeros((BATCH, CONV_WIDTH - 1, CONV_CH), x_prompt.dtype)
    ret_zero = jnp.zeros((BATCH, RET_HEADS, QK_DIM, V_DIM), x_prompt.dtype)
    conv_p_list, ret_p_list, conv_s_list, ret_s_list = [], [], [], []
    for l in range(DEPTH):
        hp, cp, rp = layer(hp, pos_p, conv_zero, ret_zero, norm_mix[l], w_in[l], conv_w[l],
                           w_out[l], norm_mlp[l], w_up[l], w_down[l], ret_prompt_fn)
        hs, cs, rs = layer(hs, pos_s, cache_conv[l], state_ret[l], norm_mix[l], w_in[l], conv_w[l],
                           w_out[l], norm_mlp[l], w_up[l], w_down[l], ret_sample_fn)
        conv_p_list.append(cp); ret_p_list.append(rp)
        conv_s_list.append(cs); ret_s_list.append(rs)

    y_prompt = rmsnorm(hp, norm_final)[:, N_META:]
    y_sample = rmsnorm(hs, norm_final)
    new_conv_prompt = jnp.stack(conv_p_list)
    new_ret_prompt = jnp.stack(ret_p_list)
    new_conv_sample = jnp.stack(conv_s_list)
    new_ret_sample = jnp.stack(ret_s_list)
    return (y_prompt, y_sample, new_conv_prompt, new_ret_prompt, new_conv_sample, new_ret_sample)
```

```python
import functools

import numpy as np
import jax
import jax.numpy as jnp
from jax import lax
from jax.experimental import pallas as pl
from jax.experimental.pallas import tpu as pltpu

D_MODEL = 1024
N_META = 16
CONV_CH = 512
CONV_WIDTH = 3
RET_HEADS = 4
HEAD_DIM = 128
RET_WIDTH = RET_HEADS * HEAD_DIM
CHUNK = 128
D_FF = 4 * D_MODEL
EPS = 1e-6
ROPE_BASE = 10000.0
PAST_LEN = 16384
K_SCALE = HEAD_DIM ** -0.5

OFF_U, OFF_C, OFF_B, OFF_Q, OFF_K, OFF_V, OFF_G = 0, 512, 1024, 1536, 2048, 2560, 3072
IN_TOTAL = 3584

GAMMA = tuple(1.0 - 2.0 ** (-5.0 - h) for h in range(RET_HEADS))
LOG_GAMMA = tuple(float(np.log(g)) for g in GAMMA)

TOKEN_TILE = 512
SAMPLE_BLOCK = 8
VMEM_LIMIT_BYTES = 56 * 1024 * 1024

F32 = jnp.float32
BF16 = jnp.bfloat16


def _rmsnorm(x, gain):
    return x * lax.rsqrt(jnp.mean(x * x, axis=-1, keepdims=True) + EPS) * gain


def _dot(a, b):
    return jnp.dot(a, b, preferred_element_type=F32)


def _dot_nt(a, b):
    return lax.dot_general(a, b, (((1,), (1,)), ((), ())), preferred_element_type=F32)


def _dot_tn(a, b):
    return lax.dot_general(a, b, (((0,), (0,)), ((), ())), preferred_element_type=F32)


def _rope_tables(pos, inv_freq):
    ang = pos * inv_freq
    lane = lax.broadcasted_iota(jnp.int32, ang.shape, 1)
    sin = jnp.sin(ang)
    return jnp.cos(ang), jnp.where(lane < HEAD_DIM // 2, -sin, sin)


def _rope(x, cos, sin):
    return x * cos + pltpu.roll(x, HEAD_DIM // 2, 1) * sin


def _decay_tables(dec_ref, n_valid=CHUNK):
    row = lax.broadcasted_iota(jnp.int32, (CHUNK, CHUNK), 0).astype(F32)
    col = lax.broadcasted_iota(jnp.int32, (CHUNK, CHUNK), 1).astype(F32)
    diff = row - col
    for h in range(RET_HEADS):
        lg = LOG_GAMMA[h]
        dec_ref[h] = jnp.where(diff >= 0, jnp.exp(lg * jnp.maximum(diff, 0.0)), 0.0)
        dec_ref[RET_HEADS + h] = jnp.exp((row + 1.0) * lg)
        dec_ref[2 * RET_HEADS + h] = jnp.where(row < n_valid, jnp.exp((n_valid - 1.0 - row) * lg), 0.0)


def _meta_kernel(meta_ref, nm_ref, win_ref, invf_ref, smeta_ref, cmeta_ref, xpad, zbuf):
    xpad[...] = jnp.zeros_like(xpad)
    xpad[0:N_META, :] = meta_ref[...]
    hn = _rmsnorm(xpad[...], nm_ref[...]).astype(BF16)
    zbuf[...] = _dot(hn, win_ref[...])
    cu = zbuf[:, OFF_C:OFF_C + CONV_CH] * zbuf[:, OFF_U:OFF_U + CONV_CH]
    cmeta_ref[...] = cu[N_META - 2:N_META, :]
    row = lax.broadcasted_iota(jnp.int32, (CHUNK, HEAD_DIM), 0).astype(F32)
    cos, sin = _rope_tables(row, invf_ref[...])
    for h in range(RET_HEADS):
        kz = zbuf[:, OFF_K + h * HEAD_DIM:OFF_K + (h + 1) * HEAD_DIM]
        vz = zbuf[:, OFF_V + h * HEAD_DIM:OFF_V + (h + 1) * HEAD_DIM]
        k = _rope(kz, cos, sin) * K_SCALE
        kdec = jnp.where(row < N_META, jnp.exp((N_META - 1.0 - row) * LOG_GAMMA[h]), 0.0)
        smeta_ref[h] = _dot_tn((k * kdec).astype(BF16), vz.astype(BF16))


def _meta_call(meta, norm_mix, w_in, inv_freq):
    return pl.pallas_call(
        _meta_kernel,
        out_shape=(jax.ShapeDtypeStruct((RET_HEADS, HEAD_DIM, HEAD_DIM), F32),
                   jax.ShapeDtypeStruct((CONV_WIDTH - 1, CONV_CH), F32)),
        scratch_shapes=[pltpu.VMEM((CHUNK, D_MODEL), F32), pltpu.VMEM((CHUNK, IN_TOTAL), F32)],
        compiler_params=pltpu.CompilerParams(vmem_limit_bytes=VMEM_LIMIT_BYTES),
        name="meta_state",
    )(meta, norm_mix, w_in, inv_freq)


def _mixer_kernel(x_ref, nm_ref, win_ref, cw_ref, wout_ref, invf_ref, smeta_ref, cmeta_ref,
                  h1_ref, cstate_ref, sstate_ref,
                  zbuf, cbuf, ymix, cos_sc, sin_sc, dec_sc):
    b = pl.program_id(0)
    t = pl.program_id(1)
    tm = TOKEN_TILE
    tile0 = pl.multiple_of(t * tm, tm)

    @pl.when(jnp.logical_and(t == 0, b == 0))
    def _():
        _decay_tables(dec_sc)

    @pl.when(b == 0)
    def _():
        pos = (lax.broadcasted_iota(jnp.int32, (tm, HEAD_DIM), 0) + (N_META + t * tm)).astype(F32)
        cos, sin = _rope_tables(pos, invf_ref[...])
        cos_sc[pl.ds(tile0, tm), :] = cos
        sin_sc[pl.ds(tile0, tm), :] = sin

    @pl.when(t == 0)
    def _():
        sstate_ref[0, 0] = smeta_ref[...]
        cbuf[6:8, :] = cmeta_ref[...]

    x = x_ref[0]
    hn = _rmsnorm(x, nm_ref[...]).astype(BF16)
    zbuf[...] = _dot(hn, win_ref[...])

    for c in range(tm // CHUNK):
        rows = slice(c * CHUNK, (c + 1) * CHUNK)
        cu = zbuf[rows, OFF_C:OFF_C + CONV_CH] * zbuf[rows, OFF_U:OFF_U + CONV_CH]
        cbuf[8 + c * CHUNK:8 + (c + 1) * CHUNK, :] = cu
        conv = (cw_ref[0:1, :] * cbuf[6 + c * CHUNK:6 + (c + 1) * CHUNK, :]
                + cw_ref[1:2, :] * cbuf[7 + c * CHUNK:7 + (c + 1) * CHUNK, :]
                + cw_ref[2:3, :] * cu)
        ymix[rows, 0:CONV_CH] = (zbuf[rows, OFF_B:OFF_B + CONV_CH] * conv).astype(BF16)
        cos = cos_sc[pl.ds(tile0 + c * CHUNK, CHUNK), :]
        sin = sin_sc[pl.ds(tile0 + c * CHUNK, CHUNK), :]
        for h in range(RET_HEADS):
            qz = zbuf[rows, OFF_Q + h * HEAD_DIM:OFF_Q + (h + 1) * HEAD_DIM]
            kz = zbuf[rows, OFF_K + h * HEAD_DIM:OFF_K + (h + 1) * HEAD_DIM]
            vb = zbuf[rows, OFF_V + h * HEAD_DIM:OFF_V + (h + 1) * HEAD_DIM].astype(BF16)
            gz = zbuf[rows, OFF_G + h * HEAD_DIM:OFF_G + (h + 1) * HEAD_DIM]
            q = _rope(qz, cos, sin)
            k = _rope(kz, cos, sin) * K_SCALE
            qb = q.astype(BF16)
            kb = k.astype(BF16)
            state = sstate_ref[0, 0, h]
            p = (_dot_nt(qb, kb) * dec_sc[h]).astype(BF16)
            o = _dot(p, vb) + _dot(qb, state.astype(BF16)) * dec_sc[RET_HEADS + h]
            kd = (k * dec_sc[2 * RET_HEADS + h]).astype(BF16)
            sstate_ref[0, 0, h] = state * (GAMMA[h] ** CHUNK) + _dot_tn(kd, vb)
            on = o * lax.rsqrt(jnp.mean(o * o, axis=-1, keepdims=True) + EPS)
            ymix[rows, CONV_CH + h * HEAD_DIM:CONV_CH + (h + 1) * HEAD_DIM] = (
                on * (gz * jax.nn.sigmoid(gz))).astype(BF16)

    tail = cbuf[8 + tm - 2:8 + tm, :]
    cbuf[6:8, :] = tail
    cstate_ref[0, 0] = tail
    h1_ref[0] = x + _dot(ymix[...], wout_ref[...])


def _full(shape):
    return pl.BlockSpec(shape, lambda *_: (0,) * len(shape))


def _mixer_call(x, norm_mix, w_in, conv_w, w_out, inv_freq, smeta, cmeta):
    bsz, seq, _ = x.shape
    tm = TOKEN_TILE
    grid = (bsz, seq // tm)
    return pl.pallas_call(
        _mixer_kernel,
        grid=grid,
        in_specs=[
            pl.BlockSpec((1, tm, D_MODEL), lambda b, t: (b, t, 0)),
            _full((1, D_MODEL)),
            _full((D_MODEL, IN_TOTAL)),
            _full((CONV_WIDTH, CONV_CH)),
            _full((D_MODEL, D_MODEL)),
            _full((1, HEAD_DIM)),
            _full((RET_HEADS, HEAD_DIM, HEAD_DIM)),
            _full((CONV_WIDTH - 1, CONV_CH)),
        ],
        out_specs=[
            pl.BlockSpec((1, tm, D_MODEL), lambda b, t: (b, t, 0)),
            pl.BlockSpec((1, 1, CONV_WIDTH - 1, CONV_CH), lambda b, t: (0, b, 0, 0)),
            pl.BlockSpec((1, 1, RET_HEADS, HEAD_DIM, HEAD_DIM), lambda b, t: (0, b, 0, 0, 0)),
        ],
        out_shape=(
            jax.ShapeDtypeStruct((bsz, seq, D_MODEL), F32),
            jax.ShapeDtypeStruct((1, bsz, CONV_WIDTH - 1, CONV_CH), F32),
            jax.ShapeDtypeStruct((1, bsz, RET_HEADS, HEAD_DIM, HEAD_DIM), F32),
        ),
        scratch_shapes=[
            pltpu.VMEM((tm, IN_TOTAL), F32),
            pltpu.VMEM((tm + 8, CONV_CH), F32),
            pltpu.VMEM((tm, D_MODEL), BF16),
            pltpu.VMEM((seq, HEAD_DIM), F32),
            pltpu.VMEM((seq, HEAD_DIM), F32),
            pltpu.VMEM((3 * RET_HEADS, CHUNK, CHUNK), F32),
        ],
        compiler_params=pltpu.CompilerParams(
            dimension_semantics=("arbitrary", "arbitrary"), vmem_limit_bytes=VMEM_LIMIT_BYTES),
        name="prompt_mixer",
    )(x, norm_mix, w_in, conv_w, w_out, inv_freq, smeta, cmeta)


def _mlp_body(h, nm, wup_ref, wdown_ref, nf):
    hn = _rmsnorm(h, nm).astype(BF16)
    acc = h
    ff_block = 1024
    for c in range(D_FF // ff_block):
        up = _dot(hn, wup_ref[:, c * ff_block:(c + 1) * ff_block])
        act = jnp.square(jnp.maximum(up, 0.0)).astype(BF16)
        acc = acc + _dot(act, wdown_ref[c * ff_block:(c + 1) * ff_block, :])
    return _rmsnorm(acc, nf)


def _mlp_kernel(h_ref, nm_ref, wup_ref, wdown_ref, nf_ref, y_ref):
    y_ref[...] = _mlp_body(h_ref[...], nm_ref[...], wup_ref, wdown_ref, nf_ref[...])


def _mlp_call(h, norm_mlp, w_up, w_down, norm_final):
    rows = h.shape[0]
    tm = TOKEN_TILE
    return pl.pallas_call(
        _mlp_kernel,
        grid=(rows // tm,),
        in_specs=[
            pl.BlockSpec((tm, D_MODEL), lambda i: (i, 0)),
            _full((1, D_MODEL)),
            _full((D_MODEL, D_FF)),
            _full((D_FF, D_MODEL)),
            _full((1, D_MODEL)),
        ],
        out_specs=pl.BlockSpec((tm, D_MODEL), lambda i: (i, 0)),
        out_shape=jax.ShapeDtypeStruct((rows, D_MODEL), F32),
        compiler_params=pltpu.CompilerParams(
            dimension_semantics=("arbitrary",), vmem_limit_bytes=VMEM_LIMIT_BYTES),
        name="prompt_mlp",
    )(h, norm_mlp, w_up, w_down, norm_final)


def _sample_proj_kernel(x_ref, cache_ref, nm_ref, win_ref, cw_ref, invf_ref,
                        q_ref, kt_ref, v_ref, inner_ref, gate_ref, yconv_ref, cnew_ref):
    hn = _rmsnorm(x_ref[...], nm_ref[...]).astype(BF16)
    z = _dot(hn, win_ref[...])
    prev0 = cache_ref[:, 0:CONV_CH]
    prev1 = cache_ref[:, CONV_CH:2 * CONV_CH]
    cu = z[:, OFF_C:OFF_C + CONV_CH] * z[:, OFF_U:OFF_U + CONV_CH]
    conv = cw_ref[0:1, :] * prev0 + cw_ref[1:2, :] * prev1 + cw_ref[2:3, :] * cu
    yconv_ref[...] = z[:, OFF_B:OFF_B + CONV_CH] * conv
    cnew_ref[:, 0:CONV_CH] = prev1
    cnew_ref[:, CONV_CH:2 * CONV_CH] = cu
    pos = jnp.full((1, HEAD_DIM), float(PAST_LEN), F32)
    cos, sin = _rope_tables(pos, invf_ref[...])
    for h in range(RET_HEADS):
        cols = slice(h * HEAD_DIM, (h + 1) * HEAD_DIM)
        q = _rope(z[:, OFF_Q + h * HEAD_DIM:OFF_Q + (h + 1) * HEAD_DIM], cos, sin)
        k = _rope(z[:, OFF_K + h * HEAD_DIM:OFF_K + (h + 1) * HEAD_DIM], cos, sin) * K_SCALE
        v = z[:, OFF_V + h * HEAD_DIM:OFF_V + (h + 1) * HEAD_DIM]
        g = z[:, OFF_G + h * HEAD_DIM:OFF_G + (h + 1) * HEAD_DIM]
        q_ref[:, cols] = q
        kt_ref[h] = k.T
        v_ref[:, cols] = v
        inner_ref[:, cols] = jnp.sum(q * k, axis=-1, keepdims=True) * v
        gate_ref[:, cols] = g * jax.nn.sigmoid(g)


def _sample_proj_call(xs, cache, norm_mix, w_in, conv_w, inv_freq):
    n = xs.shape[0]
    wide = jax.ShapeDtypeStruct((n, RET_WIDTH), F32)
    return pl.pallas_call(
        _sample_proj_kernel,
        out_shape=(wide,
                   jax.ShapeDtypeStruct((RET_HEADS, HEAD_DIM, n), F32),
                   wide, wide, wide,
                   jax.ShapeDtypeStruct((n, CONV_CH), F32),
                   jax.ShapeDtypeStruct((n, 2 * CONV_CH), F32)),
        compiler_params=pltpu.CompilerParams(vmem_limit_bytes=VMEM_LIMIT_BYTES),
        name="sample_proj",
    )(xs, cache, norm_mix, w_in, conv_w, inv_freq)


def _sample_state_kernel(q_ref, kt_ref, v_ref, st_ref, cross_ref, new_ref):
    step = pl.program_id(0)
    lane = lax.broadcasted_iota(jnp.int32, (HEAD_DIM, HEAD_DIM), 1)
    for i in range(SAMPLE_BLOCK):
        b = step * SAMPLE_BLOCK + i
        for h in range(RET_HEADS):
            cols = slice(h * HEAD_DIM, (h + 1) * HEAD_DIM)
            state = st_ref[i, h]
            q8 = jnp.broadcast_to(q_ref[i:i + 1, cols], (8, HEAD_DIM)).astype(BF16)
            cross_ref[i:i + 1, cols] = _dot(q8, state.astype(BF16))[0:1, :] * GAMMA[h]
            kt_b = jnp.where(lane == b, kt_ref[h], 0.0).astype(BF16)
            new_ref[i, h] = state * GAMMA[h] + _dot(kt_b, v_ref[:, cols].astype(BF16))


def _sample_state_call(q, kt, v, state):
    n = q.shape[0]
    bb = SAMPLE_BLOCK
    return pl.pallas_call(
        _sample_state_kernel,
        grid=(n // bb,),
        in_specs=[
            pl.BlockSpec((bb, RET_WIDTH), lambda i: (i, 0)),
            _full((RET_HEADS, HEAD_DIM, n)),
            _full((n, RET_WIDTH)),
            pl.BlockSpec((bb, RET_HEADS, HEAD_DIM, HEAD_DIM), lambda i: (i, 0, 0, 0)),
        ],
        out_specs=[
            pl.BlockSpec((bb, RET_WIDTH), lambda i: (i, 0)),
            pl.BlockSpec((bb, RET_HEADS, HEAD_DIM, HEAD_DIM), lambda i: (i, 0, 0, 0)),
        ],
        out_shape=(jax.ShapeDtypeStruct((n, RET_WIDTH), F32),
                   jax.ShapeDtypeStruct(state.shape, F32)),
        compiler_params=pltpu.CompilerParams(
            dimension_semantics=("arbitrary",), vmem_limit_bytes=VMEM_LIMIT_BYTES),
        name="sample_state",
    )(q, kt, v, state)


def _sample_out_kernel(x_ref, yconv_ref, inner_ref, cross_ref, gate_ref, wout_ref,
                       nmlp_ref, wup_ref, wdown_ref, nf_ref, y_ref, ymix):
    ymix[:, 0:CONV_CH] = yconv_ref[...].astype(BF16)
    for h in range(RET_HEADS):
        cols = slice(h * HEAD_DIM, (h + 1) * HEAD_DIM)
        o = inner_ref[:, cols] + cross_ref[:, cols]
        on = o * lax.rsqrt(jnp.mean(o * o, axis=-1, keepdims=True) + EPS)
        ymix[:, CONV_CH + h * HEAD_DIM:CONV_CH + (h + 1) * HEAD_DIM] = (
            on * gate_ref[:, cols]).astype(BF16)
    h1 = x_ref[...] + _dot(ymix[...], wout_ref[...])
    y_ref[...] = _mlp_body(h1, nmlp_ref[...], wup_ref, wdown_ref, nf_ref[...])


def _sample_out_call(xs, yconv, inner, cross, gate, w_out, norm_mlp, w_up, w_down, norm_final):
    n = xs.shape[0]
    return pl.pallas_call(
        _sample_out_kernel,
        out_shape=jax.ShapeDtypeStruct((n, D_MODEL), F32),
        scratch_shapes=[pltpu.VMEM((n, D_MODEL), BF16)],
        compiler_params=pltpu.CompilerParams(vmem_limit_bytes=VMEM_LIMIT_BYTES),
        name="sample_out",
    )(xs, yconv, inner, cross, gate, w_out, norm_mlp, w_up, w_down, norm_final)


def kernel(x_prompt, x_sample, cache_conv, state_ret, meta_tokens, norm_mix, w_in, conv_w, w_out,
           norm_mlp, w_up, w_down, norm_final):
    bsz, seq, _ = x_prompt.shape
    n_dec = x_sample.shape[0]
    half = jnp.arange(0, HEAD_DIM, 2, dtype=F32) / HEAD_DIM
    inv_half = 1.0 / (ROPE_BASE ** half)
    inv_freq = jnp.concatenate([inv_half, inv_half])[None, :]

    w_in_b = w_in[0].astype(BF16)
    w_out_b = w_out[0].astype(BF16)
    w_up_b = w_up[0].astype(BF16)
    w_down_b = w_down[0].astype(BF16)
    nmix = norm_mix[0][None, :]
    nmlp = norm_mlp[0][None, :]
    nfin = norm_final[None, :]
    cw = conv_w[0]

    smeta, cmeta = _meta_call(meta_tokens, nmix, w_in_b, inv_freq)
    h1, conv_p, ret_p = _mixer_call(x_prompt, nmix, w_in_b, cw, w_out_b, inv_freq, smeta, cmeta)
    y_prompt = _mlp_call(h1.reshape(bsz * seq, D_MODEL), nmlp, w_up_b, w_down_b, nfin)
    y_prompt = y_prompt.reshape(bsz, seq, D_MODEL)

    xs = x_sample.reshape(n_dec, D_MODEL)
    cache = cache_conv[0].reshape(n_dec, (CONV_WIDTH - 1) * CONV_CH)
    q, kt, v, inner, gate, yconv, cnew = _sample_proj_call(xs, cache, nmix, w_in_b, cw, inv_freq)
    cross, ret_s = _sample_state_call(q, kt, v, state_ret[0])
    y_sample = _sample_out_call(xs, yconv, inner, cross, gate, w_out_b, nmlp, w_up_b, w_down_b, nfin)

    return (y_prompt,
            y_sample.reshape(n_dec, 1, D_MODEL),
            conv_p,
            ret_p,
            cnew.reshape(1, n_dec, CONV_WIDTH - 1, CONV_CH),
            ret_s[None])
```

```python
import functools

import numpy as np
import jax
import jax.numpy as jnp
from jax import lax
from jax.experimental import pallas as pl
from jax.experimental.pallas import tpu as pltpu

D_MODEL = 1024
N_META = 16
CONV_CH = 512
CONV_WIDTH = 3
RET_HEADS = 4
HEAD_DIM = 128
RET_WIDTH = RET_HEADS * HEAD_DIM
CHUNK = 128
D_FF = 4 * D_MODEL
EPS = 1e-6
ROPE_BASE = 10000.0
PAST_LEN = 16384
K_SCALE = HEAD_DIM ** -0.5

OFF_U, OFF_C, OFF_B, OFF_Q, OFF_K, OFF_V, OFF_G = 0, 512, 1024, 1536, 2048, 2560, 3072
IN_TOTAL = 3584

GAMMA = tuple(1.0 - 2.0 ** (-5.0 - h) for h in range(RET_HEADS))
LOG_GAMMA = tuple(float(np.log(g)) for g in GAMMA)

TOKEN_TILE = 512
SAMPLE_BLOCK = 8
VMEM_LIMIT_BYTES = 56 * 1024 * 1024

F32 = jnp.float32
BF16 = jnp.bfloat16


def _rmsnorm(x, gain):
    return x * lax.rsqrt(jnp.mean(x * x, axis=-1, keepdims=True) + EPS) * gain


def _dot(a, b):
    return jnp.dot(a, b, preferred_element_type=F32)


def _dot_nt(a, b):
    return lax.dot_general(a, b, (((1,), (1,)), ((), ())), preferred_element_type=F32)


def _dot_tn(a, b):
    return lax.dot_general(a, b, (((0,), (0,)), ((), ())), preferred_element_type=F32)


def _rope_tables(pos, inv_freq):
    ang = pos * inv_freq
    lane = lax.broadcasted_iota(jnp.int32, ang.shape, 1)
    sin = jnp.sin(ang)
    return jnp.cos(ang), jnp.where(lane < HEAD_DIM // 2, -sin, sin)


def _rope(x, cos, sin):
    return x * cos + pltpu.roll(x, HEAD_DIM // 2, 1) * sin


def _decay_tables(dec_ref, n_valid=CHUNK):
    row = lax.broadcasted_iota(jnp.int32, (CHUNK, CHUNK), 0).astype(F32)
    col = lax.broadcasted_iota(jnp.int32, (CHUNK, CHUNK), 1).astype(F32)
    diff = row - col
    for h in range(RET_HEADS):
        lg = LOG_GAMMA[h]
        dec_ref[h] = jnp.where(diff >= 0, jnp.exp(lg * jnp.maximum(diff, 0.0)), 0.0)
        dec_ref[RET_HEADS + h] = jnp.exp((row + 1.0) * lg)
        dec_ref[2 * RET_HEADS + h] = jnp.where(row < n_valid, jnp.exp((n_valid - 1.0 - row) * lg), 0.0)


def _meta_kernel(meta_ref, nm_ref, win_ref, invf_ref, smeta_ref, cmeta_ref, xpad, zbuf):
    xpad[...] = jnp.zeros_like(xpad)
    xpad[0:N_META, :] = meta_ref[...]
    hn = _rmsnorm(xpad[...], nm_ref[...]).astype(BF16)
    zbuf[...] = _dot(hn, win_ref[...])
    cu = zbuf[:, OFF_C:OFF_C + CONV_CH] * zbuf[:, OFF_U:OFF_U + CONV_CH]
    cmeta_ref[...] = cu[N_META - 2:N_META, :]
    row = lax.broadcasted_iota(jnp.int32, (CHUNK, HEAD_DIM), 0).astype(F32)
    cos, sin = _rope_tables(row, invf_ref[...])
    for h in range(RET_HEADS):
        kz = zbuf[:, OFF_K + h * HEAD_DIM:OFF_K + (h + 1) * HEAD_DIM]
        vz = zbuf[:, OFF_V + h * HEAD_DIM:OFF_V + (h + 1) * HEAD_DIM]
        k = _rope(kz, cos, sin) * K_SCALE
        kdec = jnp.where(row < N_META, jnp.exp((N_META - 1.0 - row) * LOG_GAMMA[h]), 0.0)
        smeta_ref[h] = _dot_tn((k * kdec).astype(BF16), vz.astype(BF16))


def _meta_call(meta, norm_mix, w_in, inv_freq):
    return pl.pallas_call(
        _meta_kernel,
        out_shape=(jax.ShapeDtypeStruct((RET_HEADS, HEAD_DIM, HEAD_DIM), F32),
                   jax.ShapeDtypeStruct((CONV_WIDTH - 1, CONV_CH), F32)),
        scratch_shapes=[pltpu.VMEM((CHUNK, D_MODEL), F32), pltpu.VMEM((CHUNK, IN_TOTAL), F32)],
        compiler_params=pltpu.CompilerParams(vmem_limit_bytes=VMEM_LIMIT_BYTES),
        name="meta_state",
    )(meta, norm_mix, w_in, inv_freq)


def _mixer_kernel(x_ref, nm_ref, win_ref, cw_ref, wout_ref, invf_ref, smeta_ref, cmeta_ref,
                  h1_ref, cstate_ref, sstate_ref,
                  zbuf, cbuf, ymix, cos_sc, sin_sc, dec_sc):
    b = pl.program_id(0)
    t = pl.program_id(1)
    tm = TOKEN_TILE
    tile0 = pl.multiple_of(t * tm, tm)

    @pl.when(jnp.logical_and(t == 0, b == 0))
    def _():
        _decay_tables(dec_sc)

    @pl.when(b == 0)
    def _():
        pos = (lax.broadcasted_iota(jnp.int32, (tm, HEAD_DIM), 0) + (N_META + t * tm)).astype(F32)
        cos, sin = _rope_tables(pos, invf_ref[...])
        cos_sc[pl.ds(tile0, tm), :] = cos
        sin_sc[pl.ds(tile0, tm), :] = sin

    @pl.when(t == 0)
    def _():
        sstate_ref[0, 0] = smeta_ref[...]
        cbuf[6:8, :] = cmeta_ref[...]

    x = x_ref[0]
    hn = _rmsnorm(x, nm_ref[...]).astype(BF16)
    zbuf[...] = _dot(hn, win_ref[...])

    def conv_chunk(c):
        rows = slice(c * CHUNK, (c + 1) * CHUNK)
        cu = zbuf[rows, OFF_C:OFF_C + CONV_CH] * zbuf[rows, OFF_U:OFF_U + CONV_CH]
        cbuf[8 + c * CHUNK:8 + (c + 1) * CHUNK, :] = cu
        conv = (cw_ref[0:1, :] * cbuf[6 + c * CHUNK:6 + (c + 1) * CHUNK, :]
                + cw_ref[1:2, :] * cbuf[7 + c * CHUNK:7 + (c + 1) * CHUNK, :]
                + cw_ref[2:3, :] * cu)
        ymix[rows, 0:CONV_CH] = (zbuf[rows, OFF_B:OFF_B + CONV_CH] * conv).astype(BF16)

    def scores_and_state(c):
        rows = slice(c * CHUNK, (c + 1) * CHUNK)
        cos = cos_sc[pl.ds(tile0 + c * CHUNK, CHUNK), :]
        sin = sin_sc[pl.ds(tile0 + c * CHUNK, CHUNK), :]
        saved = []
        for h in range(RET_HEADS):
            qz = zbuf[rows, OFF_Q + h * HEAD_DIM:OFF_Q + (h + 1) * HEAD_DIM]
            kz = zbuf[rows, OFF_K + h * HEAD_DIM:OFF_K + (h + 1) * HEAD_DIM]
            vb = zbuf[rows, OFF_V + h * HEAD_DIM:OFF_V + (h + 1) * HEAD_DIM].astype(BF16)
            q = _rope(qz, cos, sin)
            k = _rope(kz, cos, sin) * K_SCALE
            qb = q.astype(BF16)
            kb = k.astype(BF16)
            state = sstate_ref[0, 0, h]
            kd = (k * dec_sc[2 * RET_HEADS + h]).astype(BF16)
            sstate_ref[0, 0, h] = state * (GAMMA[h] ** CHUNK) + _dot_tn(kd, vb)
            s = _dot_nt(qb, kb)
            cross = _dot(qb, state.astype(BF16))
            saved.append((s, cross, vb))
        return saved

    def outputs(c, saved):
        rows = slice(c * CHUNK, (c + 1) * CHUNK)
        for h in range(RET_HEADS):
            s, cross, vb = saved[h]
            gz = zbuf[rows, OFF_G + h * HEAD_DIM:OFF_G + (h + 1) * HEAD_DIM]
            p = (s * dec_sc[h]).astype(BF16)
            o = _dot(p, vb) + cross * dec_sc[RET_HEADS + h]
            on = o * lax.rsqrt(jnp.mean(o * o, axis=-1, keepdims=True) + EPS)
            ymix[rows, CONV_CH + h * HEAD_DIM:CONV_CH + (h + 1) * HEAD_DIM] = (
                on * (gz * jax.nn.sigmoid(gz))).astype(BF16)

    n_chunks = tm // CHUNK
    pending = None
    for c in range(n_chunks):
        saved = scores_and_state(c)
        conv_chunk(c)
        if pending is not None:
            outputs(c - 1, pending)
        pending = saved
    outputs(n_chunks - 1, pending)

    tail = cbuf[8 + tm - 2:8 + tm, :]
    cbuf[6:8, :] = tail
    cstate_ref[0, 0] = tail
    h1_ref[0] = x + _dot(ymix[...], wout_ref[...])


def _full(shape):
    return pl.BlockSpec(shape, lambda *_: (0,) * len(shape))


def _mixer_call(x, norm_mix, w_in, conv_w, w_out, inv_freq, smeta, cmeta):
    bsz, seq, _ = x.shape
    tm = TOKEN_TILE
    grid = (bsz, seq // tm)
    return pl.pallas_call(
        _mixer_kernel,
        grid=grid,
        in_specs=[
            pl.BlockSpec((1, tm, D_MODEL), lambda b, t: (b, t, 0)),
            _full((1, D_MODEL)),
            _full((D_MODEL, IN_TOTAL)),
            _full((CONV_WIDTH, CONV_CH)),
            _full((D_MODEL, D_MODEL)),
            _full((1, HEAD_DIM)),
            _full((RET_HEADS, HEAD_DIM, HEAD_DIM)),
            _full((CONV_WIDTH - 1, CONV_CH)),
        ],
        out_specs=[
            pl.BlockSpec((1, tm, D_MODEL), lambda b, t: (b, t, 0)),
            pl.BlockSpec((1, 1, CONV_WIDTH - 1, CONV_CH), lambda b, t: (0, b, 0, 0)),
            pl.BlockSpec((1, 1, RET_HEADS, HEAD_DIM, HEAD_DIM), lambda b, t: (0, b, 0, 0, 0)),
        ],
        out_shape=(
            jax.ShapeDtypeStruct((bsz, seq, D_MODEL), F32),
            jax.ShapeDtypeStruct((1, bsz, CONV_WIDTH - 1, CONV_CH), F32),
            jax.ShapeDtypeStruct((1, bsz, RET_HEADS, HEAD_DIM, HEAD_DIM), F32),
        ),
        scratch_shapes=[
            pltpu.VMEM((tm, IN_TOTAL), F32),
            pltpu.VMEM((tm + 8, CONV_CH), F32),
            pltpu.VMEM((tm, D_MODEL), BF16),
            pltpu.VMEM((seq, HEAD_DIM), F32),
            pltpu.VMEM((seq, HEAD_DIM), F32),
            pltpu.VMEM((3 * RET_HEADS, CHUNK, CHUNK), F32),
        ],
        compiler_params=pltpu.CompilerParams(
            dimension_semantics=("arbitrary", "arbitrary"), vmem_limit_bytes=VMEM_LIMIT_BYTES),
        name="prompt_mixer",
    )(x, norm_mix, w_in, conv_w, w_out, inv_freq, smeta, cmeta)


def _mlp_body(h, nm, wup_ref, wdown_ref, nf):
    hn = _rmsnorm(h, nm).astype(BF16)
    acc = h
    ff_block = 1024
    for c in range(D_FF // ff_block):
        up = _dot(hn, wup_ref[:, c * ff_block:(c + 1) * ff_block])
        act = jnp.square(jnp.maximum(up, 0.0)).astype(BF16)
        acc = acc + _dot(act, wdown_ref[c * ff_block:(c + 1) * ff_block, :])
    return _rmsnorm(acc, nf)


def _mlp_kernel(h_ref, nm_ref, wup_ref, wdown_ref, nf_ref, y_ref):
    y_ref[...] = _mlp_body(h_ref[...], nm_ref[...], wup_ref, wdown_ref, nf_ref[...])


def _mlp_call(h, norm_mlp, w_up, w_down, norm_final):
    rows = h.shape[0]
    tm = TOKEN_TILE
    return pl.pallas_call(
        _mlp_kernel,
        grid=(rows // tm,),
        in_specs=[
            pl.BlockSpec((tm, D_MODEL), lambda i: (i, 0)),
            _full((1, D_MODEL)),
            _full((D_MODEL, D_FF)),
            _full((D_FF, D_MODEL)),
            _full((1, D_MODEL)),
        ],
        out_specs=pl.BlockSpec((tm, D_MODEL), lambda i: (i, 0)),
        out_shape=jax.ShapeDtypeStruct((rows, D_MODEL), F32),
        compiler_params=pltpu.CompilerParams(
            dimension_semantics=("arbitrary",), vmem_limit_bytes=VMEM_LIMIT_BYTES),
        name="prompt_mlp",
    )(h, norm_mlp, w_up, w_down, norm_final)


def _sample_proj_kernel(x_ref, cache_ref, nm_ref, win_ref, cw_ref, invf_ref,
                        q_ref, kt_ref, v_ref, inner_ref, gate_ref, yconv_ref, cnew_ref):
    hn = _rmsnorm(x_ref[...], nm_ref[...]).astype(BF16)
    z = _dot(hn, win_ref[...])
    prev0 = cache_ref[:, 0:CONV_CH]
    prev1 = cache_ref[:, CONV_CH:2 * CONV_CH]
    cu = z[:, OFF_C:OFF_C + CONV_CH] * z[:, OFF_U:OFF_U + CONV_CH]
    conv = cw_ref[0:1, :] * prev0 + cw_ref[1:2, :] * prev1 + cw_ref[2:3, :] * cu
    yconv_ref[...] = z[:, OFF_B:OFF_B + CONV_CH] * conv
    cnew_ref[:, 0:CONV_CH] = prev1
    cnew_ref[:, CONV_CH:2 * CONV_CH] = cu
    pos = jnp.full((1, HEAD_DIM), float(PAST_LEN), F32)
    cos, sin = _rope_tables(pos, invf_ref[...])
    for h in range(RET_HEADS):
        cols = slice(h * HEAD_DIM, (h + 1) * HEAD_DIM)
        q = _rope(z[:, OFF_Q + h * HEAD_DIM:OFF_Q + (h + 1) * HEAD_DIM], cos, sin)
        k = _rope(z[:, OFF_K + h * HEAD_DIM:OFF_K + (h + 1) * HEAD_DIM], cos, sin) * K_SCALE
        v = z[:, OFF_V + h * HEAD_DIM:OFF_V + (h + 1) * HEAD_DIM]
        g = z[:, OFF_G + h * HEAD_DIM:OFF_G + (h + 1) * HEAD_DIM]
        q_ref[:, cols] = q
        kt_ref[h] = k.T
        v_ref[:, cols] = v
        inner_ref[:, cols] = jnp.sum(q * k, axis=-1, keepdims=True) * v
        gate_ref[:, cols] = g * jax.nn.sigmoid(g)


def _sample_proj_call(xs, cache, norm_mix, w_in, conv_w, inv_freq):
    n = xs.shape[0]
    wide = jax.ShapeDtypeStruct((n, RET_WIDTH), F32)
    return pl.pallas_call(
        _sample_proj_kernel,
        out_shape=(wide,
                   jax.ShapeDtypeStruct((RET_HEADS, HEAD_DIM, n), F32),
                   wide, wide, wide,
                   jax.ShapeDtypeStruct((n, CONV_CH), F32),
                   jax.ShapeDtypeStruct((n, 2 * CONV_CH), F32)),
        compiler_params=pltpu.CompilerParams(vmem_limit_bytes=VMEM_LIMIT_BYTES),
        name="sample_proj",
    )(xs, cache, norm_mix, w_in, conv_w, inv_freq)


def _sample_state_kernel(q_ref, kt_ref, v_ref, st_ref, cross_ref, new_ref):
    step = pl.program_id(0)
    lane = lax.broadcasted_iota(jnp.int32, (HEAD_DIM, HEAD_DIM), 1)
    for i in range(SAMPLE_BLOCK):
        b = step * SAMPLE_BLOCK + i
        for h in range(RET_HEADS):
            cols = slice(h * HEAD_DIM, (h + 1) * HEAD_DIM)
            state = st_ref[i, h]
            q8 = jnp.broadcast_to(q_ref[i:i + 1, cols], (8, HEAD_DIM)).astype(BF16)
            cross_ref[i:i + 1, cols] = _dot(q8, state.astype(BF16))[0:1, :] * GAMMA[h]
            kt_b = jnp.where(lane == b, kt_ref[h], 0.0).astype(BF16)
            new_ref[i, h] = state * GAMMA[h] + _dot(kt_b, v_ref[:, cols].astype(BF16))


def _sample_state_call(q, kt, v, state):
    n = q.shape[0]
    bb = SAMPLE_BLOCK
    return pl.pallas_call(
        _sample_state_kernel,
        grid=(n // bb,),
        in_specs=[
            pl.BlockSpec((bb, RET_WIDTH), lambda i: (i, 0)),
            _full((RET_HEADS, HEAD_DIM, n)),
            _full((n, RET_WIDTH)),
            pl.BlockSpec((bb, RET_HEADS, HEAD_DIM, HEAD_DIM), lambda i: (i, 0, 0, 0)),
        ],
        out_specs=[
            pl.BlockSpec((bb, RET_WIDTH), lambda i: (i, 0)),
            pl.BlockSpec((bb, RET_HEADS, HEAD_DIM, HEAD_DIM), lambda i: (i, 0, 0, 0)),
        ],
        out_shape=(jax.ShapeDtypeStruct((n, RET_WIDTH), F32),
                   jax.ShapeDtypeStruct(state.shape, F32)),
        compiler_params=pltpu.CompilerParams(
            dimension_semantics=("arbitrary",), vmem_limit_bytes=VMEM_LIMIT_BYTES),
        name="sample_state",
    )(q, kt, v, state)


def _sample_out_kernel(x_ref, yconv_ref, inner_ref, cross_ref, gate_ref, wout_ref,
                       nmlp_ref, wup_ref, wdown_ref, nf_ref, y_ref, ymix):
    ymix[:, 0:CONV_CH] = yconv_ref[...].astype(BF16)
    for h in range(RET_HEADS):
        cols = slice(h * HEAD_DIM, (h + 1) * HEAD_DIM)
        o = inner_ref[:, cols] + cross_ref[:, cols]
        on = o * lax.rsqrt(jnp.mean(o * o, axis=-1, keepdims=True) + EPS)
        ymix[:, CONV_CH + h * HEAD_DIM:CONV_CH + (h + 1) * HEAD_DIM] = (
            on * gate_ref[:, cols]).astype(BF16)
    h1 = x_ref[...] + _dot(ymix[...], wout_ref[...])
    y_ref[...] = _mlp_body(h1, nmlp_ref[...], wup_ref, wdown_ref, nf_ref[...])


def _sample_out_call(xs, yconv, inner, cross, gate, w_out, norm_mlp, w_up, w_down, norm_final):
    n = xs.shape[0]
    return pl.pallas_call(
        _sample_out_kernel,
        out_shape=jax.ShapeDtypeStruct((n, D_MODEL), F32),
        scratch_shapes=[pltpu.VMEM((n, D_MODEL), BF16)],
        compiler_params=pltpu.CompilerParams(vmem_limit_bytes=VMEM_LIMIT_BYTES),
        name="sample_out",
    )(xs, yconv, inner, cross, gate, w_out, norm_mlp, w_up, w_down, norm_final)


def kernel(x_prompt, x_sample, cache_conv, state_ret, meta_tokens, norm_mix, w_in, conv_w, w_out,
           norm_mlp, w_up, w_down, norm_final):
    bsz, seq, _ = x_prompt.shape
    n_dec = x_sample.shape[0]
    half = jnp.arange(0, HEAD_DIM, 2, dtype=F32) / HEAD_DIM
    inv_half = 1.0 / (ROPE_BASE ** half)
    inv_freq = jnp.concatenate([inv_half, inv_half])[None, :]

    w_in_b = w_in[0].astype(BF16)
    w_out_b = w_out[0].astype(BF16)
    w_up_b = w_up[0].astype(BF16)
    w_down_b = w_down[0].astype(BF16)
    nmix = norm_mix[0][None, :]
    nmlp = norm_mlp[0][None, :]
    nfin = norm_final[None, :]
    cw = conv_w[0]

    smeta, cmeta = _meta_call(meta_tokens, nmix, w_in_b, inv_freq)
    h1, conv_p, ret_p = _mixer_call(x_prompt, nmix, w_in_b, cw, w_out_b, inv_freq, smeta, cmeta)
    y_prompt = _mlp_call(h1.reshape(bsz * seq, D_MODEL), nmlp, w_up_b, w_down_b, nfin)
    y_prompt = y_prompt.reshape(bsz, seq, D_MODEL)

    xs = x_sample.reshape(n_dec, D_MODEL)
    cache = cache_conv[0].reshape(n_dec, (CONV_WIDTH - 1) * CONV_CH)
    q, kt, v, inner, gate, yconv, cnew = _sample_proj_call(xs, cache, nmix, w_in_b, cw, inv_freq)
    cross, ret_s = _sample_state_call(q, kt, v, state_ret[0])
    y_sample = _sample_out_call(xs, yconv, inner, cross, gate, w_out_b, nmlp, w_up_b, w_down_b, nfin)

    return (y_prompt,
            y_sample.reshape(n_dec, 1, D_MODEL),
            conv_p,
            ret_p,
            cnew.reshape(1, n_dec, CONV_WIDTH - 1, CONV_CH),
            ret_s[None])
```

```python
import functools

import numpy as np
import jax
import jax.numpy as jnp
from jax import lax
from jax.experimental import pallas as pl
from jax.experimental.pallas import tpu as pltpu

D_MODEL = 1024
N_META = 16
CONV_CH = 512
CONV_WIDTH = 3
RET_HEADS = 4
HEAD_DIM = 128
RET_WIDTH = RET_HEADS * HEAD_DIM
CHUNK = 128
D_FF = 4 * D_MODEL
EPS = 1e-6
ROPE_BASE = 10000.0
PAST_LEN = 16384
K_SCALE = HEAD_DIM ** -0.5

OFF_U, OFF_C, OFF_B, OFF_Q, OFF_K, OFF_V, OFF_G = 0, 512, 1024, 1536, 2048, 2560, 3072
IN_TOTAL = 3584

GAMMA = tuple(1.0 - 2.0 ** (-5.0 - h) for h in range(RET_HEADS))
LOG_GAMMA = tuple(float(np.log(g)) for g in GAMMA)

TOKEN_TILE = 512
SAMPLE_BLOCK = 8
VMEM_LIMIT_BYTES = 56 * 1024 * 1024

F32 = jnp.float32
BF16 = jnp.bfloat16


def _rmsnorm(x, gain):
    return x * lax.rsqrt(jnp.mean(x * x, axis=-1, keepdims=True) + EPS) * gain


def _dot(a, b):
    return jnp.dot(a, b, preferred_element_type=F32)


def _dot_nt(a, b):
    return lax.dot_general(a, b, (((1,), (1,)), ((), ())), preferred_element_type=F32)


def _dot_tn(a, b):
    return lax.dot_general(a, b, (((0,), (0,)), ((), ())), preferred_element_type=F32)


def _rope_tables(pos, inv_freq):
    ang = pos * inv_freq
    lane = lax.broadcasted_iota(jnp.int32, ang.shape, 1)
    sin = jnp.sin(ang)
    return jnp.cos(ang), jnp.where(lane < HEAD_DIM // 2, -sin, sin)


def _rope(x, cos, sin):
    return x * cos + pltpu.roll(x, HEAD_DIM // 2, 1) * sin


def _decay_tables(dec_ref, n_valid=CHUNK):
    row = lax.broadcasted_iota(jnp.int32, (CHUNK, CHUNK), 0).astype(F32)
    col = lax.broadcasted_iota(jnp.int32, (CHUNK, CHUNK), 1).astype(F32)
    diff = row - col
    for h in range(RET_HEADS):
        lg = LOG_GAMMA[h]
        dec_ref[h] = jnp.where(diff >= 0, jnp.exp(lg * jnp.maximum(diff, 0.0)), 0.0)
        dec_ref[RET_HEADS + h] = jnp.exp((row + 1.0) * lg)
        dec_ref[2 * RET_HEADS + h] = jnp.where(row < n_valid, jnp.exp((n_valid - 1.0 - row) * lg), 0.0)


def _meta_kernel(meta_ref, nm_ref, win_ref, invf_ref, smeta_ref, cmeta_ref, xpad, zbuf):
    xpad[...] = jnp.zeros_like(xpad)
    xpad[0:N_META, :] = meta_ref[...]
    hn = _rmsnorm(xpad[...], nm_ref[...]).astype(BF16)
    zbuf[...] = _dot(hn, win_ref[...])
    cu = zbuf[:, OFF_C:OFF_C + CONV_CH] * zbuf[:, OFF_U:OFF_U + CONV_CH]
    cmeta_ref[...] = cu[N_META - 2:N_META, :]
    row = lax.broadcasted_iota(jnp.int32, (CHUNK, HEAD_DIM), 0).astype(F32)
    cos, sin = _rope_tables(row, invf_ref[...])
    for h in range(RET_HEADS):
        kz = zbuf[:, OFF_K + h * HEAD_DIM:OFF_K + (h + 1) * HEAD_DIM]
        vz = zbuf[:, OFF_V + h * HEAD_DIM:OFF_V + (h + 1) * HEAD_DIM]
        k = _rope(kz, cos, sin) * K_SCALE
        kdec = jnp.where(row < N_META, jnp.exp((N_META - 1.0 - row) * LOG_GAMMA[h]), 0.0)
        smeta_ref[h] = _dot_tn((k * kdec).astype(BF16), vz.astype(BF16))


def _meta_call(meta, norm_mix, w_in, inv_freq):
    return pl.pallas_call(
        _meta_kernel,
        out_shape=(jax.ShapeDtypeStruct((RET_HEADS, HEAD_DIM, HEAD_DIM), F32),
                   jax.ShapeDtypeStruct((CONV_WIDTH - 1, CONV_CH), F32)),
        scratch_shapes=[pltpu.VMEM((CHUNK, D_MODEL), F32), pltpu.VMEM((CHUNK, IN_TOTAL), F32)],
        compiler_params=pltpu.CompilerParams(vmem_limit_bytes=VMEM_LIMIT_BYTES),
        name="meta_state",
    )(meta, norm_mix, w_in, inv_freq)


def _mixer_kernel(x_ref, nm_ref, win_ref, cw_ref, wout_ref, invf_ref, smeta_ref, cmeta_ref,
                  wup_ref, wdown_ref,
                  h1_ref, cstate_ref, sstate_ref, wup_bf_ref, wdown_bf_ref,
                  zbuf, cbuf, ymix, cos_sc, sin_sc, dec_sc):
    b = pl.program_id(0)
    t = pl.program_id(1)
    tm = TOKEN_TILE
    tile0 = pl.multiple_of(t * tm, tm)

    wup_bf_ref[...] = wup_ref[...].astype(BF16)
    wdown_bf_ref[...] = wdown_ref[...].astype(BF16)

    @pl.when(jnp.logical_and(t == 0, b == 0))
    def _():
        _decay_tables(dec_sc)

    @pl.when(b == 0)
    def _():
        pos = (lax.broadcasted_iota(jnp.int32, (tm, HEAD_DIM), 0) + (N_META + t * tm)).astype(F32)
        cos, sin = _rope_tables(pos, invf_ref[...])
        cos_sc[pl.ds(tile0, tm), :] = cos
        sin_sc[pl.ds(tile0, tm), :] = sin

    @pl.when(t == 0)
    def _():
        sstate_ref[0, 0] = smeta_ref[...]
        cbuf[6:8, :] = cmeta_ref[...]

    x = x_ref[0]
    hn = _rmsnorm(x, nm_ref[...]).astype(BF16)
    zbuf[...] = _dot(hn, win_ref[...])

    def conv_chunk(c):
        rows = slice(c * CHUNK, (c + 1) * CHUNK)
        cu = zbuf[rows, OFF_C:OFF_C + CONV_CH] * zbuf[rows, OFF_U:OFF_U + CONV_CH]
        cbuf[8 + c * CHUNK:8 + (c + 1) * CHUNK, :] = cu
        conv = (cw_ref[0:1, :] * cbuf[6 + c * CHUNK:6 + (c + 1) * CHUNK, :]
                + cw_ref[1:2, :] * cbuf[7 + c * CHUNK:7 + (c + 1) * CHUNK, :]
                + cw_ref[2:3, :] * cu)
        ymix[rows, 0:CONV_CH] = (zbuf[rows, OFF_B:OFF_B + CONV_CH] * conv).astype(BF16)

    def scores_and_state(c):
        rows = slice(c * CHUNK, (c + 1) * CHUNK)
        cos = cos_sc[pl.ds(tile0 + c * CHUNK, CHUNK), :]
        sin = sin_sc[pl.ds(tile0 + c * CHUNK, CHUNK), :]
        saved = []
        for h in range(RET_HEADS):
            qz = zbuf[rows, OFF_Q + h * HEAD_DIM:OFF_Q + (h + 1) * HEAD_DIM]
            kz = zbuf[rows, OFF_K + h * HEAD_DIM:OFF_K + (h + 1) * HEAD_DIM]
            vb = zbuf[rows, OFF_V + h * HEAD_DIM:OFF_V + (h + 1) * HEAD_DIM].astype(BF16)
            q = _rope(qz, cos, sin)
            k = _rope(kz, cos, sin) * K_SCALE
            qb = q.astype(BF16)
            kb = k.astype(BF16)
            state = sstate_ref[0, 0, h]
            kd = (k * dec_sc[2 * RET_HEADS + h]).astype(BF16)
            sstate_ref[0, 0, h] = state * (GAMMA[h] ** CHUNK) + _dot_tn(kd, vb)
            s = _dot_nt(qb, kb)
            cross = _dot(qb, state.astype(BF16))
            saved.append((s, cross, vb))
        return saved

    def outputs(c, saved):
        rows = slice(c * CHUNK, (c + 1) * CHUNK)
        for h in range(RET_HEADS):
            s, cross, vb = saved[h]
            gz = zbuf[rows, OFF_G + h * HEAD_DIM:OFF_G + (h + 1) * HEAD_DIM]
            p = (s * dec_sc[h]).astype(BF16)
            o = _dot(p, vb) + cross * dec_sc[RET_HEADS + h]
            on = o * lax.rsqrt(jnp.mean(o * o, axis=-1, keepdims=True) + EPS)
            ymix[rows, CONV_CH + h * HEAD_DIM:CONV_CH + (h + 1) * HEAD_DIM] = (
                on * (gz * jax.nn.sigmoid(gz))).astype(BF16)

    n_chunks = tm // CHUNK
    pending = None
    for c in range(n_chunks):
        saved = scores_and_state(c)
        conv_chunk(c)
        if pending is not None:
            outputs(c - 1, pending)
        pending = saved
    outputs(n_chunks - 1, pending)

    tail = cbuf[8 + tm - 2:8 + tm, :]
    cbuf[6:8, :] = tail
    cstate_ref[0, 0] = tail
    h1_ref[0] = x + _dot(ymix[...], wout_ref[...])


def _full(shape):
    return pl.BlockSpec(shape, lambda *_: (0,) * len(shape))


def _resident(shape):
    return pl.BlockSpec(shape, lambda *_: (0,) * len(shape), pipeline_mode=pl.Buffered(1))


def _mixer_call(x, norm_mix, w_in, conv_w, w_out, inv_freq, smeta, cmeta, w_up, w_down):
    bsz, seq, _ = x.shape
    tm = TOKEN_TILE
    n_t = seq // tm
    grid = (bsz, n_t)
    ff_slice = D_FF // (bsz * n_t)
    return pl.pallas_call(
        _mixer_kernel,
        grid=grid,
        in_specs=[
            pl.BlockSpec((1, tm, D_MODEL), lambda b, t: (b, t, 0)),
            _full((1, D_MODEL)),
            _resident((D_MODEL, IN_TOTAL)),
            _full((CONV_WIDTH, CONV_CH)),
            _resident((D_MODEL, D_MODEL)),
            _full((1, HEAD_DIM)),
            _full((RET_HEADS, HEAD_DIM, HEAD_DIM)),
            _full((CONV_WIDTH - 1, CONV_CH)),
            pl.BlockSpec((D_MODEL, ff_slice), lambda b, t: (0, b * n_t + t)),
            pl.BlockSpec((ff_slice, D_MODEL), lambda b, t: (b * n_t + t, 0)),
        ],
        out_specs=[
            pl.BlockSpec((1, tm, D_MODEL), lambda b, t: (b, t, 0)),
            pl.BlockSpec((1, 1, CONV_WIDTH - 1, CONV_CH), lambda b, t: (0, b, 0, 0)),
            pl.BlockSpec((1, 1, RET_HEADS, HEAD_DIM, HEAD_DIM), lambda b, t: (0, b, 0, 0, 0)),
            pl.BlockSpec((D_MODEL, ff_slice), lambda b, t: (0, b * n_t + t)),
            pl.BlockSpec((ff_slice, D_MODEL), lambda b, t: (b * n_t + t, 0)),
        ],
        out_shape=(
            jax.ShapeDtypeStruct((bsz, seq, D_MODEL), F32),
            jax.ShapeDtypeStruct((1, bsz, CONV_WIDTH - 1, CONV_CH), F32),
            jax.ShapeDtypeStruct((1, bsz, RET_HEADS, HEAD_DIM, HEAD_DIM), F32),
            jax.ShapeDtypeStruct((D_MODEL, D_FF), BF16),
            jax.ShapeDtypeStruct((D_FF, D_MODEL), BF16),
        ),
        scratch_shapes=[
            pltpu.VMEM((tm, IN_TOTAL), F32),
            pltpu.VMEM((tm + 8, CONV_CH), F32),
            pltpu.VMEM((tm, D_MODEL), BF16),
            pltpu.VMEM((seq, HEAD_DIM), F32),
            pltpu.VMEM((seq, HEAD_DIM), F32),
            pltpu.VMEM((3 * RET_HEADS, CHUNK, CHUNK), F32),
        ],
        compiler_params=pltpu.CompilerParams(
            dimension_semantics=("arbitrary", "arbitrary"), vmem_limit_bytes=VMEM_LIMIT_BYTES),
        name="prompt_mixer",
    )(x, norm_mix, w_in, conv_w, w_out, inv_freq, smeta, cmeta, w_up, w_down)


def _mlp_body(h, nm, wup_ref, wdown_ref, nf):
    hn = _rmsnorm(h, nm).astype(BF16)
    acc = h
    ff_block = 1024
    for c in range(D_FF // ff_block):
        up = _dot(hn, wup_ref[:, c * ff_block:(c + 1) * ff_block])
        act = jnp.square(jnp.maximum(up, 0.0)).astype(BF16)
        acc = acc + _dot(act, wdown_ref[c * ff_block:(c + 1) * ff_block, :])
    return _rmsnorm(acc, nf)


def _mlp_kernel(h_ref, nm_ref, wup_ref, wdown_ref, nf_ref, y_ref):
    y_ref[...] = _mlp_body(h_ref[...], nm_ref[...], wup_ref, wdown_ref, nf_ref[...])


def _mlp_call(h, norm_mlp, w_up, w_down, norm_final):
    rows = h.shape[0]
    tm = TOKEN_TILE
    return pl.pallas_call(
        _mlp_kernel,
        grid=(rows // tm,),
        in_specs=[
            pl.BlockSpec((tm, D_MODEL), lambda i: (i, 0)),
            _full((1, D_MODEL)),
            _resident((D_MODEL, D_FF)),
            _resident((D_FF, D_MODEL)),
            _full((1, D_MODEL)),
        ],
        out_specs=pl.BlockSpec((tm, D_MODEL), lambda i: (i, 0)),
        out_shape=jax.ShapeDtypeStruct((rows, D_MODEL), F32),
        compiler_params=pltpu.CompilerParams(
            dimension_semantics=("arbitrary",), vmem_limit_bytes=VMEM_LIMIT_BYTES),
        name="prompt_mlp",
    )(h, norm_mlp, w_up, w_down, norm_final)


def _sample_proj_kernel(x_ref, cache_ref, nm_ref, win_ref, cw_ref, invf_ref,
                        q_ref, kt_ref, v_ref, inner_ref, gate_ref, yconv_ref, cnew_ref):
    hn = _rmsnorm(x_ref[...], nm_ref[...]).astype(BF16)
    z = _dot(hn, win_ref[...])
    prev0 = cache_ref[:, 0:CONV_CH]
    prev1 = cache_ref[:, CONV_CH:2 * CONV_CH]
    cu = z[:, OFF_C:OFF_C + CONV_CH] * z[:, OFF_U:OFF_U + CONV_CH]
    conv = cw_ref[0:1, :] * prev0 + cw_ref[1:2, :] * prev1 + cw_ref[2:3, :] * cu
    yconv_ref[...] = z[:, OFF_B:OFF_B + CONV_CH] * conv
    cnew_ref[:, 0:CONV_CH] = prev1
    cnew_ref[:, CONV_CH:2 * CONV_CH] = cu
    pos = jnp.full((1, HEAD_DIM), float(PAST_LEN), F32)
    cos, sin = _rope_tables(pos, invf_ref[...])
    for h in range(RET_HEADS):
        cols = slice(h * HEAD_DIM, (h + 1) * HEAD_DIM)
        q = _rope(z[:, OFF_Q + h * HEAD_DIM:OFF_Q + (h + 1) * HEAD_DIM], cos, sin)
        k = _rope(z[:, OFF_K + h * HEAD_DIM:OFF_K + (h + 1) * HEAD_DIM], cos, sin) * K_SCALE
        v = z[:, OFF_V + h * HEAD_DIM:OFF_V + (h + 1) * HEAD_DIM]
        g = z[:, OFF_G + h * HEAD_DIM:OFF_G + (h + 1) * HEAD_DIM]
        q_ref[:, cols] = q
        kt_ref[h] = k.T
        v_ref[:, cols] = v
        inner_ref[:, cols] = jnp.sum(q * k, axis=-1, keepdims=True) * v
        gate_ref[:, cols] = g * jax.nn.sigmoid(g)


def _sample_proj_call(xs, cache, norm_mix, w_in, conv_w, inv_freq):
    n = xs.shape[0]
    wide = jax.ShapeDtypeStruct((n, RET_WIDTH), F32)
    return pl.pallas_call(
        _sample_proj_kernel,
        out_shape=(wide,
                   jax.ShapeDtypeStruct((RET_HEADS, HEAD_DIM, n), F32),
                   wide, wide, wide,
                   jax.ShapeDtypeStruct((n, CONV_CH), F32),
                   jax.ShapeDtypeStruct((n, 2 * CONV_CH), F32)),
        compiler_params=pltpu.CompilerParams(vmem_limit_bytes=VMEM_LIMIT_BYTES),
        name="sample_proj",
    )(xs, cache, norm_mix, w_in, conv_w, inv_freq)


def _sample_state_kernel(q_ref, kt_ref, v_ref, st_ref, cross_ref, new_ref):
    step = pl.program_id(0)
    lane = lax.broadcasted_iota(jnp.int32, (HEAD_DIM, HEAD_DIM), 1)
    for i in range(SAMPLE_BLOCK):
        b = step * SAMPLE_BLOCK + i
        for h in range(RET_HEADS):
            cols = slice(h * HEAD_DIM, (h + 1) * HEAD_DIM)
            state = st_ref[i, h]
            q8 = jnp.broadcast_to(q_ref[i:i + 1, cols], (8, HEAD_DIM)).astype(BF16)
            cross_ref[i:i + 1, cols] = _dot(q8, state.astype(BF16))[0:1, :] * GAMMA[h]
            kt_b = jnp.where(lane == b, kt_ref[h], 0.0).astype(BF16)
            new_ref[i, h] = state * GAMMA[h] + _dot(kt_b, v_ref[:, cols].astype(BF16))


def _sample_state_call(q, kt, v, state):
    n = q.shape[0]
    bb = SAMPLE_BLOCK
    return pl.pallas_call(
        _sample_state_kernel,
        grid=(n // bb,),
        in_specs=[
            pl.BlockSpec((bb, RET_WIDTH), lambda i: (i, 0)),
            _full((RET_HEADS, HEAD_DIM, n)),
            _full((n, RET_WIDTH)),
            pl.BlockSpec((bb, RET_HEADS, HEAD_DIM, HEAD_DIM), lambda i: (i, 0, 0, 0)),
        ],
        out_specs=[
            pl.BlockSpec((bb, RET_WIDTH), lambda i: (i, 0)),
            pl.BlockSpec((bb, RET_HEADS, HEAD_DIM, HEAD_DIM), lambda i: (i, 0, 0, 0)),
        ],
        out_shape=(jax.ShapeDtypeStruct((n, RET_WIDTH), F32),
                   jax.ShapeDtypeStruct(state.shape, F32)),
        compiler_params=pltpu.CompilerParams(
            dimension_semantics=("arbitrary",), vmem_limit_bytes=VMEM_LIMIT_BYTES),
        name="sample_state",
    )(q, kt, v, state)


def _sample_out_kernel(x_ref, yconv_ref, inner_ref, cross_ref, gate_ref, wout_ref,
                       nmlp_ref, wup_ref, wdown_ref, nf_ref, y_ref, ymix):
    ymix[:, 0:CONV_CH] = yconv_ref[...].astype(BF16)
    for h in range(RET_HEADS):
        cols = slice(h * HEAD_DIM, (h + 1) * HEAD_DIM)
        o = inner_ref[:, cols] + cross_ref[:, cols]
        on = o * lax.rsqrt(jnp.mean(o * o, axis=-1, keepdims=True) + EPS)
        ymix[:, CONV_CH + h * HEAD_DIM:CONV_CH + (h + 1) * HEAD_DIM] = (
            on * gate_ref[:, cols]).astype(BF16)
    h1 = x_ref[...] + _dot(ymix[...], wout_ref[...])
    y_ref[...] = _mlp_body(h1, nmlp_ref[...], wup_ref, wdown_ref, nf_ref[...])


def _sample_out_call(xs, yconv, inner, cross, gate, w_out, norm_mlp, w_up, w_down, norm_final):
    n = xs.shape[0]
    return pl.pallas_call(
        _sample_out_kernel,
        out_shape=jax.ShapeDtypeStruct((n, D_MODEL), F32),
        scratch_shapes=[pltpu.VMEM((n, D_MODEL), BF16)],
        compiler_params=pltpu.CompilerParams(vmem_limit_bytes=VMEM_LIMIT_BYTES),
        name="sample_out",
    )(xs, yconv, inner, cross, gate, w_out, norm_mlp, w_up, w_down, norm_final)


def kernel(x_prompt, x_sample, cache_conv, state_ret, meta_tokens, norm_mix, w_in, conv_w, w_out,
           norm_mlp, w_up, w_down, norm_final):
    bsz, seq, _ = x_prompt.shape
    n_dec = x_sample.shape[0]
    half = jnp.arange(0, HEAD_DIM, 2, dtype=F32) / HEAD_DIM
    inv_half = 1.0 / (ROPE_BASE ** half)
    inv_freq = jnp.concatenate([inv_half, inv_half])[None, :]

    w_in_b = w_in[0].astype(BF16)
    w_out_b = w_out[0].astype(BF16)
    nmix = norm_mix[0][None, :]
    nmlp = norm_mlp[0][None, :]
    nfin = norm_final[None, :]
    cw = conv_w[0]

    smeta, cmeta = _meta_call(meta_tokens, nmix, w_in_b, inv_freq)
    h1, conv_p, ret_p, w_up_b, w_down_b = _mixer_call(
        x_prompt, nmix, w_in_b, cw, w_out_b, inv_freq, smeta, cmeta, w_up[0], w_down[0])
    y_prompt = _mlp_call(h1.reshape(bsz * seq, D_MODEL), nmlp, w_up_b, w_down_b, nfin)
    y_prompt = y_prompt.reshape(bsz, seq, D_MODEL)

    xs = x_sample.reshape(n_dec, D_MODEL)
    cache = cache_conv[0].reshape(n_dec, (CONV_WIDTH - 1) * CONV_CH)
    q, kt, v, inner, gate, yconv, cnew = _sample_proj_call(xs, cache, nmix, w_in_b, cw, inv_freq)
    cross, ret_s = _sample_state_call(q, kt, v, state_ret[0])
    y_sample = _sample_out_call(xs, yconv, inner, cross, gate, w_out_b, nmlp, w_up_b, w_down_b, nfin)

    return (y_prompt,
            y_sample.reshape(n_dec, 1, D_MODEL),
            conv_p,
            ret_p,
            cnew.reshape(1, n_dec, CONV_WIDTH - 1, CONV_CH),
            ret_s[None])
```

```python
import functools

import numpy as np
import jax
import jax.numpy as jnp
from jax import lax
from jax.experimental import pallas as pl
from jax.experimental.pallas import tpu as pltpu

D_MODEL = 1024
N_META = 16
CONV_CH = 512
CONV_WIDTH = 3
RET_HEADS = 4
HEAD_DIM = 128
RET_WIDTH = RET_HEADS * HEAD_DIM
CHUNK = 128
D_FF = 4 * D_MODEL
EPS = 1e-6
ROPE_BASE = 10000.0
PAST_LEN = 16384
K_SCALE = HEAD_DIM ** -0.5

OFF_U, OFF_C, OFF_B, OFF_Q, OFF_K, OFF_V, OFF_G = 0, 512, 1024, 1536, 2048, 2560, 3072
IN_TOTAL = 3584

GAMMA = tuple(1.0 - 2.0 ** (-5.0 - h) for h in range(RET_HEADS))
LOG_GAMMA = tuple(float(np.log(g)) for g in GAMMA)

TOKEN_TILE = 512
MLP_TILE = 1024
MLP_SUBTILE = 512
SAMPLE_BLOCK = 8
VMEM_LIMIT_BYTES = 56 * 1024 * 1024

F32 = jnp.float32
BF16 = jnp.bfloat16


def _rmsnorm(x, gain):
    return x * lax.rsqrt(jnp.mean(x * x, axis=-1, keepdims=True) + EPS) * gain


def _dot(a, b):
    return jnp.dot(a, b, preferred_element_type=F32)


def _dot_nt(a, b):
    return lax.dot_general(a, b, (((1,), (1,)), ((), ())), preferred_element_type=F32)


def _dot_tn(a, b):
    return lax.dot_general(a, b, (((0,), (0,)), ((), ())), preferred_element_type=F32)


def _rope_tables(pos, inv_freq):
    ang = pos * inv_freq
    lane = lax.broadcasted_iota(jnp.int32, ang.shape, 1)
    sin = jnp.sin(ang)
    return jnp.cos(ang), jnp.where(lane < HEAD_DIM // 2, -sin, sin)


def _rope(x, cos, sin):
    return x * cos + pltpu.roll(x, HEAD_DIM // 2, 1) * sin


def _decay_tables(dec_ref, n_valid=CHUNK):
    row = lax.broadcasted_iota(jnp.int32, (CHUNK, CHUNK), 0).astype(F32)
    col = lax.broadcasted_iota(jnp.int32, (CHUNK, CHUNK), 1).astype(F32)
    diff = row - col
    for h in range(RET_HEADS):
        lg = LOG_GAMMA[h]
        dec_ref[h] = jnp.where(diff >= 0, jnp.exp(lg * jnp.maximum(diff, 0.0)), 0.0)
        dec_ref[RET_HEADS + h] = jnp.exp((row + 1.0) * lg)
        dec_ref[2 * RET_HEADS + h] = jnp.where(row < n_valid, jnp.exp((n_valid - 1.0 - row) * lg), 0.0)


def _meta_kernel(meta_ref, nm_ref, win_ref, invf_ref, smeta_ref, cmeta_ref, xpad, zbuf):
    xpad[...] = jnp.zeros_like(xpad)
    xpad[0:N_META, :] = meta_ref[...]
    hn = _rmsnorm(xpad[...], nm_ref[...]).astype(BF16)
    zbuf[...] = _dot(hn, win_ref[...])
    cu = zbuf[:, OFF_C:OFF_C + CONV_CH] * zbuf[:, OFF_U:OFF_U + CONV_CH]
    cmeta_ref[...] = cu[N_META - 2:N_META, :]
    row = lax.broadcasted_iota(jnp.int32, (CHUNK, HEAD_DIM), 0).astype(F32)
    cos, sin = _rope_tables(row, invf_ref[...])
    for h in range(RET_HEADS):
        kz = zbuf[:, OFF_K + h * HEAD_DIM:OFF_K + (h + 1) * HEAD_DIM]
        vz = zbuf[:, OFF_V + h * HEAD_DIM:OFF_V + (h + 1) * HEAD_DIM]
        k = _rope(kz, cos, sin) * K_SCALE
        kdec = jnp.where(row < N_META, jnp.exp((N_META - 1.0 - row) * LOG_GAMMA[h]), 0.0)
        smeta_ref[h] = _dot_tn((k * kdec).astype(BF16), vz.astype(BF16))


def _meta_call(meta, norm_mix, w_in, inv_freq):
    return pl.pallas_call(
        _meta_kernel,
        out_shape=(jax.ShapeDtypeStruct((RET_HEADS, HEAD_DIM, HEAD_DIM), F32),
                   jax.ShapeDtypeStruct((CONV_WIDTH - 1, CONV_CH), F32)),
        scratch_shapes=[pltpu.VMEM((CHUNK, D_MODEL), F32), pltpu.VMEM((CHUNK, IN_TOTAL), F32)],
        compiler_params=pltpu.CompilerParams(vmem_limit_bytes=VMEM_LIMIT_BYTES),
        name="meta_state",
    )(meta, norm_mix, w_in, inv_freq)


def _mixer_kernel(x_ref, nm_ref, win_ref, cw_ref, wout_ref, invf_ref, smeta_ref, cmeta_ref,
                  wup_ref, wdown_ref,
                  h1_ref, cstate_ref, sstate_ref, wup_bf_ref, wdown_bf_ref,
                  zbuf, cbuf, ymix, cos_sc, sin_sc, dec_sc):
    b = pl.program_id(0)
    t = pl.program_id(1)
    tm = TOKEN_TILE
    tile0 = pl.multiple_of(t * tm, tm)

    wup_bf_ref[...] = wup_ref[...].astype(BF16)
    wdown_bf_ref[...] = wdown_ref[...].astype(BF16)

    @pl.when(jnp.logical_and(t == 0, b == 0))
    def _():
        _decay_tables(dec_sc)

    @pl.when(b == 0)
    def _():
        pos = (lax.broadcasted_iota(jnp.int32, (tm, HEAD_DIM), 0) + (N_META + t * tm)).astype(F32)
        cos, sin = _rope_tables(pos, invf_ref[...])
        cos_sc[pl.ds(tile0, tm), :] = cos
        sin_sc[pl.ds(tile0, tm), :] = sin

    @pl.when(t == 0)
    def _():
        sstate_ref[0, 0] = smeta_ref[...]
        cbuf[6:8, :] = cmeta_ref[...]

    x = x_ref[0]
    hn = _rmsnorm(x, nm_ref[...]).astype(BF16)
    zbuf[...] = _dot(hn, win_ref[...])

    def conv_chunk(c):
        rows = slice(c * CHUNK, (c + 1) * CHUNK)
        cu = zbuf[rows, OFF_C:OFF_C + CONV_CH] * zbuf[rows, OFF_U:OFF_U + CONV_CH]
        cbuf[8 + c * CHUNK:8 + (c + 1) * CHUNK, :] = cu
        conv = (cw_ref[0:1, :] * cbuf[6 + c * CHUNK:6 + (c + 1) * CHUNK, :]
                + cw_ref[1:2, :] * cbuf[7 + c * CHUNK:7 + (c + 1) * CHUNK, :]
                + cw_ref[2:3, :] * cu)
        ymix[rows, 0:CONV_CH] = (zbuf[rows, OFF_B:OFF_B + CONV_CH] * conv).astype(BF16)

    def scores_and_state(c):
        rows = slice(c * CHUNK, (c + 1) * CHUNK)
        cos = cos_sc[pl.ds(tile0 + c * CHUNK, CHUNK), :]
        sin = sin_sc[pl.ds(tile0 + c * CHUNK, CHUNK), :]
        saved = []
        for h in range(RET_HEADS):
            qz = zbuf[rows, OFF_Q + h * HEAD_DIM:OFF_Q + (h + 1) * HEAD_DIM]
            kz = zbuf[rows, OFF_K + h * HEAD_DIM:OFF_K + (h + 1) * HEAD_DIM]
            vb = zbuf[rows, OFF_V + h * HEAD_DIM:OFF_V + (h + 1) * HEAD_DIM].astype(BF16)
            q = _rope(qz, cos, sin)
            k = _rope(kz, cos, sin) * K_SCALE
            qb = q.astype(BF16)
            kb = k.astype(BF16)
            state = sstate_ref[0, 0, h]
            kd = (k * dec_sc[2 * RET_HEADS + h]).astype(BF16)
            sstate_ref[0, 0, h] = state * (GAMMA[h] ** CHUNK) + _dot_tn(kd, vb)
            s = _dot_nt(qb, kb)
            cross = _dot(qb, state.astype(BF16))
            saved.append((s, cross, vb))
        return saved

    def outputs(c, saved):
        rows = slice(c * CHUNK, (c + 1) * CHUNK)
        for h in range(RET_HEADS):
            s, cross, vb = saved[h]
            gz = zbuf[rows, OFF_G + h * HEAD_DIM:OFF_G + (h + 1) * HEAD_DIM]
            p = (s * dec_sc[h]).astype(BF16)
            o = _dot(p, vb) + cross * dec_sc[RET_HEADS + h]
            on = o * lax.rsqrt(jnp.mean(o * o, axis=-1, keepdims=True) + EPS)
            ymix[rows, CONV_CH + h * HEAD_DIM:CONV_CH + (h + 1) * HEAD_DIM] = (
                on * (gz * jax.nn.sigmoid(gz))).astype(BF16)

    n_chunks = tm // CHUNK
    pending = None
    for c in range(n_chunks):
        saved = scores_and_state(c)
        conv_chunk(c)
        if pending is not None:
            outputs(c - 1, pending)
        pending = saved
    outputs(n_chunks - 1, pending)

    tail = cbuf[8 + tm - 2:8 + tm, :]
    cbuf[6:8, :] = tail
    cstate_ref[0, 0] = tail
    h1_ref[0] = x + _dot(ymix[...], wout_ref[...])


def _full(shape):
    return pl.BlockSpec(shape, lambda *_: (0,) * len(shape))


def _resident(shape):
    return pl.BlockSpec(shape, lambda *_: (0,) * len(shape), pipeline_mode=pl.Buffered(1))


def _mixer_call(x, norm_mix, w_in, conv_w, w_out, inv_freq, smeta, cmeta, w_up, w_down):
    bsz, seq, _ = x.shape
    tm = TOKEN_TILE
    n_t = seq // tm
    grid = (bsz, n_t)
    ff_slice = D_FF // (bsz * n_t)
    return pl.pallas_call(
        _mixer_kernel,
        grid=grid,
        in_specs=[
            pl.BlockSpec((1, tm, D_MODEL), lambda b, t: (b, t, 0)),
            _full((1, D_MODEL)),
            _resident((D_MODEL, IN_TOTAL)),
            _full((CONV_WIDTH, CONV_CH)),
            _resident((D_MODEL, D_MODEL)),
            _full((1, HEAD_DIM)),
            _full((RET_HEADS, HEAD_DIM, HEAD_DIM)),
            _full((CONV_WIDTH - 1, CONV_CH)),
            pl.BlockSpec((D_MODEL, ff_slice), lambda b, t: (0, b * n_t + t)),
            pl.BlockSpec((ff_slice, D_MODEL), lambda b, t: (b * n_t + t, 0)),
        ],
        out_specs=[
            pl.BlockSpec((1, tm, D_MODEL), lambda b, t: (b, t, 0)),
            pl.BlockSpec((1, 1, CONV_WIDTH - 1, CONV_CH), lambda b, t: (0, b, 0, 0)),
            pl.BlockSpec((1, 1, RET_HEADS, HEAD_DIM, HEAD_DIM), lambda b, t: (0, b, 0, 0, 0)),
            pl.BlockSpec((D_MODEL, ff_slice), lambda b, t: (0, b * n_t + t)),
            pl.BlockSpec((ff_slice, D_MODEL), lambda b, t: (b * n_t + t, 0)),
        ],
        out_shape=(
            jax.ShapeDtypeStruct((bsz, seq, D_MODEL), F32),
            jax.ShapeDtypeStruct((1, bsz, CONV_WIDTH - 1, CONV_CH), F32),
            jax.ShapeDtypeStruct((1, bsz, RET_HEADS, HEAD_DIM, HEAD_DIM), F32),
            jax.ShapeDtypeStruct((D_MODEL, D_FF), BF16),
            jax.ShapeDtypeStruct((D_FF, D_MODEL), BF16),
        ),
        scratch_shapes=[
            pltpu.VMEM((tm, IN_TOTAL), F32),
            pltpu.VMEM((tm + 8, CONV_CH), F32),
            pltpu.VMEM((tm, D_MODEL), BF16),
            pltpu.VMEM((seq, HEAD_DIM), F32),
            pltpu.VMEM((seq, HEAD_DIM), F32),
            pltpu.VMEM((3 * RET_HEADS, CHUNK, CHUNK), F32),
        ],
        compiler_params=pltpu.CompilerParams(
            dimension_semantics=("arbitrary", "arbitrary"), vmem_limit_bytes=VMEM_LIMIT_BYTES),
        name="prompt_mixer",
    )(x, norm_mix, w_in, conv_w, w_out, inv_freq, smeta, cmeta, w_up, w_down)


def _mlp_body(h, nm, wup_ref, wdown_ref, nf):
    hn = _rmsnorm(h, nm).astype(BF16)
    acc = h
    ff_block = 1024
    for c in range(D_FF // ff_block):
        up = _dot(hn, wup_ref[:, c * ff_block:(c + 1) * ff_block])
        act = jnp.square(jnp.maximum(up, 0.0)).astype(BF16)
        acc = acc + _dot(act, wdown_ref[c * ff_block:(c + 1) * ff_block, :])
    return _rmsnorm(acc, nf)


def _mlp_kernel(h_ref, nm_ref, wup_ref, wdown_ref, nf_ref, y_ref, hn_sc, acc_sc):
    n_sub = MLP_TILE // MLP_SUBTILE
    ff_block = 1024
    n_ff = D_FF // ff_block

    def rows(s):
        return slice(s * MLP_SUBTILE, (s + 1) * MLP_SUBTILE)

    hn_sc[0] = _rmsnorm(h_ref[rows(0), :], nm_ref[...]).astype(BF16)
    for s in range(n_sub):
        hn = hn_sc[s % 2]
        acc = h_ref[rows(s), :]
        for c in range(n_ff):
            up = _dot(hn, wup_ref[:, c * ff_block:(c + 1) * ff_block])
            act = jnp.square(jnp.maximum(up, 0.0)).astype(BF16)
            acc = acc + _dot(act, wdown_ref[c * ff_block:(c + 1) * ff_block, :])
            if c == 0 and s > 0:
                y_ref[rows(s - 1), :] = _rmsnorm(acc_sc[...], nf_ref[...])
            if c == 1 and s + 1 < n_sub:
                hn_sc[(s + 1) % 2] = _rmsnorm(h_ref[rows(s + 1), :], nm_ref[...]).astype(BF16)
        if s + 1 < n_sub:
            acc_sc[...] = acc
        else:
            y_ref[rows(s), :] = _rmsnorm(acc, nf_ref[...])


def _mlp_call(h, norm_mlp, w_up, w_down, norm_final):
    rows = h.shape[0]
    tm = MLP_TILE
    return pl.pallas_call(
        _mlp_kernel,
        grid=(rows // tm,),
        in_specs=[
            pl.BlockSpec((tm, D_MODEL), lambda i: (i, 0)),
            _full((1, D_MODEL)),
            _resident((D_MODEL, D_FF)),
            _resident((D_FF, D_MODEL)),
            _full((1, D_MODEL)),
        ],
        out_specs=pl.BlockSpec((tm, D_MODEL), lambda i: (i, 0)),
        out_shape=jax.ShapeDtypeStruct((rows, D_MODEL), F32),
        scratch_shapes=[pltpu.VMEM((2, MLP_SUBTILE, D_MODEL), BF16),
                        pltpu.VMEM((MLP_SUBTILE, D_MODEL), F32)],
        compiler_params=pltpu.CompilerParams(
            dimension_semantics=("arbitrary",), vmem_limit_bytes=VMEM_LIMIT_BYTES),
        name="prompt_mlp",
    )(h, norm_mlp, w_up, w_down, norm_final)


def _sample_proj_kernel(x_ref, cache_ref, nm_ref, win_ref, cw_ref, invf_ref,
                        q_ref, kt_ref, v_ref, inner_ref, gate_ref, yconv_ref, cnew_ref):
    hn = _rmsnorm(x_ref[...], nm_ref[...]).astype(BF16)
    z = _dot(hn, win_ref[...])
    prev0 = cache_ref[:, 0:CONV_CH]
    prev1 = cache_ref[:, CONV_CH:2 * CONV_CH]
    cu = z[:, OFF_C:OFF_C + CONV_CH] * z[:, OFF_U:OFF_U + CONV_CH]
    conv = cw_ref[0:1, :] * prev0 + cw_ref[1:2, :] * prev1 + cw_ref[2:3, :] * cu
    yconv_ref[...] = z[:, OFF_B:OFF_B + CONV_CH] * conv
    cnew_ref[:, 0:CONV_CH] = prev1
    cnew_ref[:, CONV_CH:2 * CONV_CH] = cu
    pos = jnp.full((1, HEAD_DIM), float(PAST_LEN), F32)
    cos, sin = _rope_tables(pos, invf_ref[...])
    for h in range(RET_HEADS):
        cols = slice(h * HEAD_DIM, (h + 1) * HEAD_DIM)
        q = _rope(z[:, OFF_Q + h * HEAD_DIM:OFF_Q + (h + 1) * HEAD_DIM], cos, sin)
        k = _rope(z[:, OFF_K + h * HEAD_DIM:OFF_K + (h + 1) * HEAD_DIM], cos, sin) * K_SCALE
        v = z[:, OFF_V + h * HEAD_DIM:OFF_V + (h + 1) * HEAD_DIM]
        g = z[:, OFF_G + h * HEAD_DIM:OFF_G + (h + 1) * HEAD_DIM]
        q_ref[:, cols] = q
        kt_ref[h] = k.T
        v_ref[:, cols] = v
        inner_ref[:, cols] = jnp.sum(q * k, axis=-1, keepdims=True) * v
        gate_ref[:, cols] = g * jax.nn.sigmoid(g)


def _sample_proj_call(xs, cache, norm_mix, w_in, conv_w, inv_freq):
    n = xs.shape[0]
    wide = jax.ShapeDtypeStruct((n, RET_WIDTH), F32)
    return pl.pallas_call(
        _sample_proj_kernel,
        out_shape=(wide,
                   jax.ShapeDtypeStruct((RET_HEADS, HEAD_DIM, n), F32),
                   wide, wide, wide,
                   jax.ShapeDtypeStruct((n, CONV_CH), F32),
                   jax.ShapeDtypeStruct((n, 2 * CONV_CH), F32)),
        compiler_params=pltpu.CompilerParams(vmem_limit_bytes=VMEM_LIMIT_BYTES),
        name="sample_proj",
    )(xs, cache, norm_mix, w_in, conv_w, inv_freq)


def _sample_state_kernel(q_ref, kt_ref, v_ref, st_ref, cross_ref, new_ref):
    step = pl.program_id(0)
    lane = lax.broadcasted_iota(jnp.int32, (HEAD_DIM, HEAD_DIM), 1)
    for i in range(SAMPLE_BLOCK):
        b = step * SAMPLE_BLOCK + i
        for h in range(RET_HEADS):
            cols = slice(h * HEAD_DIM, (h + 1) * HEAD_DIM)
            state = st_ref[i, h]
            q8 = jnp.broadcast_to(q_ref[i:i + 1, cols], (8, HEAD_DIM)).astype(BF16)
            cross_ref[i:i + 1, cols] = _dot(q8, state.astype(BF16))[0:1, :] * GAMMA[h]
            kt_b = jnp.where(lane == b, kt_ref[h], 0.0).astype(BF16)
            new_ref[i, h] = state * GAMMA[h] + _dot(kt_b, v_ref[:, cols].astype(BF16))


def _sample_state_call(q, kt, v, state):
    n = q.shape[0]
    bb = SAMPLE_BLOCK
    return pl.pallas_call(
        _sample_state_kernel,
        grid=(n // bb,),
        in_specs=[
            pl.BlockSpec((bb, RET_WIDTH), lambda i: (i, 0)),
            _full((RET_HEADS, HEAD_DIM, n)),
            _full((n, RET_WIDTH)),
            pl.BlockSpec((bb, RET_HEADS, HEAD_DIM, HEAD_DIM), lambda i: (i, 0, 0, 0)),
        ],
        out_specs=[
            pl.BlockSpec((bb, RET_WIDTH), lambda i: (i, 0)),
            pl.BlockSpec((bb, RET_HEADS, HEAD_DIM, HEAD_DIM), lambda i: (i, 0, 0, 0)),
        ],
        out_shape=(jax.ShapeDtypeStruct((n, RET_WIDTH), F32),
                   jax.ShapeDtypeStruct(state.shape, F32)),
        compiler_params=pltpu.CompilerParams(
            dimension_semantics=("arbitrary",), vmem_limit_bytes=VMEM_LIMIT_BYTES),
        name="sample_state",
    )(q, kt, v, state)


def _sample_out_kernel(x_ref, yconv_ref, inner_ref, cross_ref, gate_ref, wout_ref,
                       nmlp_ref, wup_ref, wdown_ref, nf_ref, y_ref, ymix):
    ymix[:, 0:CONV_CH] = yconv_ref[...].astype(BF16)
    for h in range(RET_HEADS):
        cols = slice(h * HEAD_DIM, (h + 1) * HEAD_DIM)
        o = inner_ref[:, cols] + cross_ref[:, cols]
        on = o * lax.rsqrt(jnp.mean(o * o, axis=-1, keepdims=True) + EPS)
        ymix[:, CONV_CH + h * HEAD_DIM:CONV_CH + (h + 1) * HEAD_DIM] = (
            on * gate_ref[:, cols]).astype(BF16)
    h1 = x_ref[...] + _dot(ymix[...], wout_ref[...])
    y_ref[...] = _mlp_body(h1, nmlp_ref[...], wup_ref, wdown_ref, nf_ref[...])


def _sample_out_call(xs, yconv, inner, cross, gate, w_out, norm_mlp, w_up, w_down, norm_final):
    n = xs.shape[0]
    return pl.pallas_call(
        _sample_out_kernel,
        out_shape=jax.ShapeDtypeStruct((n, D_MODEL), F32),
        scratch_shapes=[pltpu.VMEM((n, D_MODEL), BF16)],
        compiler_params=pltpu.CompilerParams(vmem_limit_bytes=VMEM_LIMIT_BYTES),
        name="sample_out",
    )(xs, yconv, inner, cross, gate, w_out, norm_mlp, w_up, w_down, norm_final)


def kernel(x_prompt, x_sample, cache_conv, state_ret, meta_tokens, norm_mix, w_in, conv_w, w_out,
           norm_mlp, w_up, w_down, norm_final):
    bsz, seq, _ = x_prompt.shape
    n_dec = x_sample.shape[0]
    half = jnp.arange(0, HEAD_DIM, 2, dtype=F32) / HEAD_DIM
    inv_half = 1.0 / (ROPE_BASE ** half)
    inv_freq = jnp.concatenate([inv_half, inv_half])[None, :]

    w_in_b = w_in[0].astype(BF16)
    w_out_b = w_out[0].astype(BF16)
    nmix = norm_mix[0][None, :]
    nmlp = norm_mlp[0][None, :]
    nfin = norm_final[None, :]
    cw = conv_w[0]

    smeta, cmeta = _meta_call(meta_tokens, nmix, w_in_b, inv_freq)
    h1, conv_p, ret_p, w_up_b, w_down_b = _mixer_call(
        x_prompt, nmix, w_in_b, cw, w_out_b, inv_freq, smeta, cmeta, w_up[0], w_down[0])
    y_prompt = _mlp_call(h1.reshape(bsz * seq, D_MODEL), nmlp, w_up_b, w_down_b, nfin)
    y_prompt = y_prompt.reshape(bsz, seq, D_MODEL)

    xs = x_sample.reshape(n_dec, D_MODEL)
    cache = cache_conv[0].reshape(n_dec, (CONV_WIDTH - 1) * CONV_CH)
    q, kt, v, inner, gate, yconv, cnew = _sample_proj_call(xs, cache, nmix, w_in_b, cw, inv_freq)
    cross, ret_s = _sample_state_call(q, kt, v, state_ret[0])
    y_sample = _sample_out_call(xs, yconv, inner, cross, gate, w_out_b, nmlp, w_up_b, w_down_b, nfin)

    return (y_prompt,
            y_sample.reshape(n_dec, 1, D_MODEL),
            conv_p,
            ret_p,
            cnew.reshape(1, n_dec, CONV_WIDTH - 1, CONV_CH),
            ret_s[None])
```

```python
import functools

import numpy as np
import jax
import jax.numpy as jnp
from jax import lax
from jax.experimental import pallas as pl
from jax.experimental.pallas import tpu as pltpu

D_MODEL = 1024
N_META = 16
CONV_CH = 512
CONV_WIDTH = 3
RET_HEADS = 4
HEAD_DIM = 128
RET_WIDTH = RET_HEADS * HEAD_DIM
CHUNK = 128
D_FF = 4 * D_MODEL
EPS = 1e-6
ROPE_BASE = 10000.0
PAST_LEN = 16384
K_SCALE = HEAD_DIM ** -0.5

OFF_U, OFF_C, OFF_B, OFF_Q, OFF_K, OFF_V, OFF_G = 0, 512, 1024, 1536, 2048, 2560, 3072
IN_TOTAL = 3584

GAMMA = tuple(1.0 - 2.0 ** (-5.0 - h) for h in range(RET_HEADS))
LOG_GAMMA = tuple(float(np.log(g)) for g in GAMMA)

TOKEN_TILE = 1024
MIX_SUBTILE = 512
PROJ_BLOCK = 512
OUT_BLOCK = 256
MLP_TILE = 1024
MLP_SUBTILE = 512
SAMPLE_BLOCK = 8
VMEM_LIMIT_BYTES = 60 * 1024 * 1024

F32 = jnp.float32
BF16 = jnp.bfloat16


def _rmsnorm(x, gain):
    return x * lax.rsqrt(jnp.mean(x * x, axis=-1, keepdims=True) + EPS) * gain


def _dot(a, b):
    return jnp.dot(a, b, preferred_element_type=F32)


def _dot_nt(a, b):
    return lax.dot_general(a, b, (((1,), (1,)), ((), ())), preferred_element_type=F32)


def _dot_tn(a, b):
    return lax.dot_general(a, b, (((0,), (0,)), ((), ())), preferred_element_type=F32)


def _rope_tables(pos, inv_freq):
    ang = pos * inv_freq
    lane = lax.broadcasted_iota(jnp.int32, ang.shape, 1)
    sin = jnp.sin(ang)
    return jnp.cos(ang), jnp.where(lane < HEAD_DIM // 2, -sin, sin)


def _rope(x, cos, sin):
    return x * cos + pltpu.roll(x, HEAD_DIM // 2, 1) * sin


def _decay_tables(dec_ref, n_valid=CHUNK):
    row = lax.broadcasted_iota(jnp.int32, (CHUNK, CHUNK), 0).astype(F32)
    col = lax.broadcasted_iota(jnp.int32, (CHUNK, CHUNK), 1).astype(F32)
    diff = row - col
    for h in range(RET_HEADS):
        lg = LOG_GAMMA[h]
        dec_ref[h] = jnp.where(diff >= 0, jnp.exp(lg * jnp.maximum(diff, 0.0)), 0.0)
        dec_ref[RET_HEADS + h] = jnp.exp((row + 1.0) * lg)
        dec_ref[2 * RET_HEADS + h] = jnp.where(row < n_valid, jnp.exp((n_valid - 1.0 - row) * lg), 0.0)


def _meta_kernel(meta_ref, nm_ref, win_ref, invf_ref, smeta_ref, cmeta_ref, xpad, zbuf):
    xpad[...] = jnp.zeros_like(xpad)
    xpad[0:N_META, :] = meta_ref[...]
    hn = _rmsnorm(xpad[...], nm_ref[...]).astype(BF16)
    zbuf[...] = _dot(hn, win_ref[...])
    cu = zbuf[:, OFF_C:OFF_C + CONV_CH] * zbuf[:, OFF_U:OFF_U + CONV_CH]
    cmeta_ref[...] = cu[N_META - 2:N_META, :]
    row = lax.broadcasted_iota(jnp.int32, (CHUNK, HEAD_DIM), 0).astype(F32)
    cos, sin = _rope_tables(row, invf_ref[...])
    for h in range(RET_HEADS):
        kz = zbuf[:, OFF_K + h * HEAD_DIM:OFF_K + (h + 1) * HEAD_DIM]
        vz = zbuf[:, OFF_V + h * HEAD_DIM:OFF_V + (h + 1) * HEAD_DIM]
        k = _rope(kz, cos, sin) * K_SCALE
        kdec = jnp.where(row < N_META, jnp.exp((N_META - 1.0 - row) * LOG_GAMMA[h]), 0.0)
        smeta_ref[h] = _dot_tn((k * kdec).astype(BF16), vz.astype(BF16))


def _meta_call(meta, norm_mix, w_in, inv_freq):
    return pl.pallas_call(
        _meta_kernel,
        out_shape=(jax.ShapeDtypeStruct((RET_HEADS, HEAD_DIM, HEAD_DIM), F32),
                   jax.ShapeDtypeStruct((CONV_WIDTH - 1, CONV_CH), F32)),
        scratch_shapes=[pltpu.VMEM((CHUNK, D_MODEL), F32), pltpu.VMEM((CHUNK, IN_TOTAL), F32)],
        compiler_params=pltpu.CompilerParams(vmem_limit_bytes=VMEM_LIMIT_BYTES),
        name="meta_state",
    )(meta, norm_mix, w_in, inv_freq)


def _mixer_kernel(x_ref, nm_ref, win_ref, cw_ref, wout_ref, invf_ref, smeta_ref, cmeta_ref,
                  wup_ref, wdown_ref,
                  h1_ref, cstate_ref, sstate_ref, wup_bf_ref, wdown_bf_ref,
                  zbuf, cbuf, ymix, hn_sc, cos_sc, sin_sc, dec_sc):
    b = pl.program_id(0)
    t = pl.program_id(1)
    tm = TOKEN_TILE
    tile0 = pl.multiple_of(t * tm, tm)

    wup_bf_ref[...] = wup_ref[...].astype(BF16)
    wdown_bf_ref[...] = wdown_ref[...].astype(BF16)

    @pl.when(jnp.logical_and(t == 0, b == 0))
    def _():
        _decay_tables(dec_sc)

    @pl.when(b == 0)
    def _():
        pos = (lax.broadcasted_iota(jnp.int32, (tm, HEAD_DIM), 0) + (N_META + t * tm)).astype(F32)
        cos, sin = _rope_tables(pos, invf_ref[...])
        cos_sc[pl.ds(tile0, tm), :] = cos
        sin_sc[pl.ds(tile0, tm), :] = sin

    @pl.when(t == 0)
    def _():
        sstate_ref[0, 0] = smeta_ref[...]
        cbuf[6:8, :] = cmeta_ref[...]

    n_sub = tm // MIX_SUBTILE
    chunks_per_sub = MIX_SUBTILE // CHUNK

    def sub_rows(s):
        return slice(s * MIX_SUBTILE, (s + 1) * MIX_SUBTILE)

    def norm_in(s):
        hn_sc[s % 2] = _rmsnorm(x_ref[0, sub_rows(s), :], nm_ref[...]).astype(BF16)

    def project(s, p):
        cols = slice(p * PROJ_BLOCK, (p + 1) * PROJ_BLOCK)
        zbuf[s % 2, :, cols] = _dot(hn_sc[s % 2], win_ref[:, cols])

    def out_project(s, p):
        cols = slice(p * OUT_BLOCK, (p + 1) * OUT_BLOCK)
        h1_ref[0, sub_rows(s), cols] = (x_ref[0, sub_rows(s), cols]
                                        + _dot(ymix[s % 2], wout_ref[:, cols]))

    def conv_chunk(s, c):
        z = zbuf.at[s % 2]
        rows = slice(c * CHUNK, (c + 1) * CHUNK)
        g0 = (s * chunks_per_sub + c) * CHUNK
        cu = z[rows, OFF_C:OFF_C + CONV_CH] * z[rows, OFF_U:OFF_U + CONV_CH]
        cbuf[8 + g0:8 + g0 + CHUNK, :] = cu
        conv = (cw_ref[0:1, :] * cbuf[6 + g0:6 + g0 + CHUNK, :]
                + cw_ref[1:2, :] * cbuf[7 + g0:7 + g0 + CHUNK, :]
                + cw_ref[2:3, :] * cu)
        ymix[s % 2, rows, 0:CONV_CH] = (z[rows, OFF_B:OFF_B + CONV_CH] * conv).astype(BF16)

    saved = {}

    def scores_and_state(s, c):
        z = zbuf.at[s % 2]
        rows = slice(c * CHUNK, (c + 1) * CHUNK)
        g0 = (s * chunks_per_sub + c) * CHUNK
        cos = cos_sc[pl.ds(tile0 + g0, CHUNK), :]
        sin = sin_sc[pl.ds(tile0 + g0, CHUNK), :]
        qbs, kbs, vbs, states = [], [], [], []
        for h in range(RET_HEADS):
            qz = z[rows, OFF_Q + h * HEAD_DIM:OFF_Q + (h + 1) * HEAD_DIM]
            kz = z[rows, OFF_K + h * HEAD_DIM:OFF_K + (h + 1) * HEAD_DIM]
            vb = z[rows, OFF_V + h * HEAD_DIM:OFF_V + (h + 1) * HEAD_DIM].astype(BF16)
            k = _rope(kz, cos, sin) * K_SCALE
            state = sstate_ref[0, 0, h]
            kd = (k * dec_sc[2 * RET_HEADS + h]).astype(BF16)
            sstate_ref[0, 0, h] = state * (GAMMA[h] ** CHUNK) + _dot_tn(kd, vb)
            qbs.append(_rope(qz, cos, sin).astype(BF16))
            kbs.append(k.astype(BF16))
            vbs.append(vb)
            states.append(state)
        scores = [_dot_nt(qbs[h], kbs[h]) for h in range(RET_HEADS)]
        cross = [_dot(qbs[h], states[h].astype(BF16)) for h in range(RET_HEADS)]
        saved[(s, c)] = (scores, cross, vbs)

    def outputs(s, c):
        z = zbuf.at[s % 2]
        rows = slice(c * CHUNK, (c + 1) * CHUNK)
        scores, cross, vbs = saved.pop((s, c))
        for h in range(RET_HEADS):
            gz = z[rows, OFF_G + h * HEAD_DIM:OFF_G + (h + 1) * HEAD_DIM]
            p = (scores[h] * dec_sc[h]).astype(BF16)
            o = _dot(p, vbs[h]) + cross[h] * dec_sc[RET_HEADS + h]
            on = o * lax.rsqrt(jnp.mean(o * o, axis=-1, keepdims=True) + EPS)
            ymix[s % 2, rows, CONV_CH + h * HEAD_DIM:CONV_CH + (h + 1) * HEAD_DIM] = (
                on * (gz * jax.nn.sigmoid(gz))).astype(BF16)

    norm_in(0)
    for p in range(IN_TOTAL // PROJ_BLOCK):
        project(0, p)
    for s in range(n_sub):
        fillers = []
        if s > 0:
            fillers += [functools.partial(out_project, s - 1, p)
                        for p in range(D_MODEL // OUT_BLOCK)]
        if s + 1 < n_sub:
            norm_in(s + 1)
            fillers += [functools.partial(project, s + 1, p)
                        for p in range(IN_TOTAL // PROJ_BLOCK)]
        steps = []
        for c in range(chunks_per_sub):
            steps.append(functools.partial(scores_and_state, s, c))
            if c > 0:
                steps.append(functools.partial(outputs, s, c - 1))
        steps.append(functools.partial(outputs, s, chunks_per_sub - 1))
        n_f, n_s = len(fillers), len(steps)
        placed = 0
        for i, step in enumerate(steps):
            while placed < n_f and placed * n_s <= i * n_f:
                fillers[placed]()
                placed += 1
            step()
            if i % 2 == 0:
                conv_chunk(s, i // 2)
        while placed < n_f:
            fillers[placed]()
            placed += 1
    for p in range(D_MODEL // OUT_BLOCK):
        out_project(n_sub - 1, p)

    tail = cbuf[8 + tm - 2:8 + tm, :]
    cbuf[6:8, :] = tail
    cstate_ref[0, 0] = tail


def _full(shape):
    return pl.BlockSpec(shape, lambda *_: (0,) * len(shape))


def _resident(shape):
    return pl.BlockSpec(shape, lambda *_: (0,) * len(shape), pipeline_mode=pl.Buffered(1))


def _mixer_call(x, norm_mix, w_in, conv_w, w_out, inv_freq, smeta, cmeta, w_up, w_down):
    bsz, seq, _ = x.shape
    tm = TOKEN_TILE
    n_t = seq // tm
    grid = (bsz, n_t)
    ff_slice = D_FF // (bsz * n_t)
    return pl.pallas_call(
        _mixer_kernel,
        grid=grid,
        in_specs=[
            pl.BlockSpec((1, tm, D_MODEL), lambda b, t: (b, t, 0)),
            _full((1, D_MODEL)),
            _resident((D_MODEL, IN_TOTAL)),
            _full((CONV_WIDTH, CONV_CH)),
            _resident((D_MODEL, D_MODEL)),
            _full((1, HEAD_DIM)),
            _full((RET_HEADS, HEAD_DIM, HEAD_DIM)),
            _full((CONV_WIDTH - 1, CONV_CH)),
            pl.BlockSpec((D_MODEL, ff_slice), lambda b, t: (0, b * n_t + t)),
            pl.BlockSpec((ff_slice, D_MODEL), lambda b, t: (b * n_t + t, 0)),
        ],
        out_specs=[
            pl.BlockSpec((1, tm, D_MODEL), lambda b, t: (b, t, 0)),
            pl.BlockSpec((1, 1, CONV_WIDTH - 1, CONV_CH), lambda b, t: (0, b, 0, 0)),
            pl.BlockSpec((1, 1, RET_HEADS, HEAD_DIM, HEAD_DIM), lambda b, t: (0, b, 0, 0, 0)),
            pl.BlockSpec((D_MODEL, ff_slice), lambda b, t: (0, b * n_t + t)),
            pl.BlockSpec((ff_slice, D_MODEL), lambda b, t: (b * n_t + t, 0)),
        ],
        out_shape=(
            jax.ShapeDtypeStruct((bsz, seq, D_MODEL), F32),
            jax.ShapeDtypeStruct((1, bsz, CONV_WIDTH - 1, CONV_CH), F32),
            jax.ShapeDtypeStruct((1, bsz, RET_HEADS, HEAD_DIM, HEAD_DIM), F32),
            jax.ShapeDtypeStruct((D_MODEL, D_FF), BF16),
            jax.ShapeDtypeStruct((D_FF, D_MODEL), BF16),
        ),
        scratch_shapes=[
            pltpu.VMEM((2, MIX_SUBTILE, IN_TOTAL), F32),
            pltpu.VMEM((tm + 8, CONV_CH), F32),
            pltpu.VMEM((2, MIX_SUBTILE, D_MODEL), BF16),
            pltpu.VMEM((2, MIX_SUBTILE, D_MODEL), BF16),
            pltpu.VMEM((seq, HEAD_DIM), F32),
            pltpu.VMEM((seq, HEAD_DIM), F32),
            pltpu.VMEM((3 * RET_HEADS, CHUNK, CHUNK), F32),
        ],
        compiler_params=pltpu.CompilerParams(
            dimension_semantics=("arbitrary", "arbitrary"), vmem_limit_bytes=VMEM_LIMIT_BYTES),
        name="prompt_mixer",
    )(x, norm_mix, w_in, conv_w, w_out, inv_freq, smeta, cmeta, w_up, w_down)


def _mlp_body(h, nm, wup_ref, wdown_ref, nf):
    hn = _rmsnorm(h, nm).astype(BF16)
    acc = h
    ff_block = 1024
    for c in range(D_FF // ff_block):
        up = _dot(hn, wup_ref[:, c * ff_block:(c + 1) * ff_block])
        act = jnp.square(jnp.maximum(up, 0.0)).astype(BF16)
        acc = acc + _dot(act, wdown_ref[c * ff_block:(c + 1) * ff_block, :])
    return _rmsnorm(acc, nf)


def _mlp_kernel(h_ref, nm_ref, wup_ref, wdown_ref, nf_ref, y_ref, hn_sc, acc_sc):
    n_sub = MLP_TILE // MLP_SUBTILE
    ff_block = 1024
    n_ff = D_FF // ff_block

    def rows(s):
        return slice(s * MLP_SUBTILE, (s + 1) * MLP_SUBTILE)

    hn_sc[0] = _rmsnorm(h_ref[rows(0), :], nm_ref[...]).astype(BF16)
    for s in range(n_sub):
        hn = hn_sc[s % 2]
        acc = h_ref[rows(s), :]
        for c in range(n_ff):
            up = _dot(hn, wup_ref[:, c * ff_block:(c + 1) * ff_block])
            act = jnp.square(jnp.maximum(up, 0.0)).astype(BF16)
            acc = acc + _dot(act, wdown_ref[c * ff_block:(c + 1) * ff_block, :])
            if c == 0 and s > 0:
                y_ref[rows(s - 1), :] = _rmsnorm(acc_sc[...], nf_ref[...])
            if c == 1 and s + 1 < n_sub:
                hn_sc[(s + 1) % 2] = _rmsnorm(h_ref[rows(s + 1), :], nm_ref[...]).astype(BF16)
        if s + 1 < n_sub:
            acc_sc[...] = acc
        else:
            y_ref[rows(s), :] = _rmsnorm(acc, nf_ref[...])


def _mlp_call(h, norm_mlp, w_up, w_down, norm_final):
    rows = h.shape[0]
    tm = MLP_TILE
    return pl.pallas_call(
        _mlp_kernel,
        grid=(rows // tm,),
        in_specs=[
            pl.BlockSpec((tm, D_MODEL), lambda i: (i, 0)),
            _full((1, D_MODEL)),
            _resident((D_MODEL, D_FF)),
            _resident((D_FF, D_MODEL)),
            _full((1, D_MODEL)),
        ],
        out_specs=pl.BlockSpec((tm, D_MODEL), lambda i: (i, 0)),
        out_shape=jax.ShapeDtypeStruct((rows, D_MODEL), F32),
        scratch_shapes=[pltpu.VMEM((2, MLP_SUBTILE, D_MODEL), BF16),
                        pltpu.VMEM((MLP_SUBTILE, D_MODEL), F32)],
        compiler_params=pltpu.CompilerParams(
            dimension_semantics=("arbitrary",), vmem_limit_bytes=VMEM_LIMIT_BYTES),
        name="prompt_mlp",
    )(h, norm_mlp, w_up, w_down, norm_final)


def _sample_proj_kernel(x_ref, cache_ref, nm_ref, win_ref, cw_ref, invf_ref,
                        q_ref, kt_ref, v_ref, inner_ref, gate_ref, yconv_ref, cnew_ref):
    hn = _rmsnorm(x_ref[...], nm_ref[...]).astype(BF16)
    z = _dot(hn, win_ref[...])
    prev0 = cache_ref[:, 0:CONV_CH]
    prev1 = cache_ref[:, CONV_CH:2 * CONV_CH]
    cu = z[:, OFF_C:OFF_C + CONV_CH] * z[:, OFF_U:OFF_U + CONV_CH]
    conv = cw_ref[0:1, :] * prev0 + cw_ref[1:2, :] * prev1 + cw_ref[2:3, :] * cu
    yconv_ref[...] = z[:, OFF_B:OFF_B + CONV_CH] * conv
    cnew_ref[:, 0:CONV_CH] = prev1
    cnew_ref[:, CONV_CH:2 * CONV_CH] = cu
    pos = jnp.full((1, HEAD_DIM), float(PAST_LEN), F32)
    cos, sin = _rope_tables(pos, invf_ref[...])
    for h in range(RET_HEADS):
        cols = slice(h * HEAD_DIM, (h + 1) * HEAD_DIM)
        q = _rope(z[:, OFF_Q + h * HEAD_DIM:OFF_Q + (h + 1) * HEAD_DIM], cos, sin)
        k = _rope(z[:, OFF_K + h * HEAD_DIM:OFF_K + (h + 1) * HEAD_DIM], cos, sin) * K_SCALE
        v = z[:, OFF_V + h * HEAD_DIM:OFF_V + (h + 1) * HEAD_DIM]
        g = z[:, OFF_G + h * HEAD_DIM:OFF_G + (h + 1) * HEAD_DIM]
        q_ref[:, cols] = q
        kt_ref[h] = k.T
        v_ref[:, cols] = v
        inner_ref[:, cols] = jnp.sum(q * k, axis=-1, keepdims=True) * v
        gate_ref[:, cols] = g * jax.nn.sigmoid(g)


def _sample_proj_call(xs, cache, norm_mix, w_in, conv_w, inv_freq):
    n = xs.shape[0]
    wide = jax.ShapeDtypeStruct((n, RET_WIDTH), F32)
    return pl.pallas_call(
        _sample_proj_kernel,
        out_shape=(wide,
                   jax.ShapeDtypeStruct((RET_HEADS, HEAD_DIM, n), F32),
                   wide, wide, wide,
                   jax.ShapeDtypeStruct((n, CONV_CH), F32),
                   jax.ShapeDtypeStruct((n, 2 * CONV_CH), F32)),
        compiler_params=pltpu.CompilerParams(vmem_limit_bytes=VMEM_LIMIT_BYTES),
        name="sample_proj",
    )(xs, cache, norm_mix, w_in, conv_w, inv_freq)


def _sample_state_kernel(q_ref, kt_ref, v_ref, st_ref, cross_ref, new_ref):
    step = pl.program_id(0)
    lane = lax.broadcasted_iota(jnp.int32, (HEAD_DIM, HEAD_DIM), 1)
    for i in range(SAMPLE_BLOCK):
        b = step * SAMPLE_BLOCK + i
        for h in range(RET_HEADS):
            cols = slice(h * HEAD_DIM, (h + 1) * HEAD_DIM)
            state = st_ref[i, h]
            q8 = jnp.broadcast_to(q_ref[i:i + 1, cols], (8, HEAD_DIM)).astype(BF16)
            cross_ref[i:i + 1, cols] = _dot(q8, state.astype(BF16))[0:1, :] * GAMMA[h]
            kt_b = jnp.where(lane == b, kt_ref[h], 0.0).astype(BF16)
            new_ref[i, h] = state * GAMMA[h] + _dot(kt_b, v_ref[:, cols].astype(BF16))


def _sample_state_call(q, kt, v, state):
    n = q.shape[0]
    bb = SAMPLE_BLOCK
    return pl.pallas_call(
        _sample_state_kernel,
        grid=(n // bb,),
        in_specs=[
            pl.BlockSpec((bb, RET_WIDTH), lambda i: (i, 0)),
            _full((RET_HEADS, HEAD_DIM, n)),
            _full((n, RET_WIDTH)),
            pl.BlockSpec((bb, RET_HEADS, HEAD_DIM, HEAD_DIM), lambda i: (i, 0, 0, 0)),
        ],
        out_specs=[
            pl.BlockSpec((bb, RET_WIDTH), lambda i: (i, 0)),
            pl.BlockSpec((bb, RET_HEADS, HEAD_DIM, HEAD_DIM), lambda i: (i, 0, 0, 0)),
        ],
        out_shape=(jax.ShapeDtypeStruct((n, RET_WIDTH), F32),
                   jax.ShapeDtypeStruct(state.shape, F32)),
        compiler_params=pltpu.CompilerParams(
            dimension_semantics=("arbitrary",), vmem_limit_bytes=VMEM_LIMIT_BYTES),
        name="sample_state",
    )(q, kt, v, state)


def _sample_out_kernel(x_ref, yconv_ref, inner_ref, cross_ref, gate_ref, wout_ref,
                       nmlp_ref, wup_ref, wdown_ref, nf_ref, y_ref, ymix):
    ymix[:, 0:CONV_CH] = yconv_ref[...].astype(BF16)
    for h in range(RET_HEADS):
        cols = slice(h * HEAD_DIM, (h + 1) * HEAD_DIM)
        o = inner_ref[:, cols] + cross_ref[:, cols]
        on = o * lax.rsqrt(jnp.mean(o * o, axis=-1, keepdims=True) + EPS)
        ymix[:, CONV_CH + h * HEAD_DIM:CONV_CH + (h + 1) * HEAD_DIM] = (
            on * gate_ref[:, cols]).astype(BF16)
    h1 = x_ref[...] + _dot(ymix[...], wout_ref[...])
    y_ref[...] = _mlp_body(h1, nmlp_ref[...], wup_ref, wdown_ref, nf_ref[...])


def _sample_out_call(xs, yconv, inner, cross, gate, w_out, norm_mlp, w_up, w_down, norm_final):
    n = xs.shape[0]
    return pl.pallas_call(
        _sample_out_kernel,
        out_shape=jax.ShapeDtypeStruct((n, D_MODEL), F32),
        scratch_shapes=[pltpu.VMEM((n, D_MODEL), BF16)],
        compiler_params=pltpu.CompilerParams(vmem_limit_bytes=VMEM_LIMIT_BYTES),
        name="sample_out",
    )(xs, yconv, inner, cross, gate, w_out, norm_mlp, w_up, w_down, norm_final)


def kernel(x_prompt, x_sample, cache_conv, state_ret, meta_tokens, norm_mix, w_in, conv_w, w_out,
           norm_mlp, w_up, w_down, norm_final):
    bsz, seq, _ = x_prompt.shape
    n_dec = x_sample.shape[0]
    half = jnp.arange(0, HEAD_DIM, 2, dtype=F32) / HEAD_DIM
    inv_half = 1.0 / (ROPE_BASE ** half)
    inv_freq = jnp.concatenate([inv_half, inv_half])[None, :]

    w_in_b = w_in[0].astype(BF16)
    w_out_b = w_out[0].astype(BF16)
    nmix = norm_mix[0][None, :]
    nmlp = norm_mlp[0][None, :]
    nfin = norm_final[None, :]
    cw = conv_w[0]

    smeta, cmeta = _meta_call(meta_tokens, nmix, w_in_b, inv_freq)
    h1, conv_p, ret_p, w_up_b, w_down_b = _mixer_call(
        x_prompt, nmix, w_in_b, cw, w_out_b, inv_freq, smeta, cmeta, w_up[0], w_down[0])
    y_prompt = _mlp_call(h1.reshape(bsz * seq, D_MODEL), nmlp, w_up_b, w_down_b, nfin)
    y_prompt = y_prompt.reshape(bsz, seq, D_MODEL)

    xs = x_sample.reshape(n_dec, D_MODEL)
    cache = cache_conv[0].reshape(n_dec, (CONV_WIDTH - 1) * CONV_CH)
    q, kt, v, inner, gate, yconv, cnew = _sample_proj_call(xs, cache, nmix, w_in_b, cw, inv_freq)
    cross, ret_s = _sample_state_call(q, kt, v, state_ret[0])
    y_sample = _sample_out_call(xs, yconv, inner, cross, gate, w_out_b, nmlp, w_up_b, w_down_b, nfin)

    return (y_prompt,
            y_sample.reshape(n_dec, 1, D_MODEL),
            conv_p,
            ret_p,
            cnew.reshape(1, n_dec, CONV_WIDTH - 1, CONV_CH),
            ret_s[None])
```

```python
import functools

import numpy as np
import jax
import jax.numpy as jnp
from jax import lax
from jax.experimental import pallas as pl
from jax.experimental.pallas import tpu as pltpu

D_MODEL = 1024
N_META = 16
CONV_CH = 512
CONV_WIDTH = 3
RET_HEADS = 4
HEAD_DIM = 128
RET_WIDTH = RET_HEADS * HEAD_DIM
CHUNK = 128
D_FF = 4 * D_MODEL
EPS = 1e-6
ROPE_BASE = 10000.0
PAST_LEN = 16384
K_SCALE = HEAD_DIM ** -0.5

OFF_U, OFF_C, OFF_B, OFF_Q, OFF_K, OFF_V, OFF_G = 0, 512, 1024, 1536, 2048, 2560, 3072
IN_TOTAL = 3584

GAMMA = tuple(1.0 - 2.0 ** (-5.0 - h) for h in range(RET_HEADS))
LOG_GAMMA = tuple(float(np.log(g)) for g in GAMMA)

TOKEN_TILE = 1024
MIX_SUBTILE = 512
PROJ_BLOCK = 512
OUT_BLOCK = 256
MLP_TILE = 1024
MLP_SUBTILE = 512
SAMPLE_BLOCK = 8
VMEM_LIMIT_BYTES = 60 * 1024 * 1024

F32 = jnp.float32
BF16 = jnp.bfloat16


def _rmsnorm(x, gain):
    return x * lax.rsqrt(jnp.mean(x * x, axis=-1, keepdims=True) + EPS) * gain


def _dot(a, b):
    return jnp.dot(a, b, preferred_element_type=F32)


def _dot_nt(a, b):
    return lax.dot_general(a, b, (((1,), (1,)), ((), ())), preferred_element_type=F32)


def _dot_tn(a, b):
    return lax.dot_general(a, b, (((0,), (0,)), ((), ())), preferred_element_type=F32)


def _rope_tables(pos, inv_freq):
    ang = pos * inv_freq
    lane = lax.broadcasted_iota(jnp.int32, ang.shape, 1)
    sin = jnp.sin(ang)
    return jnp.cos(ang), jnp.where(lane < HEAD_DIM // 2, -sin, sin)


def _rope(x, cos, sin):
    return x * cos + pltpu.roll(x, HEAD_DIM // 2, 1) * sin


def _decay_tables(dec_ref, n_valid=CHUNK):
    row = lax.broadcasted_iota(jnp.int32, (CHUNK, CHUNK), 0).astype(F32)
    col = lax.broadcasted_iota(jnp.int32, (CHUNK, CHUNK), 1).astype(F32)
    diff = row - col
    for h in range(RET_HEADS):
        lg = LOG_GAMMA[h]
        dec_ref[h] = jnp.where(diff >= 0, jnp.exp(lg * jnp.maximum(diff, 0.0)), 0.0)
        dec_ref[RET_HEADS + h] = jnp.exp((row + 1.0) * lg)
        dec_ref[2 * RET_HEADS + h] = jnp.where(row < n_valid, jnp.exp((n_valid - 1.0 - row) * lg), 0.0)


def _meta_kernel(meta_ref, nm_ref, win_ref, invf_ref, smeta_ref, cmeta_ref, xpad, zbuf):
    xpad[...] = jnp.zeros_like(xpad)
    xpad[0:N_META, :] = meta_ref[...]
    hn = _rmsnorm(xpad[...], nm_ref[...]).astype(BF16)
    zbuf[...] = _dot(hn, win_ref[...])
    cu = zbuf[:, OFF_C:OFF_C + CONV_CH] * zbuf[:, OFF_U:OFF_U + CONV_CH]
    cmeta_ref[...] = cu[N_META - 2:N_META, :]
    row = lax.broadcasted_iota(jnp.int32, (CHUNK, HEAD_DIM), 0).astype(F32)
    cos, sin = _rope_tables(row, invf_ref[...])
    for h in range(RET_HEADS):
        kz = zbuf[:, OFF_K + h * HEAD_DIM:OFF_K + (h + 1) * HEAD_DIM]
        vz = zbuf[:, OFF_V + h * HEAD_DIM:OFF_V + (h + 1) * HEAD_DIM]
        k = _rope(kz, cos, sin) * K_SCALE
        kdec = jnp.where(row < N_META, jnp.exp((N_META - 1.0 - row) * LOG_GAMMA[h]), 0.0)
        smeta_ref[h] = _dot_tn((k * kdec).astype(BF16), vz.astype(BF16))


def _meta_call(meta, norm_mix, w_in, inv_freq):
    return pl.pallas_call(
        _meta_kernel,
        out_shape=(jax.ShapeDtypeStruct((RET_HEADS, HEAD_DIM, HEAD_DIM), F32),
                   jax.ShapeDtypeStruct((CONV_WIDTH - 1, CONV_CH), F32)),
        scratch_shapes=[pltpu.VMEM((CHUNK, D_MODEL), F32), pltpu.VMEM((CHUNK, IN_TOTAL), F32)],
        compiler_params=pltpu.CompilerParams(vmem_limit_bytes=VMEM_LIMIT_BYTES),
        name="meta_state",
    )(meta, norm_mix, w_in, inv_freq)


def _mixer_kernel(x_ref, nm_ref, win_ref, cw_ref, wout_ref, invf_ref, smeta_ref, cmeta_ref,
                  wup_ref, wdown_ref,
                  h1_ref, cstate_ref, sstate_ref, wup_bf_ref, wdown_bf_ref,
                  zbuf, cbuf, ymix, hn_sc, cos_sc, sin_sc, dec_sc):
    b = pl.program_id(0)
    t = pl.program_id(1)
    tm = TOKEN_TILE
    tile0 = pl.multiple_of(t * tm, tm)

    wup_bf_ref[...] = wup_ref[...].astype(BF16)
    wdown_bf_ref[...] = wdown_ref[...].astype(BF16)

    @pl.when(jnp.logical_and(t == 0, b == 0))
    def _():
        _decay_tables(dec_sc)

    @pl.when(b == 0)
    def _():
        pos = (lax.broadcasted_iota(jnp.int32, (tm, HEAD_DIM), 0) + (N_META + t * tm)).astype(F32)
        cos, sin = _rope_tables(pos, invf_ref[...])
        cos_sc[pl.ds(tile0, tm), :] = cos
        sin_sc[pl.ds(tile0, tm), :] = sin

    @pl.when(t == 0)
    def _():
        sstate_ref[0, 0] = smeta_ref[...]
        cbuf[6:8, :] = cmeta_ref[...]

    n_sub = tm // MIX_SUBTILE
    chunks_per_sub = MIX_SUBTILE // CHUNK

    def sub_rows(s):
        return slice(s * MIX_SUBTILE, (s + 1) * MIX_SUBTILE)

    def norm_in(s):
        hn_sc[s % 2] = _rmsnorm(x_ref[0, sub_rows(s), :], nm_ref[...]).astype(BF16)

    def project(s, p):
        cols = slice(p * PROJ_BLOCK, (p + 1) * PROJ_BLOCK)
        zbuf[s % 2, :, cols] = _dot(hn_sc[s % 2], win_ref[:, cols])

    def out_project(s, p):
        cols = slice(p * OUT_BLOCK, (p + 1) * OUT_BLOCK)
        h1_ref[0, sub_rows(s), cols] = (x_ref[0, sub_rows(s), cols]
                                        + _dot(ymix[s % 2], wout_ref[:, cols]))

    def conv_chunk(s, c):
        z = zbuf.at[s % 2]
        rows = slice(c * CHUNK, (c + 1) * CHUNK)
        g0 = (s * chunks_per_sub + c) * CHUNK
        cu = z[rows, OFF_C:OFF_C + CONV_CH] * z[rows, OFF_U:OFF_U + CONV_CH]
        cbuf[8 + g0:8 + g0 + CHUNK, :] = cu
        conv = (cw_ref[0:1, :] * cbuf[6 + g0:6 + g0 + CHUNK, :]
                + cw_ref[1:2, :] * cbuf[7 + g0:7 + g0 + CHUNK, :]
                + cw_ref[2:3, :] * cu)
        ymix[s % 2, rows, 0:CONV_CH] = (z[rows, OFF_B:OFF_B + CONV_CH] * conv).astype(BF16)

    saved = {}

    def scores_and_state(s, c):
        z = zbuf.at[s % 2]
        rows = slice(c * CHUNK, (c + 1) * CHUNK)
        g0 = (s * chunks_per_sub + c) * CHUNK
        cos = cos_sc[pl.ds(tile0 + g0, CHUNK), :]
        sin = sin_sc[pl.ds(tile0 + g0, CHUNK), :]
        qbs, kbs, vbs, states = [], [], [], []
        for h in range(RET_HEADS):
            qz = z[rows, OFF_Q + h * HEAD_DIM:OFF_Q + (h + 1) * HEAD_DIM]
            kz = z[rows, OFF_K + h * HEAD_DIM:OFF_K + (h + 1) * HEAD_DIM]
            vb = z[rows, OFF_V + h * HEAD_DIM:OFF_V + (h + 1) * HEAD_DIM].astype(BF16)
            k = _rope(kz, cos, sin) * K_SCALE
            state = sstate_ref[0, 0, h]
            kd = (k * dec_sc[2 * RET_HEADS + h]).astype(BF16)
            sstate_ref[0, 0, h] = state * (GAMMA[h] ** CHUNK) + _dot_tn(kd, vb)
            qbs.append(_rope(qz, cos, sin).astype(BF16))
            kbs.append(k.astype(BF16))
            vbs.append(vb)
            states.append(state)
        scores = [_dot_nt(qbs[h], kbs[h]) for h in range(RET_HEADS)]
        cross = [_dot(qbs[h], states[h].astype(BF16)) for h in range(RET_HEADS)]
        saved[(s, c)] = (scores, cross, vbs)

    def outputs(s, c):
        z = zbuf.at[s % 2]
        rows = slice(c * CHUNK, (c + 1) * CHUNK)
        scores, cross, vbs = saved.pop((s, c))
        for h in range(RET_HEADS):
            gz = z[rows, OFF_G + h * HEAD_DIM:OFF_G + (h + 1) * HEAD_DIM]
            p = (scores[h] * dec_sc[h]).astype(BF16)
            o = _dot(p, vbs[h]) + cross[h] * dec_sc[RET_HEADS + h]
            on = o * lax.rsqrt(jnp.mean(o * o, axis=-1, keepdims=True) + EPS)
            ymix[s % 2, rows, CONV_CH + h * HEAD_DIM:CONV_CH + (h + 1) * HEAD_DIM] = (
                on * (gz * jax.nn.sigmoid(gz))).astype(BF16)

    norm_in(0)
    for p in range(IN_TOTAL // PROJ_BLOCK):
        project(0, p)
    for s in range(n_sub):
        fillers = []
        if s > 0:
            fillers += [functools.partial(out_project, s - 1, p)
                        for p in range(D_MODEL // OUT_BLOCK)]
        if s + 1 < n_sub:
            norm_in(s + 1)
            fillers += [functools.partial(project, s + 1, p)
                        for p in range(IN_TOTAL // PROJ_BLOCK)]
        steps = []
        for c in range(chunks_per_sub):
            steps.append(functools.partial(scores_and_state, s, c))
            if c > 0:
                steps.append(functools.partial(outputs, s, c - 1))
        steps.append(functools.partial(outputs, s, chunks_per_sub - 1))
        n_f, n_s = len(fillers), len(steps)
        placed = 0
        for i, step in enumerate(steps):
            while placed < n_f and placed * n_s <= i * n_f:
                fillers[placed]()
                placed += 1
            step()
            if i % 2 == 0:
                conv_chunk(s, i // 2)
        while placed < n_f:
            fillers[placed]()
            placed += 1
    for p in range(D_MODEL // OUT_BLOCK):
        out_project(n_sub - 1, p)

    tail = cbuf[8 + tm - 2:8 + tm, :]
    cbuf[6:8, :] = tail
    cstate_ref[0, 0] = tail


def _full(shape):
    return pl.BlockSpec(shape, lambda *_: (0,) * len(shape))


def _resident(shape):
    return pl.BlockSpec(shape, lambda *_: (0,) * len(shape), pipeline_mode=pl.Buffered(1))


def _mixer_call(x, norm_mix, w_in, conv_w, w_out, inv_freq, smeta, cmeta, w_up, w_down):
    bsz, seq, _ = x.shape
    tm = TOKEN_TILE
    n_t = seq // tm
    grid = (bsz, n_t)
    ff_slice = D_FF // (bsz * n_t)
    return pl.pallas_call(
        _mixer_kernel,
        grid=grid,
        in_specs=[
            pl.BlockSpec((1, tm, D_MODEL), lambda b, t: (b, t, 0)),
            _full((1, D_MODEL)),
            _resident((D_MODEL, IN_TOTAL)),
            _full((CONV_WIDTH, CONV_CH)),
            _resident((D_MODEL, D_MODEL)),
            _full((1, HEAD_DIM)),
            _full((RET_HEADS, HEAD_DIM, HEAD_DIM)),
            _full((CONV_WIDTH - 1, CONV_CH)),
            pl.BlockSpec((D_MODEL, ff_slice), lambda b, t: (0, b * n_t + t)),
            pl.BlockSpec((ff_slice, D_MODEL), lambda b, t: (b * n_t + t, 0)),
        ],
        out_specs=[
            pl.BlockSpec((1, tm, D_MODEL), lambda b, t: (b, t, 0)),
            pl.BlockSpec((1, 1, CONV_WIDTH - 1, CONV_CH), lambda b, t: (0, b, 0, 0)),
            pl.BlockSpec((1, 1, RET_HEADS, HEAD_DIM, HEAD_DIM), lambda b, t: (0, b, 0, 0, 0)),
            pl.BlockSpec((D_MODEL, ff_slice), lambda b, t: (0, b * n_t + t)),
            pl.BlockSpec((ff_slice, D_MODEL), lambda b, t: (b * n_t + t, 0)),
        ],
        out_shape=(
            jax.ShapeDtypeStruct((bsz, seq, D_MODEL), F32),
            jax.ShapeDtypeStruct((1, bsz, CONV_WIDTH - 1, CONV_CH), F32),
            jax.ShapeDtypeStruct((1, bsz, RET_HEADS, HEAD_DIM, HEAD_DIM), F32),
            jax.ShapeDtypeStruct((D_MODEL, D_FF), BF16),
            jax.ShapeDtypeStruct((D_FF, D_MODEL), BF16),
        ),
        scratch_shapes=[
            pltpu.VMEM((2, MIX_SUBTILE, IN_TOTAL), F32),
            pltpu.VMEM((tm + 8, CONV_CH), F32),
            pltpu.VMEM((2, MIX_SUBTILE, D_MODEL), BF16),
            pltpu.VMEM((2, MIX_SUBTILE, D_MODEL), BF16),
            pltpu.VMEM((seq, HEAD_DIM), F32),
            pltpu.VMEM((seq, HEAD_DIM), F32),
            pltpu.VMEM((3 * RET_HEADS, CHUNK, CHUNK), F32),
        ],
        compiler_params=pltpu.CompilerParams(
            dimension_semantics=("arbitrary", "arbitrary"), vmem_limit_bytes=VMEM_LIMIT_BYTES),
        name="prompt_mixer",
    )(x, norm_mix, w_in, conv_w, w_out, inv_freq, smeta, cmeta, w_up, w_down)


def _mlp_body(h, nm, wup_ref, wdown_ref, nf):
    hn = _rmsnorm(h, nm).astype(BF16)
    acc = h
    ff_block = 1024
    for c in range(D_FF // ff_block):
        up = _dot(hn, wup_ref[:, c * ff_block:(c + 1) * ff_block])
        act = jnp.square(jnp.maximum(up, 0.0)).astype(BF16)
        acc = acc + _dot(act, wdown_ref[c * ff_block:(c + 1) * ff_block, :])
    return _rmsnorm(acc, nf)


def _prompt_mlp_tile(h_ref, nm_ref, wup_ref, wdown_ref, nf_ref, y_ref, hn_sc, acc_sc):
    n_sub = MLP_TILE // MLP_SUBTILE
    ff_block = 1024
    n_ff = D_FF // ff_block

    def rows(s):
        return slice(s * MLP_SUBTILE, (s + 1) * MLP_SUBTILE)

    hn_sc[0] = _rmsnorm(h_ref[rows(0), :], nm_ref[...]).astype(BF16)
    for s in range(n_sub):
        hn = hn_sc[s % 2]
        acc = h_ref[rows(s), :]
        for c in range(n_ff):
            up = _dot(hn, wup_ref[:, c * ff_block:(c + 1) * ff_block])
            act = jnp.square(jnp.maximum(up, 0.0)).astype(BF16)
            acc = acc + _dot(act, wdown_ref[c * ff_block:(c + 1) * ff_block, :])
            if c == 0 and s > 0:
                y_ref[rows(s - 1), :] = _rmsnorm(acc_sc[...], nf_ref[...])
            if c == 1 and s + 1 < n_sub:
                hn_sc[(s + 1) % 2] = _rmsnorm(h_ref[rows(s + 1), :], nm_ref[...]).astype(BF16)
        if s + 1 < n_sub:
            acc_sc[...] = acc
        else:
            y_ref[rows(s), :] = _rmsnorm(acc, nf_ref[...])


def _sample_state_block(step, q_ref, kt_ref, v_ref, st_ref, new_ref, cross_sc):
    assert SAMPLE_BLOCK == 8
    lane = lax.broadcasted_iota(jnp.int32, (HEAD_DIM, HEAD_DIM), 1)
    sub = lax.broadcasted_iota(jnp.int32, (SAMPLE_BLOCK, HEAD_DIM), 0)
    row0 = pl.multiple_of(step * SAMPLE_BLOCK, SAMPLE_BLOCK)
    for h in range(RET_HEADS):
        cols = slice(h * HEAD_DIM, (h + 1) * HEAD_DIM)
        vb = v_ref[:, cols].astype(BF16)
        cross = jnp.zeros((SAMPLE_BLOCK, HEAD_DIM), F32)
        for i in range(SAMPLE_BLOCK):
            b = step * SAMPLE_BLOCK + i
            state = st_ref[i, h]
            q8 = jnp.broadcast_to(q_ref[i:i + 1, cols], (SAMPLE_BLOCK, HEAD_DIM)).astype(BF16)
            cross = jnp.where(sub == i, _dot(q8, state.astype(BF16)), cross)
            kt_b = jnp.where(lane == b, kt_ref[h], 0.0).astype(BF16)
            new_ref[i, h] = state * GAMMA[h] + _dot(kt_b, vb)
        cross_sc[pl.ds(row0, SAMPLE_BLOCK), cols] = cross * GAMMA[h]


def _sample_tail(x_ref, yconv_ref, inner_ref, gate_ref, wout_ref, nmlp_ref, wup_ref, wdown_ref,
                 nf_ref, cross_sc, ymix, y_ref):
    ymix[:, 0:CONV_CH] = yconv_ref[...].astype(BF16)
    for h in range(RET_HEADS):
        cols = slice(h * HEAD_DIM, (h + 1) * HEAD_DIM)
        o = inner_ref[:, cols] + cross_sc[:, cols]
        on = o * lax.rsqrt(jnp.mean(o * o, axis=-1, keepdims=True) + EPS)
        ymix[:, CONV_CH + h * HEAD_DIM:CONV_CH + (h + 1) * HEAD_DIM] = (
            on * gate_ref[:, cols]).astype(BF16)
    h1 = x_ref[...] + _dot(ymix[...], wout_ref[...])
    y_ref[...] = _mlp_body(h1, nmlp_ref[...], wup_ref, wdown_ref, nf_ref[...])


def _mlp_kernel(h_ref, nm_ref, wup_ref, wdown_ref, nf_ref,
                q_ref, kt_ref, v_ref, st_ref, xs_ref, yconv_ref, inner_ref, gate_ref, wout_ref,
                y_ref, new_ref, ys_ref,
                hn_sc, acc_sc, cross_sc, ymix_sc):
    step = pl.program_id(0)
    n_tiles = pl.num_programs(0) - 1

    @pl.when(step < n_tiles)
    def _():
        _sample_state_block(step, q_ref, kt_ref, v_ref, st_ref, new_ref, cross_sc)
        _prompt_mlp_tile(h_ref, nm_ref, wup_ref, wdown_ref, nf_ref, y_ref, hn_sc, acc_sc)

    @pl.when(step == n_tiles)
    def _():
        _sample_tail(xs_ref, yconv_ref, inner_ref, gate_ref, wout_ref, nm_ref, wup_ref, wdown_ref,
                     nf_ref, cross_sc, ymix_sc, ys_ref)


def _mlp_call(h, norm_mlp, w_up, w_down, norm_final, q, kt, v, state, xs, yconv, inner, gate, w_out):
    rows = h.shape[0]
    n = xs.shape[0]
    tm = MLP_TILE
    n_tiles = rows // tm
    bb = SAMPLE_BLOCK
    assert n_tiles * bb == n
    last = n_tiles - 1

    def tile(i):
        return jnp.minimum(i, last)

    return pl.pallas_call(
        _mlp_kernel,
        grid=(n_tiles + 1,),
        in_specs=[
            pl.BlockSpec((tm, D_MODEL), lambda i: (tile(i), 0)),
            _full((1, D_MODEL)),
            _resident((D_MODEL, D_FF)),
            _resident((D_FF, D_MODEL)),
            _full((1, D_MODEL)),
            pl.BlockSpec((bb, RET_WIDTH), lambda i: (tile(i), 0)),
            _full((RET_HEADS, HEAD_DIM, n)),
            _full((n, RET_WIDTH)),
            pl.BlockSpec((bb, RET_HEADS, HEAD_DIM, HEAD_DIM), lambda i: (tile(i), 0, 0, 0)),
            _full((n, D_MODEL)),
            _full((n, CONV_CH)),
            _full((n, RET_WIDTH)),
            _full((n, RET_WIDTH)),
            _resident((D_MODEL, D_MODEL)),
        ],
        out_specs=[
            pl.BlockSpec((tm, D_MODEL), lambda i: (tile(i), 0)),
            pl.BlockSpec((bb, RET_HEADS, HEAD_DIM, HEAD_DIM), lambda i: (tile(i), 0, 0, 0)),
            _full((n, D_MODEL)),
        ],
        out_shape=(jax.ShapeDtypeStruct((rows, D_MODEL), F32),
                   jax.ShapeDtypeStruct(state.shape, F32),
                   jax.ShapeDtypeStruct((n, D_MODEL), F32)),
        scratch_shapes=[pltpu.VMEM((2, MLP_SUBTILE, D_MODEL), BF16),
                        pltpu.VMEM((MLP_SUBTILE, D_MODEL), F32),
                        pltpu.VMEM((n, RET_WIDTH), F32),
                        pltpu.VMEM((n, D_MODEL), BF16)],
        compiler_params=pltpu.CompilerParams(
            dimension_semantics=("arbitrary",), vmem_limit_bytes=VMEM_LIMIT_BYTES),
        name="mlp_and_sample",
    )(h, norm_mlp, w_up, w_down, norm_final, q, kt, v, state, xs, yconv, inner, gate, w_out)


def _sample_proj_kernel(x_ref, cache_ref, nm_ref, win_ref, cw_ref, invf_ref,
                        q_ref, kt_ref, v_ref, inner_ref, gate_ref, yconv_ref, cnew_ref):
    hn = _rmsnorm(x_ref[...], nm_ref[...]).astype(BF16)
    z = _dot(hn, win_ref[...])
    prev0 = cache_ref[:, 0:CONV_CH]
    prev1 = cache_ref[:, CONV_CH:2 * CONV_CH]
    cu = z[:, OFF_C:OFF_C + CONV_CH] * z[:, OFF_U:OFF_U + CONV_CH]
    conv = cw_ref[0:1, :] * prev0 + cw_ref[1:2, :] * prev1 + cw_ref[2:3, :] * cu
    yconv_ref[...] = z[:, OFF_B:OFF_B + CONV_CH] * conv
    cnew_ref[:, 0:CONV_CH] = prev1
    cnew_ref[:, CONV_CH:2 * CONV_CH] = cu
    pos = jnp.full((1, HEAD_DIM), float(PAST_LEN), F32)
    cos, sin = _rope_tables(pos, invf_ref[...])
    for h in range(RET_HEADS):
        cols = slice(h * HEAD_DIM, (h + 1) * HEAD_DIM)
        q = _rope(z[:, OFF_Q + h * HEAD_DIM:OFF_Q + (h + 1) * HEAD_DIM], cos, sin)
        k = _rope(z[:, OFF_K + h * HEAD_DIM:OFF_K + (h + 1) * HEAD_DIM], cos, sin) * K_SCALE
        v = z[:, OFF_V + h * HEAD_DIM:OFF_V + (h + 1) * HEAD_DIM]
        g = z[:, OFF_G + h * HEAD_DIM:OFF_G + (h + 1) * HEAD_DIM]
        q_ref[:, cols] = q
        kt_ref[h] = k.T
        v_ref[:, cols] = v
        inner_ref[:, cols] = jnp.sum(q * k, axis=-1, keepdims=True) * v
        gate_ref[:, cols] = g * jax.nn.sigmoid(g)


def _sample_proj_call(xs, cache, norm_mix, w_in, conv_w, inv_freq):
    n = xs.shape[0]
    wide = jax.ShapeDtypeStruct((n, RET_WIDTH), F32)
    return pl.pallas_call(
        _sample_proj_kernel,
        out_shape=(wide,
                   jax.ShapeDtypeStruct((RET_HEADS, HEAD_DIM, n), F32),
                   wide, wide, wide,
                   jax.ShapeDtypeStruct((n, CONV_CH), F32),
                   jax.ShapeDtypeStruct((n, 2 * CONV_CH), F32)),
        compiler_params=pltpu.CompilerParams(vmem_limit_bytes=VMEM_LIMIT_BYTES),
        name="sample_proj",
    )(xs, cache, norm_mix, w_in, conv_w, inv_freq)


def kernel(x_prompt, x_sample, cache_conv, state_ret, meta_tokens, norm_mix, w_in, conv_w, w_out,
           norm_mlp, w_up, w_down, norm_final):
    bsz, seq, _ = x_prompt.shape
    n_dec = x_sample.shape[0]
    half = jnp.arange(0, HEAD_DIM, 2, dtype=F32) / HEAD_DIM
    inv_half = 1.0 / (ROPE_BASE ** half)
    inv_freq = jnp.concatenate([inv_half, inv_half])[None, :]

    w_in_b = w_in[0].astype(BF16)
    w_out_b = w_out[0].astype(BF16)
    nmix = norm_mix[0][None, :]
    nmlp = norm_mlp[0][None, :]
    nfin = norm_final[None, :]
    cw = conv_w[0]

    smeta, cmeta = _meta_call(meta_tokens, nmix, w_in_b, inv_freq)
    h1, conv_p, ret_p, w_up_b, w_down_b = _mixer_call(
        x_prompt, nmix, w_in_b, cw, w_out_b, inv_freq, smeta, cmeta, w_up[0], w_down[0])

    xs = x_sample.reshape(n_dec, D_MODEL)
    cache = cache_conv[0].reshape(n_dec, (CONV_WIDTH - 1) * CONV_CH)
    q, kt, v, inner, gate, yconv, cnew = _sample_proj_call(xs, cache, nmix, w_in_b, cw, inv_freq)
    y_prompt, ret_s, y_sample = _mlp_call(
        h1.reshape(bsz * seq, D_MODEL), nmlp, w_up_b, w_down_b, nfin,
        q, kt, v, state_ret[0], xs, yconv, inner, gate, w_out_b)
    y_prompt = y_prompt.reshape(bsz, seq, D_MODEL)

    return (y_prompt,
            y_sample.reshape(n_dec, 1, D_MODEL),
            conv_p,
            ret_p,
            cnew.reshape(1, n_dec, CONV_WIDTH - 1, CONV_CH),
            ret_s[None])
```

```python
import functools

import numpy as np
import jax
import jax.numpy as jnp
from jax import lax
from jax.experimental import pallas as pl
from jax.experimental.pallas import tpu as pltpu

D_MODEL = 1024
N_META = 16
CONV_CH = 512
CONV_WIDTH = 3
RET_HEADS = 4
HEAD_DIM = 128
RET_WIDTH = RET_HEADS * HEAD_DIM
CHUNK = 128
D_FF = 4 * D_MODEL
EPS = 1e-6
ROPE_BASE = 10000.0
PAST_LEN = 16384
K_SCALE = HEAD_DIM ** -0.5

OFF_U, OFF_C, OFF_B, OFF_Q, OFF_K, OFF_V, OFF_G = 0, 512, 1024, 1536, 2048, 2560, 3072
IN_TOTAL = 3584

GAMMA = tuple(1.0 - 2.0 ** (-5.0 - h) for h in range(RET_HEADS))
LOG_GAMMA = tuple(float(np.log(g)) for g in GAMMA)

TOKEN_TILE = 1024
MIX_SUBTILE = 512
PROJ_BLOCK = 512
OUT_BLOCK = 256
MLP_TILE = 1024
MLP_SUBTILE = 512
SAMPLE_BLOCK = 8
VMEM_LIMIT_BYTES = 60 * 1024 * 1024

F32 = jnp.float32
BF16 = jnp.bfloat16


def _rmsnorm(x, gain):
    return x * lax.rsqrt(jnp.mean(x * x, axis=-1, keepdims=True) + EPS) * gain


def _dot(a, b):
    return jnp.dot(a, b, preferred_element_type=F32)


def _dot_nt(a, b):
    return lax.dot_general(a, b, (((1,), (1,)), ((), ())), preferred_element_type=F32)


def _dot_tn(a, b):
    return lax.dot_general(a, b, (((0,), (0,)), ((), ())), preferred_element_type=F32)


def _rope_tables(pos, inv_freq):
    ang = pos * inv_freq
    lane = lax.broadcasted_iota(jnp.int32, ang.shape, 1)
    sin = jnp.sin(ang)
    return jnp.cos(ang), jnp.where(lane < HEAD_DIM // 2, -sin, sin)


def _rope(x, cos, sin):
    return x * cos + pltpu.roll(x, HEAD_DIM // 2, 1) * sin


def _decay_tables(dec_ref, n_valid=CHUNK):
    row = lax.broadcasted_iota(jnp.int32, (CHUNK, CHUNK), 0).astype(F32)
    col = lax.broadcasted_iota(jnp.int32, (CHUNK, CHUNK), 1).astype(F32)
    diff = row - col
    for h in range(RET_HEADS):
        lg = LOG_GAMMA[h]
        dec_ref[h] = jnp.where(diff >= 0, jnp.exp(lg * jnp.maximum(diff, 0.0)), 0.0)
        dec_ref[RET_HEADS + h] = jnp.exp((row + 1.0) * lg)
        dec_ref[2 * RET_HEADS + h] = jnp.where(row < n_valid, jnp.exp((n_valid - 1.0 - row) * lg), 0.0)


def _z_slice(zbuf, rows, off, width):
    blk, inner = divmod(off, PROJ_BLOCK)
    assert inner + width <= PROJ_BLOCK
    return zbuf[blk, rows, inner:inner + width]


def _meta_tail(zbuf, invf_ref, smeta_ref, cmeta_ref):
    rows = slice(0, CHUNK)
    cu = _z_slice(zbuf, rows, OFF_C, CONV_CH) * _z_slice(zbuf, rows, OFF_U, CONV_CH)
    cmeta_ref[...] = cu[N_META - 2:N_META, :]
    row = lax.broadcasted_iota(jnp.int32, (CHUNK, HEAD_DIM), 0).astype(F32)
    cos, sin = _rope_tables(row, invf_ref[...])
    for h in range(RET_HEADS):
        kz = _z_slice(zbuf, rows, OFF_K + h * HEAD_DIM, HEAD_DIM)
        vz = _z_slice(zbuf, rows, OFF_V + h * HEAD_DIM, HEAD_DIM)
        k = _rope(kz, cos, sin) * K_SCALE
        kdec = jnp.where(row < N_META, jnp.exp((N_META - 1.0 - row) * LOG_GAMMA[h]), 0.0)
        smeta_ref[h] = _dot_tn((k * kdec).astype(BF16), vz.astype(BF16))


def _sample_proj_tail(zbuf, rows, cache_ref, cw_ref, invf_ref,
                      q_ref, kt_ref, v_ref, inner_ref, gate_ref, yconv_ref, cnew_ref):
    prev0 = cache_ref[:, 0:CONV_CH]
    prev1 = cache_ref[:, CONV_CH:2 * CONV_CH]
    cu = _z_slice(zbuf, rows, OFF_C, CONV_CH) * _z_slice(zbuf, rows, OFF_U, CONV_CH)
    conv = cw_ref[0:1, :] * prev0 + cw_ref[1:2, :] * prev1 + cw_ref[2:3, :] * cu
    yconv_ref[...] = _z_slice(zbuf, rows, OFF_B, CONV_CH) * conv
    cnew_ref[:, 0:CONV_CH] = prev1
    cnew_ref[:, CONV_CH:2 * CONV_CH] = cu
    pos = jnp.full((1, HEAD_DIM), float(PAST_LEN), F32)
    cos, sin = _rope_tables(pos, invf_ref[...])
    for h in range(RET_HEADS):
        cols = slice(h * HEAD_DIM, (h + 1) * HEAD_DIM)
        q = _rope(_z_slice(zbuf, rows, OFF_Q + h * HEAD_DIM, HEAD_DIM), cos, sin)
        k = _rope(_z_slice(zbuf, rows, OFF_K + h * HEAD_DIM, HEAD_DIM), cos, sin) * K_SCALE
        v = _z_slice(zbuf, rows, OFF_V + h * HEAD_DIM, HEAD_DIM)
        g = _z_slice(zbuf, rows, OFF_G + h * HEAD_DIM, HEAD_DIM)
        q_ref[:, cols] = q
        kt_ref[h] = k.T
        v_ref[:, cols] = v
        inner_ref[:, cols] = jnp.sum(q * k, axis=-1, keepdims=True) * v
        gate_ref[:, cols] = g * jax.nn.sigmoid(g)


def _prologue_kernel(meta_ref, xs_ref, cache_ref, nm_ref, win_ref, wout_ref, cw_ref, invf_ref,
                     winb_ref, woutb_ref, smeta_ref, cmeta_ref,
                     q_ref, kt_ref, v_ref, inner_ref, gate_ref, yconv_ref, cnew_ref,
                     xpad, hn_sc, zbuf):
    j = pl.program_id(0)
    n_dec = xs_ref.shape[0]

    @pl.when(j == 0)
    def _():
        xpad[...] = jnp.zeros_like(xpad)
        xpad[0:N_META, :] = meta_ref[...]
        hn_sc[0:CHUNK, :] = _rmsnorm(xpad[...], nm_ref[...]).astype(BF16)
        hn_sc[CHUNK:CHUNK + n_dec, :] = _rmsnorm(xs_ref[...], nm_ref[...]).astype(BF16)
        woutb_ref[...] = wout_ref[...].astype(BF16)

    w_blk = win_ref[...].astype(BF16)
    winb_ref[...] = w_blk
    zbuf[j] = _dot(hn_sc[...], w_blk)

    @pl.when(j == pl.num_programs(0) - 1)
    def _():
        _meta_tail(zbuf, invf_ref, smeta_ref, cmeta_ref)
        _sample_proj_tail(zbuf, slice(CHUNK, CHUNK + n_dec), cache_ref, cw_ref, invf_ref,
                          q_ref, kt_ref, v_ref, inner_ref, gate_ref, yconv_ref, cnew_ref)


def _prologue_call(meta, xs, cache, norm_mix, w_in, w_out, conv_w, inv_freq):
    n = xs.shape[0]
    n_blk = IN_TOTAL // PROJ_BLOCK
    wide = jax.ShapeDtypeStruct((n, RET_WIDTH), F32)
    return pl.pallas_call(
        _prologue_kernel,
        grid=(n_blk,),
        in_specs=[
            _full((N_META, D_MODEL)),
            _full((n, D_MODEL)),
            _full((n, 2 * CONV_CH)),
            _full((1, D_MODEL)),
            pl.BlockSpec((D_MODEL, PROJ_BLOCK), lambda j: (0, j)),
            _full((D_MODEL, D_MODEL)),
            _full((CONV_WIDTH, CONV_CH)),
            _full((1, HEAD_DIM)),
        ],
        out_specs=[
            pl.BlockSpec((D_MODEL, PROJ_BLOCK), lambda j: (0, j)),
            _full((D_MODEL, D_MODEL)),
            _full((RET_HEADS, HEAD_DIM, HEAD_DIM)),
            _full((CONV_WIDTH - 1, CONV_CH)),
            _full((n, RET_WIDTH)),
            _full((RET_HEADS, HEAD_DIM, n)),
            _full((n, RET_WIDTH)),
            _full((n, RET_WIDTH)),
            _full((n, RET_WIDTH)),
            _full((n, CONV_CH)),
            _full((n, 2 * CONV_CH)),
        ],
        out_shape=(jax.ShapeDtypeStruct((D_MODEL, IN_TOTAL), BF16),
                   jax.ShapeDtypeStruct((D_MODEL, D_MODEL), BF16),
                   jax.ShapeDtypeStruct((RET_HEADS, HEAD_DIM, HEAD_DIM), F32),
                   jax.ShapeDtypeStruct((CONV_WIDTH - 1, CONV_CH), F32),
                   wide,
                   jax.ShapeDtypeStruct((RET_HEADS, HEAD_DIM, n), F32),
                   wide, wide, wide,
                   jax.ShapeDtypeStruct((n, CONV_CH), F32),
                   jax.ShapeDtypeStruct((n, 2 * CONV_CH), F32)),
        scratch_shapes=[pltpu.VMEM((CHUNK, D_MODEL), F32),
                        pltpu.VMEM((CHUNK + n, D_MODEL), BF16),
                        pltpu.VMEM((n_blk, CHUNK + n, PROJ_BLOCK), F32)],
        compiler_params=pltpu.CompilerParams(
            dimension_semantics=("arbitrary",), vmem_limit_bytes=VMEM_LIMIT_BYTES),
        name="prologue",
    )(meta, xs, cache, norm_mix, w_in, w_out, conv_w, inv_freq)


def _mixer_kernel(x_ref, nm_ref, win_ref, cw_ref, wout_ref, invf_ref, smeta_ref, cmeta_ref,
                  wup_ref, wdown_ref,
                  h1_ref, cstate_ref, sstate_ref, wup_bf_ref, wdown_bf_ref,
                  zbuf, cbuf, ymix, hn_sc, cos_sc, sin_sc, dec_sc):
    b = pl.program_id(0)
    t = pl.program_id(1)
    tm = TOKEN_TILE
    tile0 = pl.multiple_of(t * tm, tm)

    wup_bf_ref[...] = wup_ref[...].astype(BF16)
    wdown_bf_ref[...] = wdown_ref[...].astype(BF16)

    @pl.when(jnp.logical_and(t == 0, b == 0))
    def _():
        _decay_tables(dec_sc)

    @pl.when(b == 0)
    def _():
        pos = (lax.broadcasted_iota(jnp.int32, (tm, HEAD_DIM), 0) + (N_META + t * tm)).astype(F32)
        cos, sin = _rope_tables(pos, invf_ref[...])
        cos_sc[pl.ds(tile0, tm), :] = cos
        sin_sc[pl.ds(tile0, tm), :] = sin

    @pl.when(t == 0)
    def _():
        sstate_ref[0, 0] = smeta_ref[...]
        cbuf[6:8, :] = cmeta_ref[...]

    n_sub = tm // MIX_SUBTILE
    chunks_per_sub = MIX_SUBTILE // CHUNK

    def sub_rows(s):
        return slice(s * MIX_SUBTILE, (s + 1) * MIX_SUBTILE)

    def norm_in(s):
        hn_sc[s % 2] = _rmsnorm(x_ref[0, sub_rows(s), :], nm_ref[...]).astype(BF16)

    def project(s, p):
        cols = slice(p * PROJ_BLOCK, (p + 1) * PROJ_BLOCK)
        zbuf[s % 2, :, cols] = _dot(hn_sc[s % 2], win_ref[:, cols])

    def out_project(s, p):
        cols = slice(p * OUT_BLOCK, (p + 1) * OUT_BLOCK)
        h1_ref[0, sub_rows(s), cols] = (x_ref[0, sub_rows(s), cols]
                                        + _dot(ymix[s % 2], wout_ref[:, cols]))

    def conv_chunk(s, c):
        z = zbuf.at[s % 2]
        rows = slice(c * CHUNK, (c + 1) * CHUNK)
        g0 = (s * chunks_per_sub + c) * CHUNK
        cu = z[rows, OFF_C:OFF_C + CONV_CH] * z[rows, OFF_U:OFF_U + CONV_CH]
        cbuf[8 + g0:8 + g0 + CHUNK, :] = cu
        conv = (cw_ref[0:1, :] * cbuf[6 + g0:6 + g0 + CHUNK, :]
                + cw_ref[1:2, :] * cbuf[7 + g0:7 + g0 + CHUNK, :]
                + cw_ref[2:3, :] * cu)
        ymix[s % 2, rows, 0:CONV_CH] = (z[rows, OFF_B:OFF_B + CONV_CH] * conv).astype(BF16)

    saved = {}

    def scores_and_state(s, c):
        z = zbuf.at[s % 2]
        rows = slice(c * CHUNK, (c + 1) * CHUNK)
        g0 = (s * chunks_per_sub + c) * CHUNK
        cos = cos_sc[pl.ds(tile0 + g0, CHUNK), :]
        sin = sin_sc[pl.ds(tile0 + g0, CHUNK), :]
        qbs, kbs, vbs, states = [], [], [], []
        for h in range(RET_HEADS):
            qz = z[rows, OFF_Q + h * HEAD_DIM:OFF_Q + (h + 1) * HEAD_DIM]
            kz = z[rows, OFF_K + h * HEAD_DIM:OFF_K + (h + 1) * HEAD_DIM]
            vb = z[rows, OFF_V + h * HEAD_DIM:OFF_V + (h + 1) * HEAD_DIM].astype(BF16)
            k = _rope(kz, cos, sin) * K_SCALE
            state = sstate_ref[0, 0, h]
            kd = (k * dec_sc[2 * RET_HEADS + h]).astype(BF16)
            sstate_ref[0, 0, h] = state * (GAMMA[h] ** CHUNK) + _dot_tn(kd, vb)
            qbs.append(_rope(qz, cos, sin).astype(BF16))
            kbs.append(k.astype(BF16))
            vbs.append(vb)
            states.append(state)
        scores = [_dot_nt(qbs[h], kbs[h]) for h in range(RET_HEADS)]
        cross = [_dot(qbs[h], states[h].astype(BF16)) for h in range(RET_HEADS)]
        saved[(s, c)] = (scores, cross, vbs)

    def outputs(s, c):
        z = zbuf.at[s % 2]
        rows = slice(c * CHUNK, (c + 1) * CHUNK)
        scores, cross, vbs = saved.pop((s, c))
        for h in range(RET_HEADS):
            gz = z[rows, OFF_G + h * HEAD_DIM:OFF_G + (h + 1) * HEAD_DIM]
            p = (scores[h] * dec_sc[h]).astype(BF16)
            o = _dot(p, vbs[h]) + cross[h] * dec_sc[RET_HEADS + h]
            on = o * lax.rsqrt(jnp.mean(o * o, axis=-1, keepdims=True) + EPS)
            ymix[s % 2, rows, CONV_CH + h * HEAD_DIM:CONV_CH + (h + 1) * HEAD_DIM] = (
                on * (gz * jax.nn.sigmoid(gz))).astype(BF16)

    norm_in(0)
    for p in range(IN_TOTAL // PROJ_BLOCK):
        project(0, p)
    for s in range(n_sub):
        fillers = []
        if s > 0:
            fillers += [functools.partial(out_project, s - 1, p)
                        for p in range(D_MODEL // OUT_BLOCK)]
        if s + 1 < n_sub:
            norm_in(s + 1)
            fillers += [functools.partial(project, s + 1, p)
                        for p in range(IN_TOTAL // PROJ_BLOCK)]
        steps = []
        for c in range(chunks_per_sub):
            steps.append(functools.partial(scores_and_state, s, c))
            if c > 0:
                steps.append(functools.partial(outputs, s, c - 1))
        steps.append(functools.partial(outputs, s, chunks_per_sub - 1))
        n_f, n_s = len(fillers), len(steps)
        placed = 0
        for i, step in enumerate(steps):
            while placed < n_f and placed * n_s <= i * n_f:
                fillers[placed]()
                placed += 1
            step()
            if i % 2 == 0:
                conv_chunk(s, i // 2)
        while placed < n_f:
            fillers[placed]()
            placed += 1
    for p in range(D_MODEL // OUT_BLOCK):
        out_project(n_sub - 1, p)

    tail = cbuf[8 + tm - 2:8 + tm, :]
    cbuf[6:8, :] = tail
    cstate_ref[0, 0] = tail


def _full(shape):
    return pl.BlockSpec(shape, lambda *_: (0,) * len(shape))


def _resident(shape):
    return pl.BlockSpec(shape, lambda *_: (0,) * len(shape), pipeline_mode=pl.Buffered(1))


def _mixer_call(x, norm_mix, w_in, conv_w, w_out, inv_freq, smeta, cmeta, w_up, w_down):
    bsz, seq, _ = x.shape
    tm = TOKEN_TILE
    n_t = seq // tm
    grid = (bsz, n_t)
    ff_slice = D_FF // (bsz * n_t)
    return pl.pallas_call(
        _mixer_kernel,
        grid=grid,
        in_specs=[
            pl.BlockSpec((1, tm, D_MODEL), lambda b, t: (b, t, 0)),
            _full((1, D_MODEL)),
            _resident((D_MODEL, IN_TOTAL)),
            _full((CONV_WIDTH, CONV_CH)),
            _resident((D_MODEL, D_MODEL)),
            _full((1, HEAD_DIM)),
            _full((RET_HEADS, HEAD_DIM, HEAD_DIM)),
            _full((CONV_WIDTH - 1, CONV_CH)),
            pl.BlockSpec((D_MODEL, ff_slice), lambda b, t: (0, b * n_t + t)),
            pl.BlockSpec((ff_slice, D_MODEL), lambda b, t: (b * n_t + t, 0)),
        ],
        out_specs=[
            pl.BlockSpec((1, tm, D_MODEL), lambda b, t: (b, t, 0)),
            pl.BlockSpec((1, 1, CONV_WIDTH - 1, CONV_CH), lambda b, t: (0, b, 0, 0)),
            pl.BlockSpec((1, 1, RET_HEADS, HEAD_DIM, HEAD_DIM), lambda b, t: (0, b, 0, 0, 0)),
            pl.BlockSpec((D_MODEL, ff_slice), lambda b, t: (0, b * n_t + t)),
            pl.BlockSpec((ff_slice, D_MODEL), lambda b, t: (b * n_t + t, 0)),
        ],
        out_shape=(
            jax.ShapeDtypeStruct((bsz, seq, D_MODEL), F32),
            jax.ShapeDtypeStruct((1, bsz, CONV_WIDTH - 1, CONV_CH), F32),
            jax.ShapeDtypeStruct((1, bsz, RET_HEADS, HEAD_DIM, HEAD_DIM), F32),
            jax.ShapeDtypeStruct((D_MODEL, D_FF), BF16),
            jax.ShapeDtypeStruct((D_FF, D_MODEL), BF16),
        ),
        scratch_shapes=[
            pltpu.VMEM((2, MIX_SUBTILE, IN_TOTAL), F32),
            pltpu.VMEM((tm + 8, CONV_CH), F32),
            pltpu.VMEM((2, MIX_SUBTILE, D_MODEL), BF16),
            pltpu.VMEM((2, MIX_SUBTILE, D_MODEL), BF16),
            pltpu.VMEM((seq, HEAD_DIM), F32),
            pltpu.VMEM((seq, HEAD_DIM), F32),
            pltpu.VMEM((3 * RET_HEADS, CHUNK, CHUNK), F32),
        ],
        compiler_params=pltpu.CompilerParams(
            dimension_semantics=("arbitrary", "arbitrary"), vmem_limit_bytes=VMEM_LIMIT_BYTES),
        name="prompt_mixer",
    )(x, norm_mix, w_in, conv_w, w_out, inv_freq, smeta, cmeta, w_up, w_down)


def _mlp_body(h, nm, wup_ref, wdown_ref, nf):
    hn = _rmsnorm(h, nm).astype(BF16)
    acc = h
    ff_block = 1024
    for c in range(D_FF // ff_block):
        up = _dot(hn, wup_ref[:, c * ff_block:(c + 1) * ff_block])
        act = jnp.square(jnp.maximum(up, 0.0)).astype(BF16)
        acc = acc + _dot(act, wdown_ref[c * ff_block:(c + 1) * ff_block, :])
    return _rmsnorm(acc, nf)


def _prompt_mlp_tile(h_ref, nm_ref, wup_ref, wdown_ref, nf_ref, y_ref, hn_sc, acc_sc):
    n_sub = MLP_TILE // MLP_SUBTILE
    ff_block = 1024
    n_ff = D_FF // ff_block

    def rows(s):
        return slice(s * MLP_SUBTILE, (s + 1) * MLP_SUBTILE)

    hn_sc[0] = _rmsnorm(h_ref[rows(0), :], nm_ref[...]).astype(BF16)
    for s in range(n_sub):
        hn = hn_sc[s % 2]
        acc = h_ref[rows(s), :]
        for c in range(n_ff):
            up = _dot(hn, wup_ref[:, c * ff_block:(c + 1) * ff_block])
            act = jnp.square(jnp.maximum(up, 0.0)).astype(BF16)
            acc = acc + _dot(act, wdown_ref[c * ff_block:(c + 1) * ff_block, :])
            if c == 0 and s > 0:
                y_ref[rows(s - 1), :] = _rmsnorm(acc_sc[...], nf_ref[...])
            if c == 1 and s + 1 < n_sub:
                hn_sc[(s + 1) % 2] = _rmsnorm(h_ref[rows(s + 1), :], nm_ref[...]).astype(BF16)
        if s + 1 < n_sub:
            acc_sc[...] = acc
        else:
            y_ref[rows(s), :] = _rmsnorm(acc, nf_ref[...])


def _sample_state_block(step, q_ref, kt_ref, v_ref, st_ref, new_ref, cross_sc):
    assert SAMPLE_BLOCK == 8
    lane = lax.broadcasted_iota(jnp.int32, (HEAD_DIM, HEAD_DIM), 1)
    sub = lax.broadcasted_iota(jnp.int32, (SAMPLE_BLOCK, HEAD_DIM), 0)
    row0 = pl.multiple_of(step * SAMPLE_BLOCK, SAMPLE_BLOCK)
    for h in range(RET_HEADS):
        cols = slice(h * HEAD_DIM, (h + 1) * HEAD_DIM)
        vb = v_ref[:, cols].astype(BF16)
        cross = jnp.zeros((SAMPLE_BLOCK, HEAD_DIM), F32)
        for i in range(SAMPLE_BLOCK):
            b = step * SAMPLE_BLOCK + i
            state = st_ref[i, h]
            q8 = jnp.broadcast_to(q_ref[i:i + 1, cols], (SAMPLE_BLOCK, HEAD_DIM)).astype(BF16)
            cross = jnp.where(sub == i, _dot(q8, state.astype(BF16)), cross)
            kt_b = jnp.where(lane == b, kt_ref[h], 0.0).astype(BF16)
            new_ref[i, h] = state * GAMMA[h] + _dot(kt_b, vb)
        cross_sc[pl.ds(row0, SAMPLE_BLOCK), cols] = cross * GAMMA[h]


def _sample_tail(x_ref, yconv_ref, inner_ref, gate_ref, wout_ref, nmlp_ref, wup_ref, wdown_ref,
                 nf_ref, cross_sc, ymix, y_ref):
    ymix[:, 0:CONV_CH] = yconv_ref[...].astype(BF16)
    for h in range(RET_HEADS):
        cols = slice(h * HEAD_DIM, (h + 1) * HEAD_DIM)
        o = inner_ref[:, cols] + cross_sc[:, cols]
        on = o * lax.rsqrt(jnp.mean(o * o, axis=-1, keepdims=True) + EPS)
        ymix[:, CONV_CH + h * HEAD_DIM:CONV_CH + (h + 1) * HEAD_DIM] = (
            on * gate_ref[:, cols]).astype(BF16)
    h1 = x_ref[...] + _dot(ymix[...], wout_ref[...])
    y_ref[...] = _mlp_body(h1, nmlp_ref[...], wup_ref, wdown_ref, nf_ref[...])


def _mlp_kernel(h_ref, nm_ref, wup_ref, wdown_ref, nf_ref,
                q_ref, kt_ref, v_ref, st_ref, xs_ref, yconv_ref, inner_ref, gate_ref, wout_ref,
                y_ref, new_ref, ys_ref,
                hn_sc, acc_sc, cross_sc, ymix_sc):
    step = pl.program_id(0)
    n_tiles = pl.num_programs(0) - 1

    @pl.when(step < n_tiles)
    def _():
        _sample_state_block(step, q_ref, kt_ref, v_ref, st_ref, new_ref, cross_sc)
        _prompt_mlp_tile(h_ref, nm_ref, wup_ref, wdown_ref, nf_ref, y_ref, hn_sc, acc_sc)

    @pl.when(step == n_tiles)
    def _():
        _sample_tail(xs_ref, yconv_ref, inner_ref, gate_ref, wout_ref, nm_ref, wup_ref, wdown_ref,
                     nf_ref, cross_sc, ymix_sc, ys_ref)


def _mlp_call(h, norm_mlp, w_up, w_down, norm_final, q, kt, v, state, xs, yconv, inner, gate, w_out):
    rows = h.shape[0]
    n = xs.shape[0]
    tm = MLP_TILE
    n_tiles = rows // tm
    bb = SAMPLE_BLOCK
    assert n_tiles * bb == n
    last = n_tiles - 1

    def tile(i):
        return jnp.minimum(i, last)

    return pl.pallas_call(
        _mlp_kernel,
        grid=(n_tiles + 1,),
        in_specs=[
            pl.BlockSpec((tm, D_MODEL), lambda i: (tile(i), 0)),
            _full((1, D_MODEL)),
            _resident((D_MODEL, D_FF)),
            _resident((D_FF, D_MODEL)),
            _full((1, D_MODEL)),
            pl.BlockSpec((bb, RET_WIDTH), lambda i: (tile(i), 0)),
            _full((RET_HEADS, HEAD_DIM, n)),
            _full((n, RET_WIDTH)),
            pl.BlockSpec((bb, RET_HEADS, HEAD_DIM, HEAD_DIM), lambda i: (tile(i), 0, 0, 0)),
            _full((n, D_MODEL)),
            _full((n, CONV_CH)),
            _full((n, RET_WIDTH)),
            _full((n, RET_WIDTH)),
            _resident((D_MODEL, D_MODEL)),
        ],
        out_specs=[
            pl.BlockSpec((tm, D_MODEL), lambda i: (tile(i), 0)),
            pl.BlockSpec((bb, RET_HEADS, HEAD_DIM, HEAD_DIM), lambda i: (tile(i), 0, 0, 0)),
            _full((n, D_MODEL)),
        ],
        out_shape=(jax.ShapeDtypeStruct((rows, D_MODEL), F32),
                   jax.ShapeDtypeStruct(state.shape, F32),
                   jax.ShapeDtypeStruct((n, D_MODEL), F32)),
        scratch_shapes=[pltpu.VMEM((2, MLP_SUBTILE, D_MODEL), BF16),
                        pltpu.VMEM((MLP_SUBTILE, D_MODEL), F32),
                        pltpu.VMEM((n, RET_WIDTH), F32),
                        pltpu.VMEM((n, D_MODEL), BF16)],
        compiler_params=pltpu.CompilerParams(
            dimension_semantics=("arbitrary",), vmem_limit_bytes=VMEM_LIMIT_BYTES),
        name="mlp_and_sample",
    )(h, norm_mlp, w_up, w_down, norm_final, q, kt, v, state, xs, yconv, inner, gate, w_out)


def kernel(x_prompt, x_sample, cache_conv, state_ret, meta_tokens, norm_mix, w_in, conv_w, w_out,
           norm_mlp, w_up, w_down, norm_final):
    bsz, seq, _ = x_prompt.shape
    n_dec = x_sample.shape[0]
    half = jnp.arange(0, HEAD_DIM, 2, dtype=F32) / HEAD_DIM
    inv_half = 1.0 / (ROPE_BASE ** half)
    inv_freq = jnp.concatenate([inv_half, inv_half])[None, :]

    nmix = norm_mix[0][None, :]
    nmlp = norm_mlp[0][None, :]
    nfin = norm_final[None, :]
    cw = conv_w[0]
    xs = x_sample.reshape(n_dec, D_MODEL)
    cache = cache_conv[0].reshape(n_dec, (CONV_WIDTH - 1) * CONV_CH)

    (w_in_b, w_out_b, smeta, cmeta, q, kt, v, inner, gate, yconv, cnew) = _prologue_call(
        meta_tokens, xs, cache, nmix, w_in[0], w_out[0], cw, inv_freq)
    h1, conv_p, ret_p, w_up_b, w_down_b = _mixer_call(
        x_prompt, nmix, w_in_b, cw, w_out_b, inv_freq, smeta, cmeta, w_up[0], w_down[0])
    y_prompt, ret_s, y_sample = _mlp_call(
        h1.reshape(bsz * seq, D_MODEL), nmlp, w_up_b, w_down_b, nfin,
        q, kt, v, state_ret[0], xs, yconv, inner, gate, w_out_b)
    y_prompt = y_prompt.reshape(bsz, seq, D_MODEL)

    return (y_prompt,
            y_sample.reshape(n_dec, 1, D_MODEL),
            conv_p,
            ret_p,
            cnew.reshape(1, n_dec, CONV_WIDTH - 1, CONV_CH),
            ret_s[None])
```

```python
import functools

import numpy as np
import jax
import jax.numpy as jnp
from jax import lax
from jax.experimental import pallas as pl
from jax.experimental.pallas import tpu as pltpu

D_MODEL = 1024
N_META = 16
CONV_CH = 512
CONV_WIDTH = 3
RET_HEADS = 4
HEAD_DIM = 128
RET_WIDTH = RET_HEADS * HEAD_DIM
CHUNK = 128
D_FF = 4 * D_MODEL
EPS = 1e-6
ROPE_BASE = 10000.0
PAST_LEN = 16384
K_SCALE = HEAD_DIM ** -0.5

OFF_U, OFF_C, OFF_B, OFF_Q, OFF_K, OFF_V, OFF_G = 0, 512, 1024, 1536, 2048, 2560, 3072
IN_TOTAL = 3584

GAMMA = tuple(1.0 - 2.0 ** (-5.0 - h) for h in range(RET_HEADS))
LOG_GAMMA = tuple(float(np.log(g)) for g in GAMMA)

TOKEN_TILE = 1024
MIX_SUBTILE = 512
PROJ_BLOCK = 512
OUT_BLOCK = 256
MLP_TILE = 1024
MLP_SUBTILE = 512
SAMPLE_BLOCK = 8
VMEM_LIMIT_BYTES = 60 * 1024 * 1024

F32 = jnp.float32
BF16 = jnp.bfloat16


def _rmsnorm(x, gain):
    return x * lax.rsqrt(jnp.mean(x * x, axis=-1, keepdims=True) + EPS) * gain


def _dot(a, b):
    return jnp.dot(a, b, preferred_element_type=F32)


def _dot_nt(a, b):
    return lax.dot_general(a, b, (((1,), (1,)), ((), ())), preferred_element_type=F32)


def _dot_tn(a, b):
    return lax.dot_general(a, b, (((0,), (0,)), ((), ())), preferred_element_type=F32)


def _rope_tables(pos, inv_freq):
    ang = pos * inv_freq
    lane = lax.broadcasted_iota(jnp.int32, ang.shape, 1)
    sin = jnp.sin(ang)
    return jnp.cos(ang), jnp.where(lane < HEAD_DIM // 2, -sin, sin)


def _rope(x, cos, sin):
    return x * cos + pltpu.roll(x, HEAD_DIM // 2, 1) * sin


def _decay_tables(dec_ref, n_valid=CHUNK):
    row = lax.broadcasted_iota(jnp.int32, (CHUNK, CHUNK), 0).astype(F32)
    col = lax.broadcasted_iota(jnp.int32, (CHUNK, CHUNK), 1).astype(F32)
    diff = row - col
    for h in range(RET_HEADS):
        lg = LOG_GAMMA[h]
        dec_ref[h] = jnp.where(diff >= 0, jnp.exp(lg * jnp.maximum(diff, 0.0)), 0.0)
        dec_ref[RET_HEADS + h] = jnp.exp((row + 1.0) * lg)
        dec_ref[2 * RET_HEADS + h] = jnp.where(row < n_valid, jnp.exp((n_valid - 1.0 - row) * lg), 0.0)


def _z_slice(zbuf, rows, off, width):
    blk, inner = divmod(off, PROJ_BLOCK)
    assert inner + width <= PROJ_BLOCK
    return zbuf[blk, rows, inner:inner + width]


def _meta_tail(zbuf, invf_ref, smeta_ref, cmeta_ref):
    rows = slice(0, CHUNK)
    cu = _z_slice(zbuf, rows, OFF_C, CONV_CH) * _z_slice(zbuf, rows, OFF_U, CONV_CH)
    cmeta_ref[...] = cu[N_META - 2:N_META, :]
    row = lax.broadcasted_iota(jnp.int32, (CHUNK, HEAD_DIM), 0).astype(F32)
    cos, sin = _rope_tables(row, invf_ref[...])
    for h in range(RET_HEADS):
        kz = _z_slice(zbuf, rows, OFF_K + h * HEAD_DIM, HEAD_DIM)
        vz = _z_slice(zbuf, rows, OFF_V + h * HEAD_DIM, HEAD_DIM)
        k = _rope(kz, cos, sin) * K_SCALE
        kdec = jnp.where(row < N_META, jnp.exp((N_META - 1.0 - row) * LOG_GAMMA[h]), 0.0)
        smeta_ref[h] = _dot_tn((k * kdec).astype(BF16), vz.astype(BF16))


def _sample_proj_tail(zbuf, rows, cache_ref, cw_ref, invf_ref,
                      q_ref, kt_ref, v_ref, inner_ref, gate_ref, yconv_ref, cnew_ref):
    prev0 = cache_ref[:, 0, :]
    prev1 = cache_ref[:, 1, :]
    cu = _z_slice(zbuf, rows, OFF_C, CONV_CH) * _z_slice(zbuf, rows, OFF_U, CONV_CH)
    conv = cw_ref[0, 0:1, :] * prev0 + cw_ref[0, 1:2, :] * prev1 + cw_ref[0, 2:3, :] * cu
    yconv_ref[...] = _z_slice(zbuf, rows, OFF_B, CONV_CH) * conv
    cnew_ref[:, 0, :] = prev1
    cnew_ref[:, 1, :] = cu
    pos = jnp.full((1, HEAD_DIM), float(PAST_LEN), F32)
    cos, sin = _rope_tables(pos, invf_ref[...])
    for h in range(RET_HEADS):
        cols = slice(h * HEAD_DIM, (h + 1) * HEAD_DIM)
        q = _rope(_z_slice(zbuf, rows, OFF_Q + h * HEAD_DIM, HEAD_DIM), cos, sin)
        k = _rope(_z_slice(zbuf, rows, OFF_K + h * HEAD_DIM, HEAD_DIM), cos, sin) * K_SCALE
        v = _z_slice(zbuf, rows, OFF_V + h * HEAD_DIM, HEAD_DIM)
        g = _z_slice(zbuf, rows, OFF_G + h * HEAD_DIM, HEAD_DIM)
        q_ref[:, cols] = q
        kt_ref[h] = k.T
        v_ref[:, cols] = v
        inner_ref[:, cols] = jnp.sum(q * k, axis=-1, keepdims=True) * v
        gate_ref[:, cols] = g * jax.nn.sigmoid(g)


def _prologue_kernel(meta_ref, xs_ref, cache_ref, nm_ref, win_ref, wout_ref, cw_ref, invf_ref,
                     winb_ref, woutb_ref, smeta_ref, cmeta_ref,
                     q_ref, kt_ref, v_ref, inner_ref, gate_ref, yconv_ref, cnew_ref,
                     xpad, hn_sc, zbuf):
    j = pl.program_id(0)
    n_dec = xs_ref.shape[0]

    @pl.when(j == 0)
    def _():
        xpad[...] = jnp.zeros_like(xpad)
        xpad[0:N_META, :] = meta_ref[...]
        hn_sc[0:CHUNK, :] = _rmsnorm(xpad[...], nm_ref[...]).astype(BF16)
        hn_sc[CHUNK:CHUNK + n_dec, :] = _rmsnorm(xs_ref[:, 0, :], nm_ref[...]).astype(BF16)
        woutb_ref[...] = wout_ref[...].astype(BF16)

    w_blk = win_ref[...].astype(BF16)
    winb_ref[...] = w_blk
    zbuf[j] = _dot(hn_sc[...], w_blk)

    @pl.when(j == pl.num_programs(0) - 1)
    def _():
        _meta_tail(zbuf, invf_ref, smeta_ref, cmeta_ref)
        _sample_proj_tail(zbuf, slice(CHUNK, CHUNK + n_dec), cache_ref, cw_ref, invf_ref,
                          q_ref, kt_ref, v_ref, inner_ref, gate_ref, yconv_ref, cnew_ref)


def _prologue_call(meta, xs, cache, norm_mix, w_in, w_out, conv_w, inv_freq):
    n = xs.shape[0]
    n_blk = IN_TOTAL // PROJ_BLOCK
    wide = jax.ShapeDtypeStruct((n, RET_WIDTH), F32)
    return pl.pallas_call(
        _prologue_kernel,
        grid=(n_blk,),
        in_specs=[
            _full((N_META, D_MODEL)),
            _full((n, 1, D_MODEL)),
            _full((n, CONV_WIDTH - 1, CONV_CH)),
            _full((1, D_MODEL)),
            pl.BlockSpec((D_MODEL, PROJ_BLOCK), lambda j: (0, j)),
            _full((D_MODEL, D_MODEL)),
            _full((1, CONV_WIDTH, CONV_CH)),
            _full((1, HEAD_DIM)),
        ],
        out_specs=[
            pl.BlockSpec((D_MODEL, PROJ_BLOCK), lambda j: (0, j)),
            _full((D_MODEL, D_MODEL)),
            _full((RET_HEADS, HEAD_DIM, HEAD_DIM)),
            _full((CONV_WIDTH - 1, CONV_CH)),
            _full((n, RET_WIDTH)),
            _full((RET_HEADS, HEAD_DIM, n)),
            _full((n, RET_WIDTH)),
            _full((n, RET_WIDTH)),
            _full((n, RET_WIDTH)),
            _full((n, CONV_CH)),
            _full((n, CONV_WIDTH - 1, CONV_CH)),
        ],
        out_shape=(jax.ShapeDtypeStruct((D_MODEL, IN_TOTAL), BF16),
                   jax.ShapeDtypeStruct((D_MODEL, D_MODEL), BF16),
                   jax.ShapeDtypeStruct((RET_HEADS, HEAD_DIM, HEAD_DIM), F32),
                   jax.ShapeDtypeStruct((CONV_WIDTH - 1, CONV_CH), F32),
                   wide,
                   jax.ShapeDtypeStruct((RET_HEADS, HEAD_DIM, n), F32),
                   wide, wide, wide,
                   jax.ShapeDtypeStruct((n, CONV_CH), F32),
                   jax.ShapeDtypeStruct((n, CONV_WIDTH - 1, CONV_CH), F32)),
        scratch_shapes=[pltpu.VMEM((CHUNK, D_MODEL), F32),
                        pltpu.VMEM((CHUNK + n, D_MODEL), BF16),
                        pltpu.VMEM((n_blk, CHUNK + n, PROJ_BLOCK), F32)],
        compiler_params=pltpu.CompilerParams(
            dimension_semantics=("arbitrary",), vmem_limit_bytes=VMEM_LIMIT_BYTES),
        name="prologue",
    )(meta, xs, cache, norm_mix, w_in, w_out, conv_w, inv_freq)


def _mixer_kernel(x_ref, nm_ref, win_ref, cw_ref, wout_ref, invf_ref, smeta_ref, cmeta_ref,
                  wup_ref, wdown_ref,
                  h1_ref, cstate_ref, sstate_ref, wup_bf_ref, wdown_bf_ref,
                  zbuf, cbuf, ymix, hn_sc, cos_sc, sin_sc, dec_sc):
    b = pl.program_id(0)
    t = pl.program_id(1)
    tm = TOKEN_TILE
    tile0 = pl.multiple_of(t * tm, tm)

    wup_bf_ref[...] = wup_ref[...].astype(BF16)
    wdown_bf_ref[...] = wdown_ref[...].astype(BF16)

    @pl.when(jnp.logical_and(t == 0, b == 0))
    def _():
        _decay_tables(dec_sc)

    @pl.when(b == 0)
    def _():
        pos = (lax.broadcasted_iota(jnp.int32, (tm, HEAD_DIM), 0) + (N_META + t * tm)).astype(F32)
        cos, sin = _rope_tables(pos, invf_ref[...])
        cos_sc[pl.ds(tile0, tm), :] = cos
        sin_sc[pl.ds(tile0, tm), :] = sin

    @pl.when(t == 0)
    def _():
        sstate_ref[0, 0] = smeta_ref[...]
        cbuf[6:8, :] = cmeta_ref[...]

    n_sub = tm // MIX_SUBTILE
    chunks_per_sub = MIX_SUBTILE // CHUNK

    def sub_rows(s):
        return slice(s * MIX_SUBTILE, (s + 1) * MIX_SUBTILE)

    def norm_in(s):
        hn_sc[s % 2] = _rmsnorm(x_ref[0, sub_rows(s), :], nm_ref[...]).astype(BF16)

    def project(s, p):
        cols = slice(p * PROJ_BLOCK, (p + 1) * PROJ_BLOCK)
        zbuf[s % 2, :, cols] = _dot(hn_sc[s % 2], win_ref[:, cols])

    def out_project(s, p):
        cols = slice(p * OUT_BLOCK, (p + 1) * OUT_BLOCK)
        h1_ref[0, sub_rows(s), cols] = (x_ref[0, sub_rows(s), cols]
                                        + _dot(ymix[s % 2], wout_ref[:, cols]))

    def conv_chunk(s, c):
        z = zbuf.at[s % 2]
        rows = slice(c * CHUNK, (c + 1) * CHUNK)
        g0 = (s * chunks_per_sub + c) * CHUNK
        cu = z[rows, OFF_C:OFF_C + CONV_CH] * z[rows, OFF_U:OFF_U + CONV_CH]
        cbuf[8 + g0:8 + g0 + CHUNK, :] = cu
        conv = (cw_ref[0, 0:1, :] * cbuf[6 + g0:6 + g0 + CHUNK, :]
                + cw_ref[0, 1:2, :] * cbuf[7 + g0:7 + g0 + CHUNK, :]
                + cw_ref[0, 2:3, :] * cu)
        ymix[s % 2, rows, 0:CONV_CH] = (z[rows, OFF_B:OFF_B + CONV_CH] * conv).astype(BF16)

    saved = {}

    def scores_and_state(s, c):
        z = zbuf.at[s % 2]
        rows = slice(c * CHUNK, (c + 1) * CHUNK)
        g0 = (s * chunks_per_sub + c) * CHUNK
        cos = cos_sc[pl.ds(tile0 + g0, CHUNK), :]
        sin = sin_sc[pl.ds(tile0 + g0, CHUNK), :]
        qbs, kbs, vbs, states = [], [], [], []
        for h in range(RET_HEADS):
            qz = z[rows, OFF_Q + h * HEAD_DIM:OFF_Q + (h + 1) * HEAD_DIM]
            kz = z[rows, OFF_K + h * HEAD_DIM:OFF_K + (h + 1) * HEAD_DIM]
            vb = z[rows, OFF_V + h * HEAD_DIM:OFF_V + (h + 1) * HEAD_DIM].astype(BF16)
            k = _rope(kz, cos, sin) * K_SCALE
            state = sstate_ref[0, 0, h]
            kd = (k * dec_sc[2 * RET_HEADS + h]).astype(BF16)
            sstate_ref[0, 0, h] = state * (GAMMA[h] ** CHUNK) + _dot_tn(kd, vb)
            qbs.append(_rope(qz, cos, sin).astype(BF16))
            kbs.append(k.astype(BF16))
            vbs.append(vb)
            states.append(state)
        scores = [_dot_nt(qbs[h], kbs[h]) for h in range(RET_HEADS)]
        cross = [_dot(qbs[h], states[h].astype(BF16)) for h in range(RET_HEADS)]
        saved[(s, c)] = (scores, cross, vbs)

    def outputs(s, c):
        z = zbuf.at[s % 2]
        rows = slice(c * CHUNK, (c + 1) * CHUNK)
        scores, cross, vbs = saved.pop((s, c))
        for h in range(RET_HEADS):
            gz = z[rows, OFF_G + h * HEAD_DIM:OFF_G + (h + 1) * HEAD_DIM]
            p = (scores[h] * dec_sc[h]).astype(BF16)
            o = _dot(p, vbs[h]) + cross[h] * dec_sc[RET_HEADS + h]
            on = o * lax.rsqrt(jnp.mean(o * o, axis=-1, keepdims=True) + EPS)
            ymix[s % 2, rows, CONV_CH + h * HEAD_DIM:CONV_CH + (h + 1) * HEAD_DIM] = (
                on * (gz * jax.nn.sigmoid(gz))).astype(BF16)

    norm_in(0)
    for p in range(IN_TOTAL // PROJ_BLOCK):
        project(0, p)
    for s in range(n_sub):
        fillers = []
        if s > 0:
            fillers += [functools.partial(out_project, s - 1, p)
                        for p in range(D_MODEL // OUT_BLOCK)]
        if s + 1 < n_sub:
            norm_in(s + 1)
            fillers += [functools.partial(project, s + 1, p)
                        for p in range(IN_TOTAL // PROJ_BLOCK)]
        steps = []
        for c in range(chunks_per_sub):
            steps.append(functools.partial(scores_and_state, s, c))
            if c > 0:
                steps.append(functools.partial(outputs, s, c - 1))
        steps.append(functools.partial(outputs, s, chunks_per_sub - 1))
        n_f, n_s = len(fillers), len(steps)
        placed = 0
        for i, step in enumerate(steps):
            while placed < n_f and placed * n_s <= i * n_f:
                fillers[placed]()
                placed += 1
            step()
            if i % 2 == 0:
                conv_chunk(s, i // 2)
        while placed < n_f:
            fillers[placed]()
            placed += 1
    for p in range(D_MODEL // OUT_BLOCK):
        out_project(n_sub - 1, p)

    tail = cbuf[8 + tm - 2:8 + tm, :]
    cbuf[6:8, :] = tail
    cstate_ref[0, 0] = tail


def _full(shape):
    return pl.BlockSpec(shape, lambda *_: (0,) * len(shape))


def _resident(shape):
    return pl.BlockSpec(shape, lambda *_: (0,) * len(shape), pipeline_mode=pl.Buffered(1))


def _mixer_call(x, norm_mix, w_in, conv_w, w_out, inv_freq, smeta, cmeta, w_up, w_down):
    bsz, seq, _ = x.shape
    tm = TOKEN_TILE
    n_t = seq // tm
    grid = (bsz, n_t)
    ff_slice = D_FF // (bsz * n_t)
    return pl.pallas_call(
        _mixer_kernel,
        grid=grid,
        in_specs=[
            pl.BlockSpec((1, tm, D_MODEL), lambda b, t: (b, t, 0)),
            _full((1, D_MODEL)),
            _resident((D_MODEL, IN_TOTAL)),
            _full((1, CONV_WIDTH, CONV_CH)),
            _resident((D_MODEL, D_MODEL)),
            _full((1, HEAD_DIM)),
            _full((RET_HEADS, HEAD_DIM, HEAD_DIM)),
            _full((CONV_WIDTH - 1, CONV_CH)),
            pl.BlockSpec((D_MODEL, ff_slice), lambda b, t: (0, b * n_t + t)),
            pl.BlockSpec((ff_slice, D_MODEL), lambda b, t: (b * n_t + t, 0)),
        ],
        out_specs=[
            pl.BlockSpec((1, tm, D_MODEL), lambda b, t: (b, t, 0)),
            pl.BlockSpec((1, 1, CONV_WIDTH - 1, CONV_CH), lambda b, t: (0, b, 0, 0)),
            pl.BlockSpec((1, 1, RET_HEADS, HEAD_DIM, HEAD_DIM), lambda b, t: (0, b, 0, 0, 0)),
            pl.BlockSpec((D_MODEL, ff_slice), lambda b, t: (0, b * n_t + t)),
            pl.BlockSpec((ff_slice, D_MODEL), lambda b, t: (b * n_t + t, 0)),
        ],
        out_shape=(
            jax.ShapeDtypeStruct((bsz, seq, D_MODEL), F32),
            jax.ShapeDtypeStruct((1, bsz, CONV_WIDTH - 1, CONV_CH), F32),
            jax.ShapeDtypeStruct((1, bsz, RET_HEADS, HEAD_DIM, HEAD_DIM), F32),
            jax.ShapeDtypeStruct((D_MODEL, D_FF), BF16),
            jax.ShapeDtypeStruct((D_FF, D_MODEL), BF16),
        ),
        scratch_shapes=[
            pltpu.VMEM((2, MIX_SUBTILE, IN_TOTAL), F32),
            pltpu.VMEM((tm + 8, CONV_CH), F32),
            pltpu.VMEM((2, MIX_SUBTILE, D_MODEL), BF16),
            pltpu.VMEM((2, MIX_SUBTILE, D_MODEL), BF16),
            pltpu.VMEM((seq, HEAD_DIM), F32),
            pltpu.VMEM((seq, HEAD_DIM), F32),
            pltpu.VMEM((3 * RET_HEADS, CHUNK, CHUNK), F32),
        ],
        compiler_params=pltpu.CompilerParams(
            dimension_semantics=("arbitrary", "arbitrary"), vmem_limit_bytes=VMEM_LIMIT_BYTES),
        name="prompt_mixer",
    )(x, norm_mix, w_in, conv_w, w_out, inv_freq, smeta, cmeta, w_up, w_down)


def _mlp_body(h, nm, wup_ref, wdown_ref, nf):
    hn = _rmsnorm(h, nm).astype(BF16)
    acc = h
    ff_block = 1024
    for c in range(D_FF // ff_block):
        up = _dot(hn, wup_ref[:, c * ff_block:(c + 1) * ff_block])
        act = jnp.square(jnp.maximum(up, 0.0)).astype(BF16)
        acc = acc + _dot(act, wdown_ref[c * ff_block:(c + 1) * ff_block, :])
    return _rmsnorm(acc, nf)


def _prompt_mlp_tile(h_ref, nm_ref, wup_ref, wdown_ref, nf_ref, y_ref, hn_sc, acc_sc):
    n_sub = MLP_TILE // MLP_SUBTILE
    ff_block = 1024
    n_ff = D_FF // ff_block

    def rows(s):
        return slice(s * MLP_SUBTILE, (s + 1) * MLP_SUBTILE)

    hn_sc[0] = _rmsnorm(h_ref[rows(0), :], nm_ref[...]).astype(BF16)
    for s in range(n_sub):
        hn = hn_sc[s % 2]
        acc = h_ref[rows(s), :]
        for c in range(n_ff):
            up = _dot(hn, wup_ref[:, c * ff_block:(c + 1) * ff_block])
            act = jnp.square(jnp.maximum(up, 0.0)).astype(BF16)
            acc = acc + _dot(act, wdown_ref[c * ff_block:(c + 1) * ff_block, :])
            if c == 0 and s > 0:
                y_ref[rows(s - 1), :] = _rmsnorm(acc_sc[...], nf_ref[...])
            if c == 1 and s + 1 < n_sub:
                hn_sc[(s + 1) % 2] = _rmsnorm(h_ref[rows(s + 1), :], nm_ref[...]).astype(BF16)
        if s + 1 < n_sub:
            acc_sc[...] = acc
        else:
            y_ref[rows(s), :] = _rmsnorm(acc, nf_ref[...])


def _sample_state_block(step, q_ref, kt_ref, v_ref, st_ref, new_ref, cross_sc):
    assert SAMPLE_BLOCK == 8
    lane = lax.broadcasted_iota(jnp.int32, (HEAD_DIM, HEAD_DIM), 1)
    sub = lax.broadcasted_iota(jnp.int32, (SAMPLE_BLOCK, HEAD_DIM), 0)
    row0 = pl.multiple_of(step * SAMPLE_BLOCK, SAMPLE_BLOCK)
    for h in range(RET_HEADS):
        cols = slice(h * HEAD_DIM, (h + 1) * HEAD_DIM)
        vb = v_ref[:, cols].astype(BF16)
        cross = jnp.zeros((SAMPLE_BLOCK, HEAD_DIM), F32)
        for i in range(SAMPLE_BLOCK):
            b = step * SAMPLE_BLOCK + i
            state = st_ref[i, h]
            q8 = jnp.broadcast_to(q_ref[i:i + 1, cols], (SAMPLE_BLOCK, HEAD_DIM)).astype(BF16)
            cross = jnp.where(sub == i, _dot(q8, state.astype(BF16)), cross)
            kt_b = jnp.where(lane == b, kt_ref[h], 0.0).astype(BF16)
            new_ref[i, h] = state * GAMMA[h] + _dot(kt_b, vb)
        cross_sc[pl.ds(row0, SAMPLE_BLOCK), cols] = cross * GAMMA[h]


def _sample_tail(x_ref, yconv_ref, inner_ref, gate_ref, wout_ref, nmlp_ref, wup_ref, wdown_ref,
                 nf_ref, cross_sc, ymix, y_ref):
    ymix[:, 0:CONV_CH] = yconv_ref[...].astype(BF16)
    for h in range(RET_HEADS):
        cols = slice(h * HEAD_DIM, (h + 1) * HEAD_DIM)
        o = inner_ref[:, cols] + cross_sc[:, cols]
        on = o * lax.rsqrt(jnp.mean(o * o, axis=-1, keepdims=True) + EPS)
        ymix[:, CONV_CH + h * HEAD_DIM:CONV_CH + (h + 1) * HEAD_DIM] = (
            on * gate_ref[:, cols]).astype(BF16)
    h1 = x_ref[:, 0, :] + _dot(ymix[...], wout_ref[...])
    y_ref[:, 0, :] = _mlp_body(h1, nmlp_ref[...], wup_ref, wdown_ref, nf_ref[...])


def _mlp_kernel(h_ref, nm_ref, wup_ref, wdown_ref, nf_ref,
                q_ref, kt_ref, v_ref, st_ref, xs_ref, yconv_ref, inner_ref, gate_ref, wout_ref,
                y_ref, new_ref, ys_ref,
                hn_sc, acc_sc, cross_sc, ymix_sc):
    step = pl.program_id(0)
    n_tiles = pl.num_programs(0) - 1

    @pl.when(step < n_tiles)
    def _():
        _sample_state_block(step, q_ref, kt_ref, v_ref, st_ref, new_ref, cross_sc)
        _prompt_mlp_tile(h_ref, nm_ref, wup_ref, wdown_ref, nf_ref, y_ref, hn_sc, acc_sc)

    @pl.when(step == n_tiles)
    def _():
        _sample_tail(xs_ref, yconv_ref, inner_ref, gate_ref, wout_ref, nm_ref, wup_ref, wdown_ref,
                     nf_ref, cross_sc, ymix_sc, ys_ref)


def _mlp_call(h, norm_mlp, w_up, w_down, norm_final, q, kt, v, state, xs, yconv, inner, gate, w_out):
    rows = h.shape[0]
    n = xs.shape[0]
    tm = MLP_TILE
    n_tiles = rows // tm
    bb = SAMPLE_BLOCK
    assert n_tiles * bb == n
    last = n_tiles - 1

    def tile(i):
        return jnp.minimum(i, last)

    return pl.pallas_call(
        _mlp_kernel,
        grid=(n_tiles + 1,),
        in_specs=[
            pl.BlockSpec((tm, D_MODEL), lambda i: (tile(i), 0)),
            _full((1, D_MODEL)),
            _resident((D_MODEL, D_FF)),
            _resident((D_FF, D_MODEL)),
            _full((1, D_MODEL)),
            pl.BlockSpec((bb, RET_WIDTH), lambda i: (tile(i), 0)),
            _full((RET_HEADS, HEAD_DIM, n)),
            _full((n, RET_WIDTH)),
            pl.BlockSpec((bb, RET_HEADS, HEAD_DIM, HEAD_DIM), lambda i: (tile(i), 0, 0, 0)),
            _full((n, 1, D_MODEL)),
            _full((n, CONV_CH)),
            _full((n, RET_WIDTH)),
            _full((n, RET_WIDTH)),
            _resident((D_MODEL, D_MODEL)),
        ],
        out_specs=[
            pl.BlockSpec((tm, D_MODEL), lambda i: (tile(i), 0)),
            pl.BlockSpec((bb, RET_HEADS, HEAD_DIM, HEAD_DIM), lambda i: (tile(i), 0, 0, 0)),
            _full((n, 1, D_MODEL)),
        ],
        out_shape=(jax.ShapeDtypeStruct((rows, D_MODEL), F32),
                   jax.ShapeDtypeStruct(state.shape, F32),
                   jax.ShapeDtypeStruct((n, 1, D_MODEL), F32)),
        scratch_shapes=[pltpu.VMEM((2, MLP_SUBTILE, D_MODEL), BF16),
                        pltpu.VMEM((MLP_SUBTILE, D_MODEL), F32),
                        pltpu.VMEM((n, RET_WIDTH), F32),
                        pltpu.VMEM((n, D_MODEL), BF16)],
        compiler_params=pltpu.CompilerParams(
            dimension_semantics=("arbitrary",), vmem_limit_bytes=VMEM_LIMIT_BYTES),
        name="mlp_and_sample",
    )(h, norm_mlp, w_up, w_down, norm_final, q, kt, v, state, xs, yconv, inner, gate, w_out)


def kernel(x_prompt, x_sample, cache_conv, state_ret, meta_tokens, norm_mix, w_in, conv_w, w_out,
           norm_mlp, w_up, w_down, norm_final):
    bsz, seq, _ = x_prompt.shape
    n_dec = x_sample.shape[0]
    half = jnp.arange(0, HEAD_DIM, 2, dtype=F32) / HEAD_DIM
    inv_half = 1.0 / (ROPE_BASE ** half)
    inv_freq = jnp.concatenate([inv_half, inv_half])[None, :]

    nmix = norm_mix[0][None, :]
    nmlp = norm_mlp[0][None, :]
    nfin = norm_final[None, :]

    (w_in_b, w_out_b, smeta, cmeta, q, kt, v, inner, gate, yconv, cnew) = _prologue_call(
        meta_tokens, x_sample, cache_conv[0], nmix, w_in[0], w_out[0], conv_w, inv_freq)
    h1, conv_p, ret_p, w_up_b, w_down_b = _mixer_call(
        x_prompt, nmix, w_in_b, conv_w, w_out_b, inv_freq, smeta, cmeta, w_up[0], w_down[0])
    y_prompt, ret_s, y_sample = _mlp_call(
        h1.reshape(bsz * seq, D_MODEL), nmlp, w_up_b, w_down_b, nfin,
        q, kt, v, state_ret[0], x_sample, yconv, inner, gate, w_out_b)
    y_prompt = y_prompt.reshape(bsz, seq, D_MODEL)

    return (y_prompt, y_sample, conv_p, ret_p, cnew[None], ret_s[None])
```

```python
import functools

import numpy as np
import jax
import jax.numpy as jnp
from jax import lax
from jax.experimental import pallas as pl
from jax.experimental.pallas import tpu as pltpu

D_MODEL = 1024
N_META = 16
CONV_CH = 512
CONV_WIDTH = 3
RET_HEADS = 4
HEAD_DIM = 128
RET_WIDTH = RET_HEADS * HEAD_DIM
CHUNK = 128
D_FF = 4 * D_MODEL
EPS = 1e-6
ROPE_BASE = 10000.0
PAST_LEN = 16384
K_SCALE = HEAD_DIM ** -0.5

OFF_U, OFF_C, OFF_B, OFF_Q, OFF_K, OFF_V, OFF_G = 0, 512, 1024, 1536, 2048, 2560, 3072
IN_TOTAL = 3584

GAMMA = tuple(1.0 - 2.0 ** (-5.0 - h) for h in range(RET_HEADS))
LOG_GAMMA = tuple(float(np.log(g)) for g in GAMMA)

TOKEN_TILE = 1024
MIX_SUBTILE = 512
PROJ_BLOCK = 512
OUT_BLOCK = 256
ROPE_PIECE = 256
MLP_TILE = 1024
MLP_SUBTILE = 512
SAMPLE_BLOCK = 8
VMEM_LIMIT_BYTES = 60 * 1024 * 1024

F32 = jnp.float32
BF16 = jnp.bfloat16


def _rmsnorm(x, gain):
    return x * lax.rsqrt(jnp.mean(x * x, axis=-1, keepdims=True) + EPS) * gain


def _dot(a, b):
    return jnp.dot(a, b, preferred_element_type=F32)


def _dot_nt(a, b):
    return lax.dot_general(a, b, (((1,), (1,)), ((), ())), preferred_element_type=F32)


def _dot_tn(a, b):
    return lax.dot_general(a, b, (((0,), (0,)), ((), ())), preferred_element_type=F32)


def _rope_tables(pos, inv_freq):
    ang = pos * inv_freq
    lane = lax.broadcasted_iota(jnp.int32, ang.shape, 1)
    sin = jnp.sin(ang)
    return jnp.cos(ang), jnp.where(lane < HEAD_DIM // 2, -sin, sin)


def _rope(x, cos, sin):
    return x * cos + pltpu.roll(x, HEAD_DIM // 2, 1) * sin


def _decay_tables(dec_ref):
    row = lax.broadcasted_iota(jnp.int32, (CHUNK, CHUNK), 0).astype(F32)
    col = lax.broadcasted_iota(jnp.int32, (CHUNK, CHUNK), 1).astype(F32)
    diff = row - col
    for h in range(RET_HEADS):
        lg = LOG_GAMMA[h]
        dec_ref[h] = jnp.where(diff >= 0, jnp.exp(lg * jnp.maximum(diff, 0.0)), 0.0)
        dec_ref[RET_HEADS + h] = jnp.exp((row + 1.0) * lg)
        dec_ref[2 * RET_HEADS + h] = jnp.exp((CHUNK - 1.0 - row) * lg)


def _full(shape):
    return pl.BlockSpec(shape, lambda *_: (0,) * len(shape))


def _resident(shape):
    return pl.BlockSpec(shape, lambda *_: (0,) * len(shape), pipeline_mode=pl.Buffered(1))


def _z_slice(zbuf, rows, off, width):
    blk, inner = divmod(off, PROJ_BLOCK)
    assert inner + width <= PROJ_BLOCK
    return zbuf[blk, rows, inner:inner + width]


def _meta_tail(zbuf, invf_ref, smeta_ref, cmeta_ref):
    rows = slice(0, CHUNK)
    cu = _z_slice(zbuf, rows, OFF_C, CONV_CH) * _z_slice(zbuf, rows, OFF_U, CONV_CH)
    cmeta_ref[...] = cu[N_META - 2:N_META, :]
    row = lax.broadcasted_iota(jnp.int32, (CHUNK, HEAD_DIM), 0).astype(F32)
    cos, sin = _rope_tables(row, invf_ref[...])
    for h in range(RET_HEADS):
        kz = _z_slice(zbuf, rows, OFF_K + h * HEAD_DIM, HEAD_DIM)
        vz = _z_slice(zbuf, rows, OFF_V + h * HEAD_DIM, HEAD_DIM)
        k = _rope(kz, cos, sin) * K_SCALE
        kdec = jnp.where(row < N_META, jnp.exp((N_META - 1.0 - row) * LOG_GAMMA[h]), 0.0)
        smeta_ref[h] = _dot_tn((k * kdec).astype(BF16), vz.astype(BF16))


def _sample_proj_tail(zbuf, rows, cache_ref, cw_ref, invf_ref,
                      q_ref, kt_ref, v_ref, inner_ref, gate_ref, yconv_ref, cnew_ref):
    prev0 = cache_ref[:, 0, :]
    prev1 = cache_ref[:, 1, :]
    cu = _z_slice(zbuf, rows, OFF_C, CONV_CH) * _z_slice(zbuf, rows, OFF_U, CONV_CH)
    conv = cw_ref[0, 0:1, :] * prev0 + cw_ref[0, 1:2, :] * prev1 + cw_ref[0, 2:3, :] * cu
    yconv_ref[...] = _z_slice(zbuf, rows, OFF_B, CONV_CH) * conv
    cnew_ref[:, 0, :] = prev1
    cnew_ref[:, 1, :] = cu
    pos = jnp.full((1, HEAD_DIM), float(PAST_LEN), F32)
    cos, sin = _rope_tables(pos, invf_ref[...])
    for h in range(RET_HEADS):
        cols = slice(h * HEAD_DIM, (h + 1) * HEAD_DIM)
        q = _rope(_z_slice(zbuf, rows, OFF_Q + h * HEAD_DIM, HEAD_DIM), cos, sin)
        k = _rope(_z_slice(zbuf, rows, OFF_K + h * HEAD_DIM, HEAD_DIM), cos, sin) * K_SCALE
        v = _z_slice(zbuf, rows, OFF_V + h * HEAD_DIM, HEAD_DIM)
        g = _z_slice(zbuf, rows, OFF_G + h * HEAD_DIM, HEAD_DIM)
        q_ref[:, cols] = q
        kt_ref[h] = k.T
        v_ref[:, cols] = v
        inner_ref[:, cols] = jnp.sum(q * k, axis=-1, keepdims=True) * v
        gate_ref[:, cols] = g * jax.nn.sigmoid(g)


def _prologue_kernel(meta_ref, xs_ref, cache_ref, nm_ref, win_ref, wout_ref, cw_ref, invf_ref,
                     winb_ref, woutb_ref, cos_ref, sin_ref, dec_ref, smeta_ref, cmeta_ref,
                     q_ref, kt_ref, v_ref, inner_ref, gate_ref, yconv_ref, cnew_ref,
                     xpad, hn_sc, zbuf):
    j = pl.program_id(0)
    n_steps = pl.num_programs(0)
    n_dec = xs_ref.shape[0]
    n_pieces = cos_ref.shape[0] // ROPE_PIECE

    def rope_piece(p):
        row0 = pl.multiple_of(p * ROPE_PIECE, ROPE_PIECE)
        pos = (lax.broadcasted_iota(jnp.int32, (ROPE_PIECE, HEAD_DIM), 0)
               + (N_META + p * ROPE_PIECE)).astype(F32)
        cos, sin = _rope_tables(pos, invf_ref[...])
        cos_ref[pl.ds(row0, ROPE_PIECE), :] = cos
        sin_ref[pl.ds(row0, ROPE_PIECE), :] = sin

    @pl.when(j == 0)
    def _():
        xpad[...] = jnp.zeros_like(xpad)
        xpad[0:N_META, :] = meta_ref[...]
        hn_sc[0:CHUNK, :] = _rmsnorm(xpad[...], nm_ref[...]).astype(BF16)
        hn_sc[CHUNK:CHUNK + n_dec, :] = _rmsnorm(xs_ref[:, 0, :], nm_ref[...]).astype(BF16)
        woutb_ref[...] = wout_ref[...].astype(BF16)
        _decay_tables(dec_ref)

    w_blk = win_ref[...].astype(BF16)
    winb_ref[...] = w_blk
    zbuf[j] = _dot(hn_sc[...], w_blk)
    rope_piece(j)

    @pl.when(j == n_steps - 1)
    def _():
        for p in range(IN_TOTAL // PROJ_BLOCK, n_pieces):
            rope_piece(p)
        _meta_tail(zbuf, invf_ref, smeta_ref, cmeta_ref)
        _sample_proj_tail(zbuf, slice(CHUNK, CHUNK + n_dec), cache_ref, cw_ref, invf_ref,
                          q_ref, kt_ref, v_ref, inner_ref, gate_ref, yconv_ref, cnew_ref)


def _prologue_call(meta, xs, cache, norm_mix, w_in, w_out, conv_w, inv_freq, seq):
    n = xs.shape[0]
    n_blk = IN_TOTAL // PROJ_BLOCK
    assert seq % ROPE_PIECE == 0 and seq // ROPE_PIECE >= n_blk
    wide = jax.ShapeDtypeStruct((n, RET_WIDTH), F32)
    return pl.pallas_call(
        _prologue_kernel,
        grid=(n_blk,),
        in_specs=[
            _full((N_META, D_MODEL)),
            _full((n, 1, D_MODEL)),
            _full((n, CONV_WIDTH - 1, CONV_CH)),
            _full((1, D_MODEL)),
            pl.BlockSpec((D_MODEL, PROJ_BLOCK), lambda j: (0, j)),
            _full((D_MODEL, D_MODEL)),
            _full((1, CONV_WIDTH, CONV_CH)),
            _full((1, HEAD_DIM)),
        ],
        out_specs=[
            pl.BlockSpec((D_MODEL, PROJ_BLOCK), lambda j: (0, j)),
            _full((D_MODEL, D_MODEL)),
            _full((seq, HEAD_DIM)),
            _full((seq, HEAD_DIM)),
            _full((3 * RET_HEADS, CHUNK, CHUNK)),
            _full((RET_HEADS, HEAD_DIM, HEAD_DIM)),
            _full((CONV_WIDTH - 1, CONV_CH)),
            _full((n, RET_WIDTH)),
            _full((RET_HEADS, HEAD_DIM, n)),
            _full((n, RET_WIDTH)),
            _full((n, RET_WIDTH)),
            _full((n, RET_WIDTH)),
            _full((n, CONV_CH)),
            _full((n, CONV_WIDTH - 1, CONV_CH)),
        ],
        out_shape=(jax.ShapeDtypeStruct((D_MODEL, IN_TOTAL), BF16),
                   jax.ShapeDtypeStruct((D_MODEL, D_MODEL), BF16),
                   jax.ShapeDtypeStruct((seq, HEAD_DIM), F32),
                   jax.ShapeDtypeStruct((seq, HEAD_DIM), F32),
                   jax.ShapeDtypeStruct((3 * RET_HEADS, CHUNK, CHUNK), F32),
                   jax.ShapeDtypeStruct((RET_HEADS, HEAD_DIM, HEAD_DIM), F32),
                   jax.ShapeDtypeStruct((CONV_WIDTH - 1, CONV_CH), F32),
                   wide,
                   jax.ShapeDtypeStruct((RET_HEADS, HEAD_DIM, n), F32),
                   wide, wide, wide,
                   jax.ShapeDtypeStruct((n, CONV_CH), F32),
                   jax.ShapeDtypeStruct((n, CONV_WIDTH - 1, CONV_CH), F32)),
        scratch_shapes=[pltpu.VMEM((CHUNK, D_MODEL), F32),
                        pltpu.VMEM((CHUNK + n, D_MODEL), BF16),
                        pltpu.VMEM((n_blk, CHUNK + n, PROJ_BLOCK), F32)],
        compiler_params=pltpu.CompilerParams(
            dimension_semantics=("arbitrary",), vmem_limit_bytes=VMEM_LIMIT_BYTES),
        name="prologue",
    )(meta, xs, cache, norm_mix, w_in, w_out, conv_w, inv_freq)


def _mixer_kernel(x_ref, nm_ref, win_ref, cw_ref, wout_ref, cos_ref, sin_ref, dec_ref,
                  smeta_ref, cmeta_ref, wup_ref, wdown_ref,
                  h1_ref, cstate_ref, sstate_ref, wup_bf_ref, wdown_bf_ref,
                  zbuf, cbuf, ymix, hn_sc):
    t = pl.program_id(1)
    tm = TOKEN_TILE

    wup_bf_ref[...] = wup_ref[...].astype(BF16)
    wdown_bf_ref[...] = wdown_ref[...].astype(BF16)

    @pl.when(t == 0)
    def _():
        sstate_ref[0, 0] = smeta_ref[...]
        cbuf[6:8, :] = cmeta_ref[...]

    n_sub = tm // MIX_SUBTILE
    chunks_per_sub = MIX_SUBTILE // CHUNK

    def sub_rows(s):
        return slice(s * MIX_SUBTILE, (s + 1) * MIX_SUBTILE)

    def stage_rows(s):
        hn_sc[s % 2] = _rmsnorm(x_ref[0, sub_rows(s), :], nm_ref[...]).astype(BF16)

    def project(s, p):
        cols = slice(p * PROJ_BLOCK, (p + 1) * PROJ_BLOCK)
        zbuf[s % 2, :, cols] = _dot(hn_sc[s % 2], win_ref[:, cols])

    def out_project(s, p):
        cols = slice(p * OUT_BLOCK, (p + 1) * OUT_BLOCK)
        h1_ref[0, sub_rows(s), cols] = (x_ref[0, sub_rows(s), cols]
                                        + _dot(ymix[s % 2], wout_ref[:, cols]))

    def conv_chunk(s, c):
        z = zbuf.at[s % 2]
        rows = slice(c * CHUNK, (c + 1) * CHUNK)
        g0 = (s * chunks_per_sub + c) * CHUNK
        cu = z[rows, OFF_C:OFF_C + CONV_CH] * z[rows, OFF_U:OFF_U + CONV_CH]
        cbuf[8 + g0:8 + g0 + CHUNK, :] = cu
        conv = (cw_ref[0, 0:1, :] * cbuf[6 + g0:6 + g0 + CHUNK, :]
                + cw_ref[0, 1:2, :] * cbuf[7 + g0:7 + g0 + CHUNK, :]
                + cw_ref[0, 2:3, :] * cu)
        ymix[s % 2, rows, 0:CONV_CH] = (z[rows, OFF_B:OFF_B + CONV_CH] * conv).astype(BF16)

    saved = {}

    def scores_and_state(s, c):
        z_cur = zbuf.at[s % 2]
        rows = slice(c * CHUNK, (c + 1) * CHUNK)
        g0 = (s * chunks_per_sub + c) * CHUNK
        cos = cos_ref[g0:g0 + CHUNK, :]
        sin = sin_ref[g0:g0 + CHUNK, :]
        qbs, kbs, vbs, states = [], [], [], []
        for h in range(RET_HEADS):
            qz = z_cur[rows, OFF_Q + h * HEAD_DIM:OFF_Q + (h + 1) * HEAD_DIM]
            kz = z_cur[rows, OFF_K + h * HEAD_DIM:OFF_K + (h + 1) * HEAD_DIM]
            vb = z_cur[rows, OFF_V + h * HEAD_DIM:OFF_V + (h + 1) * HEAD_DIM].astype(BF16)
            k = _rope(kz, cos, sin) * K_SCALE
            state = sstate_ref[0, 0, h]
            kd = (k * dec_ref[2 * RET_HEADS + h]).astype(BF16)
            sstate_ref[0, 0, h] = state * (GAMMA[h] ** CHUNK) + _dot_tn(kd, vb)
            qbs.append(_rope(qz, cos, sin).astype(BF16))
            kbs.append(k.astype(BF16))
            vbs.append(vb)
            states.append(state)
        scores = [_dot_nt(qbs[h], kbs[h]) for h in range(RET_HEADS)]
        cross = [_dot(qbs[h], states[h].astype(BF16)) for h in range(RET_HEADS)]
        saved[(s, c)] = (scores, cross, vbs)

    def outputs(s, c):
        z_cur = zbuf.at[s % 2]
        rows = slice(c * CHUNK, (c + 1) * CHUNK)
        scores, cross, vbs = saved.pop((s, c))
        for h in range(RET_HEADS):
            gz = z_cur[rows, OFF_G + h * HEAD_DIM:OFF_G + (h + 1) * HEAD_DIM]
            p = (scores[h] * dec_ref[h]).astype(BF16)
            o = _dot(p, vbs[h]) + cross[h] * dec_ref[RET_HEADS + h]
            on = o * lax.rsqrt(jnp.mean(o * o, axis=-1, keepdims=True) + EPS)
            ymix[s % 2, rows, CONV_CH + h * HEAD_DIM:CONV_CH + (h + 1) * HEAD_DIM] = (
                on * (gz * jax.nn.sigmoid(gz))).astype(BF16)

    stage_rows(0)
    for p in range(IN_TOTAL // PROJ_BLOCK):
        project(0, p)
    for s in range(n_sub):
        fillers = []
        if s > 0:
            fillers += [functools.partial(out_project, s - 1, p)
                        for p in range(D_MODEL // OUT_BLOCK)]
        if s + 1 < n_sub:
            stage_rows(s + 1)
            fillers += [functools.partial(project, s + 1, p)
                        for p in range(IN_TOTAL // PROJ_BLOCK)]
        steps = []
        for c in range(chunks_per_sub):
            steps.append(functools.partial(scores_and_state, s, c))
            if c > 0:
                steps.append(functools.partial(outputs, s, c - 1))
        steps.append(functools.partial(outputs, s, chunks_per_sub - 1))
        n_f, n_s = len(fillers), len(steps)
        placed = 0
        for i, step in enumerate(steps):
            while placed < n_f and placed * n_s <= i * n_f:
                fillers[placed]()
                placed += 1
            step()
            if i % 2 == 0:
                conv_chunk(s, i // 2)
        while placed < n_f:
            fillers[placed]()
            placed += 1
    for p in range(D_MODEL // OUT_BLOCK):
        out_project(n_sub - 1, p)

    tail = cbuf[8 + tm - 2:8 + tm, :]
    cbuf[6:8, :] = tail
    cstate_ref[0, 0] = tail


def _mixer_call(x, norm_mix, w_in, conv_w, w_out, cos, sin, dec, smeta, cmeta, w_up, w_down):
    bsz, seq, _ = x.shape
    tm = TOKEN_TILE
    n_t = seq // tm
    grid = (bsz, n_t)
    ff_slice = D_FF // (bsz * n_t)
    return pl.pallas_call(
        _mixer_kernel,
        grid=grid,
        in_specs=[
            pl.BlockSpec((1, tm, D_MODEL), lambda b, t: (b, t, 0)),
            _full((1, D_MODEL)),
            _resident((D_MODEL, IN_TOTAL)),
            _full((1, CONV_WIDTH, CONV_CH)),
            _resident((D_MODEL, D_MODEL)),
            pl.BlockSpec((tm, HEAD_DIM), lambda b, t: (t, 0)),
            pl.BlockSpec((tm, HEAD_DIM), lambda b, t: (t, 0)),
            _full((3 * RET_HEADS, CHUNK, CHUNK)),
            _full((RET_HEADS, HEAD_DIM, HEAD_DIM)),
            _full((CONV_WIDTH - 1, CONV_CH)),
            pl.BlockSpec((D_MODEL, ff_slice), lambda b, t: (0, b * n_t + t)),
            pl.BlockSpec((ff_slice, D_MODEL), lambda b, t: (b * n_t + t, 0)),
        ],
        out_specs=[
            pl.BlockSpec((1, tm, D_MODEL), lambda b, t: (b, t, 0)),
            pl.BlockSpec((1, 1, CONV_WIDTH - 1, CONV_CH), lambda b, t: (0, b, 0, 0)),
            pl.BlockSpec((1, 1, RET_HEADS, HEAD_DIM, HEAD_DIM), lambda b, t: (0, b, 0, 0, 0)),
            pl.BlockSpec((D_MODEL, ff_slice), lambda b, t: (0, b * n_t + t)),
            pl.BlockSpec((ff_slice, D_MODEL), lambda b, t: (b * n_t + t, 0)),
        ],
        out_shape=(
            jax.ShapeDtypeStruct((bsz, seq, D_MODEL), F32),
            jax.ShapeDtypeStruct((1, bsz, CONV_WIDTH - 1, CONV_CH), F32),
            jax.ShapeDtypeStruct((1, bsz, RET_HEADS, HEAD_DIM, HEAD_DIM), F32),
            jax.ShapeDtypeStruct((D_MODEL, D_FF), BF16),
            jax.ShapeDtypeStruct((D_FF, D_MODEL), BF16),
        ),
        scratch_shapes=[
            pltpu.VMEM((2, MIX_SUBTILE, IN_TOTAL), F32),
            pltpu.VMEM((tm + 8, CONV_CH), F32),
            pltpu.VMEM((2, MIX_SUBTILE, D_MODEL), BF16),
            pltpu.VMEM((2, MIX_SUBTILE, D_MODEL), BF16),
        ],
        compiler_params=pltpu.CompilerParams(
            dimension_semantics=("arbitrary", "arbitrary"), vmem_limit_bytes=VMEM_LIMIT_BYTES),
        name="prompt_mixer",
    )(x, norm_mix, w_in, conv_w, w_out, cos, sin, dec, smeta, cmeta, w_up, w_down)


def _mlp_body(h, nm, wup_ref, wdown_ref, nf):
    hn = _rmsnorm(h, nm).astype(BF16)
    acc = h
    ff_block = 1024
    for c in range(D_FF // ff_block):
        up = _dot(hn, wup_ref[:, c * ff_block:(c + 1) * ff_block])
        act = jnp.square(jnp.maximum(up, 0.0)).astype(BF16)
        acc = acc + _dot(act, wdown_ref[c * ff_block:(c + 1) * ff_block, :])
    return _rmsnorm(acc, nf)


def _prompt_mlp_tile(h_ref, nm_ref, wup_ref, wdown_ref, nf_ref, y_ref, hn_sc, acc_sc):
    n_sub = MLP_TILE // MLP_SUBTILE
    ff_block = 1024
    n_ff = D_FF // ff_block

    def rows(s):
        return slice(s * MLP_SUBTILE, (s + 1) * MLP_SUBTILE)

    hn_sc[0] = _rmsnorm(h_ref[rows(0), :], nm_ref[...]).astype(BF16)
    for s in range(n_sub):
        hn = hn_sc[s % 2]
        acc = h_ref[rows(s), :]
        for c in range(n_ff):
            up = _dot(hn, wup_ref[:, c * ff_block:(c + 1) * ff_block])
            act = jnp.square(jnp.maximum(up, 0.0)).astype(BF16)
            acc = acc + _dot(act, wdown_ref[c * ff_block:(c + 1) * ff_block, :])
            if c == 0 and s > 0:
                y_ref[rows(s - 1), :] = _rmsnorm(acc_sc[...], nf_ref[...])
            if c == 1 and s + 1 < n_sub:
                hn_sc[(s + 1) % 2] = _rmsnorm(h_ref[rows(s + 1), :], nm_ref[...]).astype(BF16)
        if s + 1 < n_sub:
            acc_sc[...] = acc
        else:
            y_ref[rows(s), :] = _rmsnorm(acc, nf_ref[...])


def _sample_state_block(step, q_ref, kt_ref, v_ref, st_ref, new_ref, cross_sc):
    assert SAMPLE_BLOCK == 8
    lane = lax.broadcasted_iota(jnp.int32, (HEAD_DIM, HEAD_DIM), 1)
    sub = lax.broadcasted_iota(jnp.int32, (SAMPLE_BLOCK, HEAD_DIM), 0)
    row0 = pl.multiple_of(step * SAMPLE_BLOCK, SAMPLE_BLOCK)
    for h in range(RET_HEADS):
        cols = slice(h * HEAD_DIM, (h + 1) * HEAD_DIM)
        vb = v_ref[:, cols].astype(BF16)
        cross = jnp.zeros((SAMPLE_BLOCK, HEAD_DIM), F32)
        for i in range(SAMPLE_BLOCK):
            b = step * SAMPLE_BLOCK + i
            state = st_ref[i, h]
            q8 = jnp.broadcast_to(q_ref[i:i + 1, cols], (SAMPLE_BLOCK, HEAD_DIM)).astype(BF16)
            cross = jnp.where(sub == i, _dot(q8, state.astype(BF16)), cross)
            kt_b = jnp.where(lane == b, kt_ref[h], 0.0).astype(BF16)
            new_ref[i, h] = state * GAMMA[h] + _dot(kt_b, vb)
        cross_sc[pl.ds(row0, SAMPLE_BLOCK), cols] = cross * GAMMA[h]


def _sample_tail(x_ref, yconv_ref, inner_ref, gate_ref, wout_ref, nmlp_ref, wup_ref, wdown_ref,
                 nf_ref, cross_sc, ymix, y_ref):
    ymix[:, 0:CONV_CH] = yconv_ref[...].astype(BF16)
    for h in range(RET_HEADS):
        cols = slice(h * HEAD_DIM, (h + 1) * HEAD_DIM)
        o = inner_ref[:, cols] + cross_sc[:, cols]
        on = o * lax.rsqrt(jnp.mean(o * o, axis=-1, keepdims=True) + EPS)
        ymix[:, CONV_CH + h * HEAD_DIM:CONV_CH + (h + 1) * HEAD_DIM] = (
            on * gate_ref[:, cols]).astype(BF16)
    h1 = x_ref[:, 0, :] + _dot(ymix[...], wout_ref[...])
    y_ref[:, 0, :] = _mlp_body(h1, nmlp_ref[...], wup_ref, wdown_ref, nf_ref[...])


def _mlp_kernel(h_ref, nm_ref, wup_ref, wdown_ref, nf_ref,
                q_ref, kt_ref, v_ref, st_ref, xs_ref, yconv_ref, inner_ref, gate_ref, wout_ref,
                y_ref, new_ref, ys_ref,
                hn_sc, acc_sc, cross_sc, ymix_sc):
    step = pl.program_id(0)
    n_tiles = pl.num_programs(0) - 1

    @pl.when(step < n_tiles)
    def _():
        _sample_state_block(step, q_ref, kt_ref, v_ref, st_ref, new_ref, cross_sc)
        _prompt_mlp_tile(h_ref, nm_ref, wup_ref, wdown_ref, nf_ref, y_ref, hn_sc, acc_sc)

    @pl.when(step == n_tiles)
    def _():
        _sample_tail(xs_ref, yconv_ref, inner_ref, gate_ref, wout_ref, nm_ref, wup_ref, wdown_ref,
                     nf_ref, cross_sc, ymix_sc, ys_ref)


def _mlp_call(h, norm_mlp, w_up, w_down, norm_final, q, kt, v, state, xs, yconv, inner, gate, w_out):
    rows = h.shape[0]
    n = xs.shape[0]
    tm = MLP_TILE
    n_tiles = rows // tm
    bb = SAMPLE_BLOCK
    assert n_tiles * bb == n
    last = n_tiles - 1

    def tile(i):
        return jnp.minimum(i, last)

    return pl.pallas_call(
        _mlp_kernel,
        grid=(n_tiles + 1,),
        in_specs=[
            pl.BlockSpec((tm, D_MODEL), lambda i: (tile(i), 0)),
            _full((1, D_MODEL)),
            _resident((D_MODEL, D_FF)),
            _resident((D_FF, D_MODEL)),
            _full((1, D_MODEL)),
            pl.BlockSpec((bb, RET_WIDTH), lambda i: (tile(i), 0)),
            _full((RET_HEADS, HEAD_DIM, n)),
            _full((n, RET_WIDTH)),
            pl.BlockSpec((bb, RET_HEADS, HEAD_DIM, HEAD_DIM), lambda i: (tile(i), 0, 0, 0)),
            _full((n, 1, D_MODEL)),
            _full((n, CONV_CH)),
            _full((n, RET_WIDTH)),
            _full((n, RET_WIDTH)),
            _resident((D_MODEL, D_MODEL)),
        ],
        out_specs=[
            pl.BlockSpec((tm, D_MODEL), lambda i: (tile(i), 0)),
            pl.BlockSpec((bb, RET_HEADS, HEAD_DIM, HEAD_DIM), lambda i: (tile(i), 0, 0, 0)),
            _full((n, 1, D_MODEL)),
        ],
        out_shape=(jax.ShapeDtypeStruct((rows, D_MODEL), F32),
                   jax.ShapeDtypeStruct(state.shape, F32),
                   jax.ShapeDtypeStruct((n, 1, D_MODEL), F32)),
        scratch_shapes=[pltpu.VMEM((2, MLP_SUBTILE, D_MODEL), BF16),
                        pltpu.VMEM((MLP_SUBTILE, D_MODEL), F32),
                        pltpu.VMEM((n, RET_WIDTH), F32),
                        pltpu.VMEM((n, D_MODEL), BF16)],
        compiler_params=pltpu.CompilerParams(
            dimension_semantics=("arbitrary",), vmem_limit_bytes=VMEM_LIMIT_BYTES),
        name="mlp_and_sample",
    )(h, norm_mlp, w_up, w_down, norm_final, q, kt, v, state, xs, yconv, inner, gate, w_out)


def kernel(x_prompt, x_sample, cache_conv, state_ret, meta_tokens, norm_mix, w_in, conv_w, w_out,
           norm_mlp, w_up, w_down, norm_final):
    bsz, seq, _ = x_prompt.shape
    half = jnp.arange(0, HEAD_DIM, 2, dtype=F32) / HEAD_DIM
    inv_half = 1.0 / (ROPE_BASE ** half)
    inv_freq = jnp.concatenate([inv_half, inv_half])[None, :]

    nmix = norm_mix[0][None, :]
    nmlp = norm_mlp[0][None, :]
    nfin = norm_final[None, :]

    (w_in_b, w_out_b, cos, sin, dec, smeta, cmeta,
     q, kt, v, inner, gate, yconv, cnew) = _prologue_call(
        meta_tokens, x_sample, cache_conv[0], nmix, w_in[0], w_out[0], conv_w, inv_freq, seq)
    h1, conv_p, ret_p, w_up_b, w_down_b = _mixer_call(
        x_prompt, nmix, w_in_b, conv_w, w_out_b, cos, sin, dec, smeta, cmeta, w_up[0], w_down[0])
    y_prompt, ret_s, y_sample = _mlp_call(
        h1.reshape(bsz * seq, D_MODEL), nmlp, w_up_b, w_down_b, nfin,
        q, kt, v, state_ret[0], x_sample, yconv, inner, gate, w_out_b)
    y_prompt = y_prompt.reshape(bsz, seq, D_MODEL)

    return (y_prompt, y_sample, conv_p, ret_p, cnew[None], ret_s[None])
```

```python
import functools

import numpy as np
import jax
import jax.numpy as jnp
from jax import lax
from jax.experimental import pallas as pl
from jax.experimental.pallas import tpu as pltpu
from jax.experimental.pallas import tpu_sc as plsc

D_MODEL = 1024
N_META = 16
CONV_CH = 512
CONV_WIDTH = 3
RET_HEADS = 4
HEAD_DIM = 128
RET_WIDTH = RET_HEADS * HEAD_DIM
CHUNK = 128
D_FF = 4 * D_MODEL
EPS = 1e-6
ROPE_BASE = 10000.0
PAST_LEN = 16384
K_SCALE = HEAD_DIM ** -0.5

OFF_U, OFF_C, OFF_B, OFF_Q, OFF_K, OFF_V, OFF_G = 0, 512, 1024, 1536, 2048, 2560, 3072
IN_TOTAL = 3584

GAMMA = tuple(1.0 - 2.0 ** (-5.0 - h) for h in range(RET_HEADS))
LOG_GAMMA = tuple(float(np.log(g)) for g in GAMMA)

TOKEN_TILE = 1024
MIX_SUBTILE = 512
PROJ_BLOCK = 512
OUT_BLOCK = 256
ROPE_PIECE = 256
MLP_TILE = 1024
MLP_SUBTILE = 512
SC_LANES = 16
VMEM_LIMIT_BYTES = 60 * 1024 * 1024

F32 = jnp.float32
BF16 = jnp.bfloat16


def _rmsnorm(x, gain):
    return x * lax.rsqrt(jnp.mean(x * x, axis=-1, keepdims=True) + EPS) * gain


def _dot(a, b):
    return jnp.dot(a, b, preferred_element_type=F32)


def _dot_nt(a, b):
    return lax.dot_general(a, b, (((1,), (1,)), ((), ())), preferred_element_type=F32)


def _dot_tn(a, b):
    return lax.dot_general(a, b, (((0,), (0,)), ((), ())), preferred_element_type=F32)


def _rope_tables(pos, inv_freq):
    ang = pos * inv_freq
    lane = lax.broadcasted_iota(jnp.int32, ang.shape, 1)
    sin = jnp.sin(ang)
    return jnp.cos(ang), jnp.where(lane < HEAD_DIM // 2, -sin, sin)


def _rope(x, cos, sin):
    return x * cos + pltpu.roll(x, HEAD_DIM // 2, 1) * sin


def _decay_tables(dec_ref):
    row = lax.broadcasted_iota(jnp.int32, (CHUNK, CHUNK), 0).astype(F32)
    col = lax.broadcasted_iota(jnp.int32, (CHUNK, CHUNK), 1).astype(F32)
    diff = row - col
    for h in range(RET_HEADS):
        lg = LOG_GAMMA[h]
        dec_ref[h] = jnp.where(diff >= 0, jnp.exp(lg * jnp.maximum(diff, 0.0)), 0.0)
        dec_ref[RET_HEADS + h] = jnp.exp((row + 1.0) * lg)
        dec_ref[2 * RET_HEADS + h] = jnp.exp((CHUNK - 1.0 - row) * lg)


def _full(shape):
    return pl.BlockSpec(shape, lambda *_: (0,) * len(shape))


def _resident(shape):
    return pl.BlockSpec(shape, lambda *_: (0,) * len(shape), pipeline_mode=pl.Buffered(1))


def _z_slice(zbuf, rows, off, width):
    blk, inner = divmod(off, PROJ_BLOCK)
    assert inner + width <= PROJ_BLOCK
    return zbuf[blk, rows, inner:inner + width]


def _meta_tail(zbuf, invf_ref, smeta_ref, cmeta_ref):
    rows = slice(0, CHUNK)
    cu = _z_slice(zbuf, rows, OFF_C, CONV_CH) * _z_slice(zbuf, rows, OFF_U, CONV_CH)
    cmeta_ref[...] = cu[N_META - 2:N_META, :]
    row = lax.broadcasted_iota(jnp.int32, (CHUNK, HEAD_DIM), 0).astype(F32)
    cos, sin = _rope_tables(row, invf_ref[...])
    for h in range(RET_HEADS):
        kz = _z_slice(zbuf, rows, OFF_K + h * HEAD_DIM, HEAD_DIM)
        vz = _z_slice(zbuf, rows, OFF_V + h * HEAD_DIM, HEAD_DIM)
        k = _rope(kz, cos, sin) * K_SCALE
        kdec = jnp.where(row < N_META, jnp.exp((N_META - 1.0 - row) * LOG_GAMMA[h]), 0.0)
        smeta_ref[h] = _dot_tn((k * kdec).astype(BF16), vz.astype(BF16))


def _sample_proj_tail(zbuf, rows, cache_ref, cw_ref, invf_ref,
                      q_ref, kt_ref, v_ref, inner_ref, gate_ref, yconv_ref, cnew_ref):
    prev0 = cache_ref[:, 0, :]
    prev1 = cache_ref[:, 1, :]
    cu = _z_slice(zbuf, rows, OFF_C, CONV_CH) * _z_slice(zbuf, rows, OFF_U, CONV_CH)
    conv = cw_ref[0, 0:1, :] * prev0 + cw_ref[0, 1:2, :] * prev1 + cw_ref[0, 2:3, :] * cu
    yconv_ref[...] = _z_slice(zbuf, rows, OFF_B, CONV_CH) * conv
    cnew_ref[:, 0, :] = prev1
    cnew_ref[:, 1, :] = cu
    pos = jnp.full((1, HEAD_DIM), float(PAST_LEN), F32)
    cos, sin = _rope_tables(pos, invf_ref[...])
    for h in range(RET_HEADS):
        cols = slice(h * HEAD_DIM, (h + 1) * HEAD_DIM)
        q = _rope(_z_slice(zbuf, rows, OFF_Q + h * HEAD_DIM, HEAD_DIM), cos, sin)
        k = _rope(_z_slice(zbuf, rows, OFF_K + h * HEAD_DIM, HEAD_DIM), cos, sin) * K_SCALE
        v = _z_slice(zbuf, rows, OFF_V + h * HEAD_DIM, HEAD_DIM)
        g = _z_slice(zbuf, rows, OFF_G + h * HEAD_DIM, HEAD_DIM)
        q_ref[h] = q
        kt_ref[h] = k
        v_ref[h] = v
        inner_ref[:, cols] = jnp.sum(q * k, axis=-1, keepdims=True) * v
        gate_ref[:, cols] = g * jax.nn.sigmoid(g)


def _prologue_kernel(meta_ref, xs_ref, cache_ref, nm_ref, win_ref, wout_ref, cw_ref, invf_ref,
                     winb_ref, woutb_ref, cos_ref, sin_ref, dec_ref, smeta_ref, cmeta_ref,
                     q_ref, kt_ref, v_ref, inner_ref, gate_ref, yconv_ref, cnew_ref,
                     xpad, hn_sc, zbuf):
    j = pl.program_id(0)
    n_steps = pl.num_programs(0)
    n_dec = xs_ref.shape[0]
    n_pieces = cos_ref.shape[0] // ROPE_PIECE

    def rope_piece(p):
        row0 = pl.multiple_of(p * ROPE_PIECE, ROPE_PIECE)
        pos = (lax.broadcasted_iota(jnp.int32, (ROPE_PIECE, HEAD_DIM), 0)
               + (N_META + p * ROPE_PIECE)).astype(F32)
        cos, sin = _rope_tables(pos, invf_ref[...])
        cos_ref[pl.ds(row0, ROPE_PIECE), :] = cos
        sin_ref[pl.ds(row0, ROPE_PIECE), :] = sin

    @pl.when(j == 0)
    def _():
        xpad[...] = jnp.zeros_like(xpad)
        xpad[0:N_META, :] = meta_ref[...]
        hn_sc[0:CHUNK, :] = _rmsnorm(xpad[...], nm_ref[...]).astype(BF16)
        hn_sc[CHUNK:CHUNK + n_dec, :] = _rmsnorm(xs_ref[:, 0, :], nm_ref[...]).astype(BF16)
        woutb_ref[...] = wout_ref[...].astype(BF16)
        _decay_tables(dec_ref)

    w_blk = win_ref[...].astype(BF16)
    winb_ref[...] = w_blk
    zbuf[j] = _dot(hn_sc[...], w_blk)
    rope_piece(j)

    @pl.when(j == n_steps - 1)
    def _():
        for p in range(IN_TOTAL // PROJ_BLOCK, n_pieces):
            rope_piece(p)
        _meta_tail(zbuf, invf_ref, smeta_ref, cmeta_ref)
        _sample_proj_tail(zbuf, slice(CHUNK, CHUNK + n_dec), cache_ref, cw_ref, invf_ref,
                          q_ref, kt_ref, v_ref, inner_ref, gate_ref, yconv_ref, cnew_ref)


def _prologue_call(meta, xs, cache, norm_mix, w_in, w_out, conv_w, inv_freq, seq):
    n = xs.shape[0]
    n_blk = IN_TOTAL // PROJ_BLOCK
    assert seq % ROPE_PIECE == 0 and seq // ROPE_PIECE >= n_blk
    wide = jax.ShapeDtypeStruct((n, RET_WIDTH), F32)
    per_head = jax.ShapeDtypeStruct((RET_HEADS, n, HEAD_DIM), F32)
    return pl.pallas_call(
        _prologue_kernel,
        grid=(n_blk,),
        in_specs=[
            _full((N_META, D_MODEL)),
            _full((n, 1, D_MODEL)),
            _full((n, CONV_WIDTH - 1, CONV_CH)),
            _full((1, D_MODEL)),
            pl.BlockSpec((D_MODEL, PROJ_BLOCK), lambda j: (0, j)),
            _full((D_MODEL, D_MODEL)),
            _full((1, CONV_WIDTH, CONV_CH)),
            _full((1, HEAD_DIM)),
        ],
        out_specs=[
            pl.BlockSpec((D_MODEL, PROJ_BLOCK), lambda j: (0, j)),
            _full((D_MODEL, D_MODEL)),
            _full((seq, HEAD_DIM)),
            _full((seq, HEAD_DIM)),
            _full((3 * RET_HEADS, CHUNK, CHUNK)),
            _full((RET_HEADS, HEAD_DIM, HEAD_DIM)),
            _full((CONV_WIDTH - 1, CONV_CH)),
            _full((RET_HEADS, n, HEAD_DIM)),
            _full((RET_HEADS, n, HEAD_DIM)),
            _full((RET_HEADS, n, HEAD_DIM)),
            _full((n, RET_WIDTH)),
            _full((n, RET_WIDTH)),
            _full((n, CONV_CH)),
            _full((n, CONV_WIDTH - 1, CONV_CH)),
        ],
        out_shape=(jax.ShapeDtypeStruct((D_MODEL, IN_TOTAL), BF16),
                   jax.ShapeDtypeStruct((D_MODEL, D_MODEL), BF16),
                   jax.ShapeDtypeStruct((seq, HEAD_DIM), F32),
                   jax.ShapeDtypeStruct((seq, HEAD_DIM), F32),
                   jax.ShapeDtypeStruct((3 * RET_HEADS, CHUNK, CHUNK), F32),
                   jax.ShapeDtypeStruct((RET_HEADS, HEAD_DIM, HEAD_DIM), F32),
                   jax.ShapeDtypeStruct((CONV_WIDTH - 1, CONV_CH), F32),
                   per_head, per_head, per_head,
                   wide, wide,
                   jax.ShapeDtypeStruct((n, CONV_CH), F32),
                   jax.ShapeDtypeStruct((n, CONV_WIDTH - 1, CONV_CH), F32)),
        scratch_shapes=[pltpu.VMEM((CHUNK, D_MODEL), F32),
                        pltpu.VMEM((CHUNK + n, D_MODEL), BF16),
                        pltpu.VMEM((n_blk, CHUNK + n, PROJ_BLOCK), F32)],
        compiler_params=pltpu.CompilerParams(
            dimension_semantics=("arbitrary",), vmem_limit_bytes=VMEM_LIMIT_BYTES),
        name="prologue",
    )(meta, xs, cache, norm_mix, w_in, w_out, conv_w, inv_freq)


def _mixer_kernel(x_ref, nm_ref, win_ref, cw_ref, wout_ref, cos_ref, sin_ref, dec_ref,
                  smeta_ref, cmeta_ref, wup_ref, wdown_ref,
                  h1_ref, cstate_ref, sstate_ref, wup_bf_ref, wdown_bf_ref,
                  zbuf, cbuf, ymix, hn_sc):
    t = pl.program_id(1)
    tm = TOKEN_TILE

    wup_bf_ref[...] = wup_ref[...].astype(BF16)
    wdown_bf_ref[...] = wdown_ref[...].astype(BF16)

    @pl.when(t == 0)
    def _():
        sstate_ref[0, 0] = smeta_ref[...]
        cbuf[6:8, :] = cmeta_ref[...]

    n_sub = tm // MIX_SUBTILE
    chunks_per_sub = MIX_SUBTILE // CHUNK

    def sub_rows(s):
        return slice(s * MIX_SUBTILE, (s + 1) * MIX_SUBTILE)

    def stage_rows(s):
        hn_sc[s % 2] = _rmsnorm(x_ref[0, sub_rows(s), :], nm_ref[...]).astype(BF16)

    def project(s, p):
        cols = slice(p * PROJ_BLOCK, (p + 1) * PROJ_BLOCK)
        zbuf[s % 2, :, cols] = _dot(hn_sc[s % 2], win_ref[:, cols])

    def out_project(s, p):
        cols = slice(p * OUT_BLOCK, (p + 1) * OUT_BLOCK)
        h1_ref[0, sub_rows(s), cols] = (x_ref[0, sub_rows(s), cols]
                                        + _dot(ymix[s % 2], wout_ref[:, cols]))

    def conv_chunk(s, c):
        z = zbuf.at[s % 2]
        rows = slice(c * CHUNK, (c + 1) * CHUNK)
        g0 = (s * chunks_per_sub + c) * CHUNK
        cu = z[rows, OFF_C:OFF_C + CONV_CH] * z[rows, OFF_U:OFF_U + CONV_CH]
        cbuf[8 + g0:8 + g0 + CHUNK, :] = cu
        conv = (cw_ref[0, 0:1, :] * cbuf[6 + g0:6 + g0 + CHUNK, :]
                + cw_ref[0, 1:2, :] * cbuf[7 + g0:7 + g0 + CHUNK, :]
                + cw_ref[0, 2:3, :] * cu)
        ymix[s % 2, rows, 0:CONV_CH] = (z[rows, OFF_B:OFF_B + CONV_CH] * conv).astype(BF16)

    saved = {}

    def scores_and_state(s, c):
        z_cur = zbuf.at[s % 2]
        rows = slice(c * CHUNK, (c + 1) * CHUNK)
        g0 = (s * chunks_per_sub + c) * CHUNK
        cos = cos_ref[g0:g0 + CHUNK, :]
        sin = sin_ref[g0:g0 + CHUNK, :]
        qbs, kbs, vbs, states = [], [], [], []
        for h in range(RET_HEADS):
            qz = z_cur[rows, OFF_Q + h * HEAD_DIM:OFF_Q + (h + 1) * HEAD_DIM]
            kz = z_cur[rows, OFF_K + h * HEAD_DIM:OFF_K + (h + 1) * HEAD_DIM]
            vb = z_cur[rows, OFF_V + h * HEAD_DIM:OFF_V + (h + 1) * HEAD_DIM].astype(BF16)
            k = _rope(kz, cos, sin) * K_SCALE
            state = sstate_ref[0, 0, h]
            kd = (k * dec_ref[2 * RET_HEADS + h]).astype(BF16)
            sstate_ref[0, 0, h] = state * (GAMMA[h] ** CHUNK) + _dot_tn(kd, vb)
            qbs.append(_rope(qz, cos, sin).astype(BF16))
            kbs.append(k.astype(BF16))
            vbs.append(vb)
            states.append(state)
        scores = [_dot_nt(qbs[h], kbs[h]) for h in range(RET_HEADS)]
        cross = [_dot(qbs[h], states[h].astype(BF16)) for h in range(RET_HEADS)]
        saved[(s, c)] = (scores, cross, vbs)

    def outputs(s, c):
        z_cur = zbuf.at[s % 2]
        rows = slice(c * CHUNK, (c + 1) * CHUNK)
        scores, cross, vbs = saved.pop((s, c))
        for h in range(RET_HEADS):
            gz = z_cur[rows, OFF_G + h * HEAD_DIM:OFF_G + (h + 1) * HEAD_DIM]
            p = (scores[h] * dec_ref[h]).astype(BF16)
            o = _dot(p, vbs[h]) + cross[h] * dec_ref[RET_HEADS + h]
            on = o * lax.rsqrt(jnp.mean(o * o, axis=-1, keepdims=True) + EPS)
            ymix[s % 2, rows, CONV_CH + h * HEAD_DIM:CONV_CH + (h + 1) * HEAD_DIM] = (
                on * (gz * jax.nn.sigmoid(gz))).astype(BF16)

    stage_rows(0)
    for p in range(IN_TOTAL // PROJ_BLOCK):
        project(0, p)
    for s in range(n_sub):
        fillers = []
        if s > 0:
            fillers += [functools.partial(out_project, s - 1, p)
                        for p in range(D_MODEL // OUT_BLOCK)]
        if s + 1 < n_sub:
            stage_rows(s + 1)
            fillers += [functools.partial(project, s + 1, p)
                        for p in range(IN_TOTAL // PROJ_BLOCK)]
        steps = []
        for c in range(chunks_per_sub):
            steps.append(functools.partial(scores_and_state, s, c))
            if c > 0:
                steps.append(functools.partial(outputs, s, c - 1))
        steps.append(functools.partial(outputs, s, chunks_per_sub - 1))
        n_f, n_s = len(fillers), len(steps)
        placed = 0
        for i, step in enumerate(steps):
            while placed < n_f and placed * n_s <= i * n_f:
                fillers[placed]()
                placed += 1
            step()
            if i % 2 == 0:
                conv_chunk(s, i // 2)
        while placed < n_f:
            fillers[placed]()
            placed += 1
    for p in range(D_MODEL // OUT_BLOCK):
        out_project(n_sub - 1, p)

    tail = cbuf[8 + tm - 2:8 + tm, :]
    cbuf[6:8, :] = tail
    cstate_ref[0, 0] = tail


def _mixer_call(x, norm_mix, w_in, conv_w, w_out, cos, sin, dec, smeta, cmeta, w_up, w_down):
    bsz, seq, _ = x.shape
    tm = TOKEN_TILE
    n_t = seq // tm
    grid = (bsz, n_t)
    ff_slice = D_FF // (bsz * n_t)
    return pl.pallas_call(
        _mixer_kernel,
        grid=grid,
        in_specs=[
            pl.BlockSpec((1, tm, D_MODEL), lambda b, t: (b, t, 0)),
            _full((1, D_MODEL)),
            _resident((D_MODEL, IN_TOTAL)),
            _full((1, CONV_WIDTH, CONV_CH)),
            _resident((D_MODEL, D_MODEL)),
            pl.BlockSpec((tm, HEAD_DIM), lambda b, t: (t, 0)),
            pl.BlockSpec((tm, HEAD_DIM), lambda b, t: (t, 0)),
            _full((3 * RET_HEADS, CHUNK, CHUNK)),
            _full((RET_HEADS, HEAD_DIM, HEAD_DIM)),
            _full((CONV_WIDTH - 1, CONV_CH)),
            pl.BlockSpec((D_MODEL, ff_slice), lambda b, t: (0, b * n_t + t)),
            pl.BlockSpec((ff_slice, D_MODEL), lambda b, t: (b * n_t + t, 0)),
        ],
        out_specs=[
            pl.BlockSpec((1, tm, D_MODEL), lambda b, t: (b, t, 0)),
            pl.BlockSpec((1, 1, CONV_WIDTH - 1, CONV_CH), lambda b, t: (0, b, 0, 0)),
            pl.BlockSpec((1, 1, RET_HEADS, HEAD_DIM, HEAD_DIM), lambda b, t: (0, b, 0, 0, 0)),
            pl.BlockSpec((D_MODEL, ff_slice), lambda b, t: (0, b * n_t + t)),
            pl.BlockSpec((ff_slice, D_MODEL), lambda b, t: (b * n_t + t, 0)),
        ],
        out_shape=(
            jax.ShapeDtypeStruct((bsz, seq, D_MODEL), F32),
            jax.ShapeDtypeStruct((1, bsz, CONV_WIDTH - 1, CONV_CH), F32),
            jax.ShapeDtypeStruct((1, bsz, RET_HEADS, HEAD_DIM, HEAD_DIM), F32),
            jax.ShapeDtypeStruct((D_MODEL, D_FF), BF16),
            jax.ShapeDtypeStruct((D_FF, D_MODEL), BF16),
        ),
        scratch_shapes=[
            pltpu.VMEM((2, MIX_SUBTILE, IN_TOTAL), F32),
            pltpu.VMEM((tm + 8, CONV_CH), F32),
            pltpu.VMEM((2, MIX_SUBTILE, D_MODEL), BF16),
            pltpu.VMEM((2, MIX_SUBTILE, D_MODEL), BF16),
        ],
        compiler_params=pltpu.CompilerParams(
            dimension_semantics=("arbitrary", "arbitrary"), vmem_limit_bytes=VMEM_LIMIT_BYTES),
        name="prompt_mixer",
    )(x, norm_mix, w_in, conv_w, w_out, cos, sin, dec, smeta, cmeta, w_up, w_down)


def _mlp_body(h, nm, wup_ref, wdown_ref, nf):
    hn = _rmsnorm(h, nm).astype(BF16)
    acc = h
    ff_block = 1024
    for c in range(D_FF // ff_block):
        up = _dot(hn, wup_ref[:, c * ff_block:(c + 1) * ff_block])
        act = jnp.square(jnp.maximum(up, 0.0)).astype(BF16)
        acc = acc + _dot(act, wdown_ref[c * ff_block:(c + 1) * ff_block, :])
    return _rmsnorm(acc, nf)


def _prompt_mlp_tile(h_ref, nm_ref, wup_ref, wdown_ref, nf_ref, y_ref, hn_sc, acc_sc):
    n_sub = MLP_TILE // MLP_SUBTILE
    ff_block = 1024
    n_ff = D_FF // ff_block

    def rows(s):
        return slice(s * MLP_SUBTILE, (s + 1) * MLP_SUBTILE)

    hn_sc[0] = _rmsnorm(h_ref[rows(0), :], nm_ref[...]).astype(BF16)
    for s in range(n_sub):
        hn = hn_sc[s % 2]
        acc = h_ref[rows(s), :]
        for c in range(n_ff):
            up = _dot(hn, wup_ref[:, c * ff_block:(c + 1) * ff_block])
            act = jnp.square(jnp.maximum(up, 0.0)).astype(BF16)
            acc = acc + _dot(act, wdown_ref[c * ff_block:(c + 1) * ff_block, :])
            if c == 0 and s > 0:
                y_ref[rows(s - 1), :] = _rmsnorm(acc_sc[...], nf_ref[...])
            if c == 1 and s + 1 < n_sub:
                hn_sc[(s + 1) % 2] = _rmsnorm(h_ref[rows(s + 1), :], nm_ref[...]).astype(BF16)
        if s + 1 < n_sub:
            acc_sc[...] = acc
        else:
            y_ref[rows(s), :] = _rmsnorm(acc, nf_ref[...])


def _sample_state_sc(q, k, v, state):
    n = state.shape[0]
    mesh = plsc.VectorSubcoreMesh(core_axis_name="core", subcore_axis_name="subcore")
    n_workers = mesh.num_cores * mesh.num_subcores
    per_worker = (n * RET_HEADS) // n_workers
    assert per_worker * n_workers == n * RET_HEADS
    n_vec = HEAD_DIM // SC_LANES

    def body(q_hbm, k_hbm, v_hbm, st_hbm, cross_hbm, new_hbm, s_in, s_out, qv, kv, vv, cv):
        worker = lax.axis_index("core") * mesh.num_subcores + lax.axis_index("subcore")

        @pl.loop(0, per_worker)
        def _(u):
            unit = worker * per_worker + u
            b = unit // RET_HEADS
            h = unit % RET_HEADS
            pltpu.sync_copy(st_hbm.at[b, h], s_in)
            pltpu.sync_copy(q_hbm.at[h, b], qv)
            pltpu.sync_copy(k_hbm.at[h, b], kv)
            pltpu.sync_copy(v_hbm.at[h, b], vv)
            gamma = jnp.float32(GAMMA[RET_HEADS - 1])
            for hh in range(RET_HEADS - 1):
                gamma = jnp.where(h == hh, jnp.float32(GAMMA[hh]), gamma)
            v_vecs = [vv[pl.ds(j * SC_LANES, SC_LANES)] for j in range(n_vec)]

            for j in range(n_vec):
                cv[pl.ds(j * SC_LANES, SC_LANES)] = jnp.zeros((SC_LANES,), F32)

            @pl.loop(0, HEAD_DIM)
            def _(d):
                idx = jnp.full((SC_LANES,), d, jnp.int32)
                qd = plsc.load_gather(qv, [idx])
                kd = plsc.load_gather(kv, [idx])
                for j in range(n_vec):
                    lanes = pl.ds(j * SC_LANES, SC_LANES)
                    r = s_in[d, lanes]
                    s_out[d, lanes] = gamma * r + kd * v_vecs[j]
                    plsc.addupdate(cv.at[lanes], qd * r)

            for j in range(n_vec):
                lanes = pl.ds(j * SC_LANES, SC_LANES)
                cv[lanes] = cv[lanes] * gamma
            pltpu.sync_copy(cv, cross_hbm.at[h, b])
            pltpu.sync_copy(s_out, new_hbm.at[b, h])

    return pl.kernel(
        body,
        out_type=(jax.ShapeDtypeStruct((RET_HEADS, n, HEAD_DIM), F32),
                  jax.ShapeDtypeStruct(state.shape, F32)),
        mesh=mesh,
        scratch_types=[pltpu.VMEM((HEAD_DIM, HEAD_DIM), F32),
                       pltpu.VMEM((HEAD_DIM, HEAD_DIM), F32),
                       pltpu.VMEM((HEAD_DIM,), F32),
                       pltpu.VMEM((HEAD_DIM,), F32),
                       pltpu.VMEM((HEAD_DIM,), F32),
                       pltpu.VMEM((HEAD_DIM,), F32)],
        compiler_params=pltpu.CompilerParams(needs_layout_passes=False),
        name="sample_state_sc",
    )(q, k, v, state)


def _sample_tail(x_ref, yconv_ref, inner_ref, gate_ref, cross_ref, wout_ref, nmlp_ref, wup_ref,
                 wdown_ref, nf_ref, ymix, y_ref):
    ymix[:, 0:CONV_CH] = yconv_ref[...].astype(BF16)
    for h in range(RET_HEADS):
        cols = slice(h * HEAD_DIM, (h + 1) * HEAD_DIM)
        o = inner_ref[:, cols] + cross_ref[h]
        on = o * lax.rsqrt(jnp.mean(o * o, axis=-1, keepdims=True) + EPS)
        ymix[:, CONV_CH + h * HEAD_DIM:CONV_CH + (h + 1) * HEAD_DIM] = (
            on * gate_ref[:, cols]).astype(BF16)
    h1 = x_ref[:, 0, :] + _dot(ymix[...], wout_ref[...])
    y_ref[:, 0, :] = _mlp_body(h1, nmlp_ref[...], wup_ref, wdown_ref, nf_ref[...])


def _mlp_kernel(h_ref, nm_ref, wup_ref, wdown_ref, nf_ref,
                xs_ref, yconv_ref, inner_ref, gate_ref, cross_ref, wout_ref,
                y_ref, ys_ref,
                hn_sc, acc_sc, ymix_sc):
    step = pl.program_id(0)
    n_tiles = pl.num_programs(0) - 1

    @pl.when(step < n_tiles)
    def _():
        _prompt_mlp_tile(h_ref, nm_ref, wup_ref, wdown_ref, nf_ref, y_ref, hn_sc, acc_sc)

    @pl.when(step == n_tiles)
    def _():
        _sample_tail(xs_ref, yconv_ref, inner_ref, gate_ref, cross_ref, wout_ref, nm_ref, wup_ref,
                     wdown_ref, nf_ref, ymix_sc, ys_ref)


def _mlp_call(h, norm_mlp, w_up, w_down, norm_final, xs, yconv, inner, gate, cross, w_out):
    rows = h.shape[0]
    n = xs.shape[0]
    tm = MLP_TILE
    n_tiles = rows // tm
    last = n_tiles - 1

    def tile(i):
        return jnp.minimum(i, last)

    return pl.pallas_call(
        _mlp_kernel,
        grid=(n_tiles + 1,),
        in_specs=[
            pl.BlockSpec((tm, D_MODEL), lambda i: (tile(i), 0)),
            _full((1, D_MODEL)),
            _resident((D_MODEL, D_FF)),
            _resident((D_FF, D_MODEL)),
            _full((1, D_MODEL)),
            _full((n, 1, D_MODEL)),
            _full((n, CONV_CH)),
            _full((n, RET_WIDTH)),
            _full((n, RET_WIDTH)),
            _full((RET_HEADS, n, HEAD_DIM)),
            _resident((D_MODEL, D_MODEL)),
        ],
        out_specs=[
            pl.BlockSpec((tm, D_MODEL), lambda i: (tile(i), 0)),
            _full((n, 1, D_MODEL)),
        ],
        out_shape=(jax.ShapeDtypeStruct((rows, D_MODEL), F32),
                   jax.ShapeDtypeStruct((n, 1, D_MODEL), F32)),
        scratch_shapes=[pltpu.VMEM((2, MLP_SUBTILE, D_MODEL), BF16),
                        pltpu.VMEM((MLP_SUBTILE, D_MODEL), F32),
                        pltpu.VMEM((n, D_MODEL), BF16)],
        compiler_params=pltpu.CompilerParams(
            dimension_semantics=("arbitrary",), vmem_limit_bytes=VMEM_LIMIT_BYTES),
        name="mlp_and_sample",
    )(h, norm_mlp, w_up, w_down, norm_final, xs, yconv, inner, gate, cross, w_out)


def kernel(x_prompt, x_sample, cache_conv, state_ret, meta_tokens, norm_mix, w_in, conv_w, w_out,
           norm_mlp, w_up, w_down, norm_final):
    bsz, seq, _ = x_prompt.shape
    half = jnp.arange(0, HEAD_DIM, 2, dtype=F32) / HEAD_DIM
    inv_half = 1.0 / (ROPE_BASE ** half)
    inv_freq = jnp.concatenate([inv_half, inv_half])[None, :]

    nmix = norm_mix[0][None, :]
    nmlp = norm_mlp[0][None, :]
    nfin = norm_final[None, :]

    (w_in_b, w_out_b, cos, sin, dec, smeta, cmeta,
     q, k, v, inner, gate, yconv, cnew) = _prologue_call(
        meta_tokens, x_sample, cache_conv[0], nmix, w_in[0], w_out[0], conv_w, inv_freq, seq)
    cross, ret_s = _sample_state_sc(q, k, v, state_ret[0])
    h1, conv_p, ret_p, w_up_b, w_down_b = _mixer_call(
        x_prompt, nmix, w_in_b, conv_w, w_out_b, cos, sin, dec, smeta, cmeta, w_up[0], w_down[0])
    y_prompt, y_sample = _mlp_call(
        h1.reshape(bsz * seq, D_MODEL), nmlp, w_up_b, w_down_b, nfin,
        x_sample, yconv, inner, gate, cross, w_out_b)
    y_prompt = y_prompt.reshape(bsz, seq, D_MODEL)

    return (y_prompt, y_sample, conv_p, ret_p, cnew[None], ret_s[None])
```

```python
import functools

import numpy as np
import jax
import jax.numpy as jnp
from jax import lax
from jax.experimental import pallas as pl
from jax.experimental.pallas import tpu as pltpu
from jax.experimental.pallas import tpu_sc as plsc

D_MODEL = 1024
N_META = 16
CONV_CH = 512
CONV_WIDTH = 3
RET_HEADS = 4
HEAD_DIM = 128
RET_WIDTH = RET_HEADS * HEAD_DIM
CHUNK = 128
D_FF = 4 * D_MODEL
EPS = 1e-6
ROPE_BASE = 10000.0
PAST_LEN = 16384
K_SCALE = HEAD_DIM ** -0.5

OFF_U, OFF_C, OFF_B, OFF_Q, OFF_K, OFF_V, OFF_G = 0, 512, 1024, 1536, 2048, 2560, 3072
IN_TOTAL = 3584

GAMMA = tuple(1.0 - 2.0 ** (-5.0 - h) for h in range(RET_HEADS))
LOG_GAMMA = tuple(float(np.log(g)) for g in GAMMA)

TOKEN_TILE = 1024
MIX_SUBTILE = 512
PROJ_BLOCK = 512
OUT_BLOCK = 256
ROPE_PIECE = 256
MLP_TILE = 1024
MLP_SUBTILE = 512
SC_LANES = 16
SC_ROWS = 8
VMEM_LIMIT_BYTES = 60 * 1024 * 1024

F32 = jnp.float32
BF16 = jnp.bfloat16


def _rmsnorm(x, gain):
    return x * lax.rsqrt(jnp.mean(x * x, axis=-1, keepdims=True) + EPS) * gain


def _dot(a, b):
    return jnp.dot(a, b, preferred_element_type=F32)


def _dot_nt(a, b):
    return lax.dot_general(a, b, (((1,), (1,)), ((), ())), preferred_element_type=F32)


def _dot_tn(a, b):
    return lax.dot_general(a, b, (((0,), (0,)), ((), ())), preferred_element_type=F32)


def _rope_tables(pos, inv_freq):
    ang = pos * inv_freq
    lane = lax.broadcasted_iota(jnp.int32, ang.shape, 1)
    sin = jnp.sin(ang)
    return jnp.cos(ang), jnp.where(lane < HEAD_DIM // 2, -sin, sin)


def _rope(x, cos, sin):
    return x * cos + pltpu.roll(x, HEAD_DIM // 2, 1) * sin


def _decay_tables(dec_ref):
    row = lax.broadcasted_iota(jnp.int32, (CHUNK, CHUNK), 0).astype(F32)
    col = lax.broadcasted_iota(jnp.int32, (CHUNK, CHUNK), 1).astype(F32)
    diff = row - col
    for h in range(RET_HEADS):
        lg = LOG_GAMMA[h]
        dec_ref[h] = jnp.where(diff >= 0, jnp.exp(lg * jnp.maximum(diff, 0.0)), 0.0)
        dec_ref[RET_HEADS + h] = jnp.exp((row + 1.0) * lg)
        dec_ref[2 * RET_HEADS + h] = jnp.exp((CHUNK - 1.0 - row) * lg)


def _full(shape):
    return pl.BlockSpec(shape, lambda *_: (0,) * len(shape))


def _resident(shape):
    return pl.BlockSpec(shape, lambda *_: (0,) * len(shape), pipeline_mode=pl.Buffered(1))


def _z_slice(zbuf, rows, off, width):
    blk, inner = divmod(off, PROJ_BLOCK)
    assert inner + width <= PROJ_BLOCK
    return zbuf[blk, rows, inner:inner + width]


def _meta_tail(zbuf, invf_ref, smeta_ref, cmeta_ref):
    rows = slice(0, CHUNK)
    cu = _z_slice(zbuf, rows, OFF_C, CONV_CH) * _z_slice(zbuf, rows, OFF_U, CONV_CH)
    cmeta_ref[...] = cu[N_META - 2:N_META, :]
    row = lax.broadcasted_iota(jnp.int32, (CHUNK, HEAD_DIM), 0).astype(F32)
    cos, sin = _rope_tables(row, invf_ref[...])
    for h in range(RET_HEADS):
        kz = _z_slice(zbuf, rows, OFF_K + h * HEAD_DIM, HEAD_DIM)
        vz = _z_slice(zbuf, rows, OFF_V + h * HEAD_DIM, HEAD_DIM)
        k = _rope(kz, cos, sin) * K_SCALE
        kdec = jnp.where(row < N_META, jnp.exp((N_META - 1.0 - row) * LOG_GAMMA[h]), 0.0)
        smeta_ref[h] = _dot_tn((k * kdec).astype(BF16), vz.astype(BF16))


def _sample_proj_tail(zbuf, rows, cache_ref, cw_ref, invf_ref,
                      q_ref, kt_ref, v_ref, inner_ref, gate_ref, yconv_ref, cnew_ref):
    prev0 = cache_ref[:, 0, :]
    prev1 = cache_ref[:, 1, :]
    cu = _z_slice(zbuf, rows, OFF_C, CONV_CH) * _z_slice(zbuf, rows, OFF_U, CONV_CH)
    conv = cw_ref[0, 0:1, :] * prev0 + cw_ref[0, 1:2, :] * prev1 + cw_ref[0, 2:3, :] * cu
    yconv_ref[...] = _z_slice(zbuf, rows, OFF_B, CONV_CH) * conv
    cnew_ref[:, 0, :] = prev1
    cnew_ref[:, 1, :] = cu
    pos = jnp.full((1, HEAD_DIM), float(PAST_LEN), F32)
    cos, sin = _rope_tables(pos, invf_ref[...])
    for h in range(RET_HEADS):
        cols = slice(h * HEAD_DIM, (h + 1) * HEAD_DIM)
        q = _rope(_z_slice(zbuf, rows, OFF_Q + h * HEAD_DIM, HEAD_DIM), cos, sin)
        k = _rope(_z_slice(zbuf, rows, OFF_K + h * HEAD_DIM, HEAD_DIM), cos, sin) * K_SCALE
        v = _z_slice(zbuf, rows, OFF_V + h * HEAD_DIM, HEAD_DIM)
        g = _z_slice(zbuf, rows, OFF_G + h * HEAD_DIM, HEAD_DIM)
        q_ref[h] = q
        kt_ref[h] = k
        v_ref[h] = v
        inner_ref[:, cols] = jnp.sum(q * k, axis=-1, keepdims=True) * v
        gate_ref[:, cols] = g * jax.nn.sigmoid(g)


def _prologue_kernel(meta_ref, xs_ref, cache_ref, nm_ref, win_ref, wout_ref, cw_ref, invf_ref,
                     winb_ref, woutb_ref, cos_ref, sin_ref, dec_ref, smeta_ref, cmeta_ref,
                     q_ref, kt_ref, v_ref, inner_ref, gate_ref, yconv_ref, cnew_ref,
                     xpad, hn_sc, zbuf):
    j = pl.program_id(0)
    n_steps = pl.num_programs(0)
    n_dec = xs_ref.shape[0]
    n_pieces = cos_ref.shape[0] // ROPE_PIECE

    def rope_piece(p):
        row0 = pl.multiple_of(p * ROPE_PIECE, ROPE_PIECE)
        pos = (lax.broadcasted_iota(jnp.int32, (ROPE_PIECE, HEAD_DIM), 0)
               + (N_META + p * ROPE_PIECE)).astype(F32)
        cos, sin = _rope_tables(pos, invf_ref[...])
        cos_ref[pl.ds(row0, ROPE_PIECE), :] = cos
        sin_ref[pl.ds(row0, ROPE_PIECE), :] = sin

    @pl.when(j == 0)
    def _():
        xpad[...] = jnp.zeros_like(xpad)
        xpad[0:N_META, :] = meta_ref[...]
        hn_sc[0:CHUNK, :] = _rmsnorm(xpad[...], nm_ref[...]).astype(BF16)
        hn_sc[CHUNK:CHUNK + n_dec, :] = _rmsnorm(xs_ref[:, 0, :], nm_ref[...]).astype(BF16)
        woutb_ref[...] = wout_ref[...].astype(BF16)
        _decay_tables(dec_ref)

    w_blk = win_ref[...].astype(BF16)
    winb_ref[...] = w_blk
    zbuf[j] = _dot(hn_sc[...], w_blk)
    rope_piece(j)

    @pl.when(j == n_steps - 1)
    def _():
        for p in range(IN_TOTAL // PROJ_BLOCK, n_pieces):
            rope_piece(p)
        _meta_tail(zbuf, invf_ref, smeta_ref, cmeta_ref)
        _sample_proj_tail(zbuf, slice(CHUNK, CHUNK + n_dec), cache_ref, cw_ref, invf_ref,
                          q_ref, kt_ref, v_ref, inner_ref, gate_ref, yconv_ref, cnew_ref)


def _prologue_call(meta, xs, cache, norm_mix, w_in, w_out, conv_w, inv_freq, seq):
    n = xs.shape[0]
    n_blk = IN_TOTAL // PROJ_BLOCK
    assert seq % ROPE_PIECE == 0 and seq // ROPE_PIECE >= n_blk
    wide = jax.ShapeDtypeStruct((n, RET_WIDTH), F32)
    per_head = jax.ShapeDtypeStruct((RET_HEADS, n, HEAD_DIM), F32)
    return pl.pallas_call(
        _prologue_kernel,
        grid=(n_blk,),
        in_specs=[
            _full((N_META, D_MODEL)),
            _full((n, 1, D_MODEL)),
            _full((n, CONV_WIDTH - 1, CONV_CH)),
            _full((1, D_MODEL)),
            pl.BlockSpec((D_MODEL, PROJ_BLOCK), lambda j: (0, j)),
            _full((D_MODEL, D_MODEL)),
            _full((1, CONV_WIDTH, CONV_CH)),
            _full((1, HEAD_DIM)),
        ],
        out_specs=[
            pl.BlockSpec((D_MODEL, PROJ_BLOCK), lambda j: (0, j)),
            _full((D_MODEL, D_MODEL)),
            _full((seq, HEAD_DIM)),
            _full((seq, HEAD_DIM)),
            _full((3 * RET_HEADS, CHUNK, CHUNK)),
            _full((RET_HEADS, HEAD_DIM, HEAD_DIM)),
            _full((CONV_WIDTH - 1, CONV_CH)),
            _full((RET_HEADS, n, HEAD_DIM)),
            _full((RET_HEADS, n, HEAD_DIM)),
            _full((RET_HEADS, n, HEAD_DIM)),
            _full((n, RET_WIDTH)),
            _full((n, RET_WIDTH)),
            _full((n, CONV_CH)),
            _full((n, CONV_WIDTH - 1, CONV_CH)),
        ],
        out_shape=(jax.ShapeDtypeStruct((D_MODEL, IN_TOTAL), BF16),
                   jax.ShapeDtypeStruct((D_MODEL, D_MODEL), BF16),
                   jax.ShapeDtypeStruct((seq, HEAD_DIM), F32),
                   jax.ShapeDtypeStruct((seq, HEAD_DIM), F32),
                   jax.ShapeDtypeStruct((3 * RET_HEADS, CHUNK, CHUNK), F32),
                   jax.ShapeDtypeStruct((RET_HEADS, HEAD_DIM, HEAD_DIM), F32),
                   jax.ShapeDtypeStruct((CONV_WIDTH - 1, CONV_CH), F32),
                   per_head, per_head, per_head,
                   wide, wide,
                   jax.ShapeDtypeStruct((n, CONV_CH), F32),
                   jax.ShapeDtypeStruct((n, CONV_WIDTH - 1, CONV_CH), F32)),
        scratch_shapes=[pltpu.VMEM((CHUNK, D_MODEL), F32),
                        pltpu.VMEM((CHUNK + n, D_MODEL), BF16),
                        pltpu.VMEM((n_blk, CHUNK + n, PROJ_BLOCK), F32)],
        compiler_params=pltpu.CompilerParams(
            dimension_semantics=("arbitrary",), vmem_limit_bytes=VMEM_LIMIT_BYTES),
        name="prologue",
    )(meta, xs, cache, norm_mix, w_in, w_out, conv_w, inv_freq)


def _mixer_kernel(x_ref, nm_ref, win_ref, cw_ref, wout_ref, cos_ref, sin_ref, dec_ref,
                  smeta_ref, cmeta_ref, wup_ref, wdown_ref,
                  h1_ref, cstate_ref, sstate_ref, wup_bf_ref, wdown_bf_ref,
                  zbuf, cbuf, ymix, hn_sc):
    t = pl.program_id(1)
    tm = TOKEN_TILE

    wup_bf_ref[...] = wup_ref[...].astype(BF16)
    wdown_bf_ref[...] = wdown_ref[...].astype(BF16)

    @pl.when(t == 0)
    def _():
        sstate_ref[0, 0] = smeta_ref[...]
        cbuf[6:8, :] = cmeta_ref[...]

    n_sub = tm // MIX_SUBTILE
    chunks_per_sub = MIX_SUBTILE // CHUNK

    def sub_rows(s):
        return slice(s * MIX_SUBTILE, (s + 1) * MIX_SUBTILE)

    def stage_rows(s):
        hn_sc[s % 2] = _rmsnorm(x_ref[0, sub_rows(s), :], nm_ref[...]).astype(BF16)

    def project(s, p):
        cols = slice(p * PROJ_BLOCK, (p + 1) * PROJ_BLOCK)
        zbuf[s % 2, :, cols] = _dot(hn_sc[s % 2], win_ref[:, cols])

    def out_project(s, p):
        cols = slice(p * OUT_BLOCK, (p + 1) * OUT_BLOCK)
        h1_ref[0, sub_rows(s), cols] = (x_ref[0, sub_rows(s), cols]
                                        + _dot(ymix[s % 2], wout_ref[:, cols]))

    def conv_chunk(s, c):
        z = zbuf.at[s % 2]
        rows = slice(c * CHUNK, (c + 1) * CHUNK)
        g0 = (s * chunks_per_sub + c) * CHUNK
        cu = z[rows, OFF_C:OFF_C + CONV_CH] * z[rows, OFF_U:OFF_U + CONV_CH]
        cbuf[8 + g0:8 + g0 + CHUNK, :] = cu
        conv = (cw_ref[0, 0:1, :] * cbuf[6 + g0:6 + g0 + CHUNK, :]
                + cw_ref[0, 1:2, :] * cbuf[7 + g0:7 + g0 + CHUNK, :]
                + cw_ref[0, 2:3, :] * cu)
        ymix[s % 2, rows, 0:CONV_CH] = (z[rows, OFF_B:OFF_B + CONV_CH] * conv).astype(BF16)

    saved = {}

    def scores_and_state(s, c):
        z_cur = zbuf.at[s % 2]
        rows = slice(c * CHUNK, (c + 1) * CHUNK)
        g0 = (s * chunks_per_sub + c) * CHUNK
        cos = cos_ref[g0:g0 + CHUNK, :]
        sin = sin_ref[g0:g0 + CHUNK, :]
        qbs, kbs, vbs, states = [], [], [], []
        for h in range(RET_HEADS):
            qz = z_cur[rows, OFF_Q + h * HEAD_DIM:OFF_Q + (h + 1) * HEAD_DIM]
            kz = z_cur[rows, OFF_K + h * HEAD_DIM:OFF_K + (h + 1) * HEAD_DIM]
            vb = z_cur[rows, OFF_V + h * HEAD_DIM:OFF_V + (h + 1) * HEAD_DIM].astype(BF16)
            k = _rope(kz, cos, sin) * K_SCALE
            state = sstate_ref[0, 0, h]
            kd = (k * dec_ref[2 * RET_HEADS + h]).astype(BF16)
            sstate_ref[0, 0, h] = state * (GAMMA[h] ** CHUNK) + _dot_tn(kd, vb)
            qbs.append(_rope(qz, cos, sin).astype(BF16))
            kbs.append(k.astype(BF16))
            vbs.append(vb)
            states.append(state)
        scores = [_dot_nt(qbs[h], kbs[h]) for h in range(RET_HEADS)]
        cross = [_dot(qbs[h], states[h].astype(BF16)) for h in range(RET_HEADS)]
        saved[(s, c)] = (scores, cross, vbs)

    def outputs(s, c):
        z_cur = zbuf.at[s % 2]
        rows = slice(c * CHUNK, (c + 1) * CHUNK)
        scores, cross, vbs = saved.pop((s, c))
        for h in range(RET_HEADS):
            gz = z_cur[rows, OFF_G + h * HEAD_DIM:OFF_G + (h + 1) * HEAD_DIM]
            p = (scores[h] * dec_ref[h]).astype(BF16)
            o = _dot(p, vbs[h]) + cross[h] * dec_ref[RET_HEADS + h]
            on = o * lax.rsqrt(jnp.mean(o * o, axis=-1, keepdims=True) + EPS)
            ymix[s % 2, rows, CONV_CH + h * HEAD_DIM:CONV_CH + (h + 1) * HEAD_DIM] = (
                on * (gz * jax.nn.sigmoid(gz))).astype(BF16)

    stage_rows(0)
    for p in range(IN_TOTAL // PROJ_BLOCK):
        project(0, p)
    for s in range(n_sub):
        fillers = []
        if s > 0:
            fillers += [functools.partial(out_project, s - 1, p)
                        for p in range(D_MODEL // OUT_BLOCK)]
        if s + 1 < n_sub:
            stage_rows(s + 1)
            fillers += [functools.partial(project, s + 1, p)
                        for p in range(IN_TOTAL // PROJ_BLOCK)]
        steps = []
        for c in range(chunks_per_sub):
            steps.append(functools.partial(scores_and_state, s, c))
            if c > 0:
                steps.append(functools.partial(outputs, s, c - 1))
        steps.append(functools.partial(outputs, s, chunks_per_sub - 1))
        n_f, n_s = len(fillers), len(steps)
        placed = 0
        for i, step in enumerate(steps):
            while placed < n_f and placed * n_s <= i * n_f:
                fillers[placed]()
                placed += 1
            step()
            if i % 2 == 0:
                conv_chunk(s, i // 2)
        while placed < n_f:
            fillers[placed]()
            placed += 1
    for p in range(D_MODEL // OUT_BLOCK):
        out_project(n_sub - 1, p)

    tail = cbuf[8 + tm - 2:8 + tm, :]
    cbuf[6:8, :] = tail
    cstate_ref[0, 0] = tail


def _mixer_call(x, norm_mix, w_in, conv_w, w_out, cos, sin, dec, smeta, cmeta, w_up, w_down):
    bsz, seq, _ = x.shape
    tm = TOKEN_TILE
    n_t = seq // tm
    grid = (bsz, n_t)
    ff_slice = D_FF // (bsz * n_t)
    return pl.pallas_call(
        _mixer_kernel,
        grid=grid,
        in_specs=[
            pl.BlockSpec((1, tm, D_MODEL), lambda b, t: (b, t, 0)),
            _full((1, D_MODEL)),
            _resident((D_MODEL, IN_TOTAL)),
            _full((1, CONV_WIDTH, CONV_CH)),
            _resident((D_MODEL, D_MODEL)),
            pl.BlockSpec((tm, HEAD_DIM), lambda b, t: (t, 0)),
            pl.BlockSpec((tm, HEAD_DIM), lambda b, t: (t, 0)),
            _full((3 * RET_HEADS, CHUNK, CHUNK)),
            _full((RET_HEADS, HEAD_DIM, HEAD_DIM)),
            _full((CONV_WIDTH - 1, CONV_CH)),
            pl.BlockSpec((D_MODEL, ff_slice), lambda b, t: (0, b * n_t + t)),
            pl.BlockSpec((ff_slice, D_MODEL), lambda b, t: (b * n_t + t, 0)),
        ],
        out_specs=[
            pl.BlockSpec((1, tm, D_MODEL), lambda b, t: (b, t, 0)),
            pl.BlockSpec((1, 1, CONV_WIDTH - 1, CONV_CH), lambda b, t: (0, b, 0, 0)),
            pl.BlockSpec((1, 1, RET_HEADS, HEAD_DIM, HEAD_DIM), lambda b, t: (0, b, 0, 0, 0)),
            pl.BlockSpec((D_MODEL, ff_slice), lambda b, t: (0, b * n_t + t)),
            pl.BlockSpec((ff_slice, D_MODEL), lambda b, t: (b * n_t + t, 0)),
        ],
        out_shape=(
            jax.ShapeDtypeStruct((bsz, seq, D_MODEL), F32),
            jax.ShapeDtypeStruct((1, bsz, CONV_WIDTH - 1, CONV_CH), F32),
            jax.ShapeDtypeStruct((1, bsz, RET_HEADS, HEAD_DIM, HEAD_DIM), F32),
            jax.ShapeDtypeStruct((D_MODEL, D_FF), BF16),
            jax.ShapeDtypeStruct((D_FF, D_MODEL), BF16),
        ),
        scratch_shapes=[
            pltpu.VMEM((2, MIX_SUBTILE, IN_TOTAL), F32),
            pltpu.VMEM((tm + 8, CONV_CH), F32),
            pltpu.VMEM((2, MIX_SUBTILE, D_MODEL), BF16),
            pltpu.VMEM((2, MIX_SUBTILE, D_MODEL), BF16),
        ],
        compiler_params=pltpu.CompilerParams(
            dimension_semantics=("arbitrary", "arbitrary"), vmem_limit_bytes=VMEM_LIMIT_BYTES),
        name="prompt_mixer",
    )(x, norm_mix, w_in, conv_w, w_out, cos, sin, dec, smeta, cmeta, w_up, w_down)


def _mlp_body(h, nm, wup_ref, wdown_ref, nf):
    hn = _rmsnorm(h, nm).astype(BF16)
    acc = h
    ff_block = 1024
    for c in range(D_FF // ff_block):
        up = _dot(hn, wup_ref[:, c * ff_block:(c + 1) * ff_block])
        act = jnp.square(jnp.maximum(up, 0.0)).astype(BF16)
        acc = acc + _dot(act, wdown_ref[c * ff_block:(c + 1) * ff_block, :])
    return _rmsnorm(acc, nf)


def _prompt_mlp_tile(h_ref, nm_ref, wup_ref, wdown_ref, nf_ref, y_ref, hn_sc, acc_sc):
    n_sub = MLP_TILE // MLP_SUBTILE
    ff_block = 1024
    n_ff = D_FF // ff_block

    def rows(s):
        return slice(s * MLP_SUBTILE, (s + 1) * MLP_SUBTILE)

    hn_sc[0] = _rmsnorm(h_ref[rows(0), :], nm_ref[...]).astype(BF16)
    for s in range(n_sub):
        hn = hn_sc[s % 2]
        acc = h_ref[rows(s), :]
        for c in range(n_ff):
            up = _dot(hn, wup_ref[:, c * ff_block:(c + 1) * ff_block])
            act = jnp.square(jnp.maximum(up, 0.0)).astype(BF16)
            acc = acc + _dot(act, wdown_ref[c * ff_block:(c + 1) * ff_block, :])
            if c == 0 and s > 0:
                y_ref[rows(s - 1), :] = _rmsnorm(acc_sc[...], nf_ref[...])
            if c == 1 and s + 1 < n_sub:
                hn_sc[(s + 1) % 2] = _rmsnorm(h_ref[rows(s + 1), :], nm_ref[...]).astype(BF16)
        if s + 1 < n_sub:
            acc_sc[...] = acc
        else:
            y_ref[rows(s), :] = _rmsnorm(acc, nf_ref[...])


def _sample_state_sc(q, k, v, state):
    n = state.shape[0]
    mesh = plsc.VectorSubcoreMesh(core_axis_name="core", subcore_axis_name="subcore")
    n_workers = mesh.num_cores * mesh.num_subcores
    per_worker = (n * RET_HEADS) // n_workers
    assert per_worker * n_workers == n * RET_HEADS
    n_vec = HEAD_DIM // SC_LANES

    def body(q_hbm, k_hbm, v_hbm, st_hbm, cross_hbm, new_hbm, s_in, s_out, qv, kv, vv, cv):
        worker = lax.axis_index("core") * mesh.num_subcores + lax.axis_index("subcore")

        @pl.loop(0, per_worker)
        def _(u):
            unit = worker * per_worker + u
            b = unit // RET_HEADS
            h = unit % RET_HEADS
            pltpu.sync_copy(st_hbm.at[b, h], s_in)
            pltpu.sync_copy(q_hbm.at[h, b], qv)
            pltpu.sync_copy(k_hbm.at[h, b], kv)
            pltpu.sync_copy(v_hbm.at[h, b], vv)
            gamma = jnp.float32(GAMMA[RET_HEADS - 1])
            for hh in range(RET_HEADS - 1):
                gamma = jnp.where(h == hh, jnp.float32(GAMMA[hh]), gamma)
            v_vecs = [vv[pl.ds(j * SC_LANES, SC_LANES)] for j in range(n_vec)]

            for j in range(n_vec):
                cv[pl.ds(j * SC_LANES, SC_LANES)] = jnp.zeros((SC_LANES,), F32)

            @pl.loop(0, HEAD_DIM, step=SC_ROWS)
            def _(d0):
                part = [None] * n_vec
                for dd in range(SC_ROWS):
                    d = d0 + dd
                    idx = jnp.full((SC_LANES,), d, jnp.int32)
                    qd = plsc.load_gather(qv, [idx])
                    kd = plsc.load_gather(kv, [idx])
                    for j in range(n_vec):
                        lanes = pl.ds(j * SC_LANES, SC_LANES)
                        r = s_in[d, lanes]
                        s_out[d, lanes] = gamma * r + kd * v_vecs[j]
                        part[j] = qd * r if dd == 0 else part[j] + qd * r
                for j in range(n_vec):
                    plsc.addupdate(cv.at[pl.ds(j * SC_LANES, SC_LANES)], part[j])

            for j in range(n_vec):
                lanes = pl.ds(j * SC_LANES, SC_LANES)
                cv[lanes] = cv[lanes] * gamma
            pltpu.sync_copy(cv, cross_hbm.at[h, b])
            pltpu.sync_copy(s_out, new_hbm.at[b, h])

    return pl.kernel(
        body,
        out_type=(jax.ShapeDtypeStruct((RET_HEADS, n, HEAD_DIM), F32),
                  jax.ShapeDtypeStruct(state.shape, F32)),
        mesh=mesh,
        scratch_types=[pltpu.VMEM((HEAD_DIM, HEAD_DIM), F32),
                       pltpu.VMEM((HEAD_DIM, HEAD_DIM), F32),
                       pltpu.VMEM((HEAD_DIM,), F32),
                       pltpu.VMEM((HEAD_DIM,), F32),
                       pltpu.VMEM((HEAD_DIM,), F32),
                       pltpu.VMEM((HEAD_DIM,), F32)],
        compiler_params=pltpu.CompilerParams(needs_layout_passes=False),
        name="sample_state_sc",
    )(q, k, v, state)


def _sample_tail(x_ref, yconv_ref, inner_ref, gate_ref, cross_ref, wout_ref, nmlp_ref, wup_ref,
                 wdown_ref, nf_ref, ymix, y_ref):
    ymix[:, 0:CONV_CH] = yconv_ref[...].astype(BF16)
    for h in range(RET_HEADS):
        cols = slice(h * HEAD_DIM, (h + 1) * HEAD_DIM)
        o = inner_ref[:, cols] + cross_ref[h]
        on = o * lax.rsqrt(jnp.mean(o * o, axis=-1, keepdims=True) + EPS)
        ymix[:, CONV_CH + h * HEAD_DIM:CONV_CH + (h + 1) * HEAD_DIM] = (
            on * gate_ref[:, cols]).astype(BF16)
    h1 = x_ref[:, 0, :] + _dot(ymix[...], wout_ref[...])
    y_ref[:, 0, :] = _mlp_body(h1, nmlp_ref[...], wup_ref, wdown_ref, nf_ref[...])


def _mlp_kernel(h_ref, nm_ref, wup_ref, wdown_ref, nf_ref,
                xs_ref, yconv_ref, inner_ref, gate_ref, cross_ref, wout_ref,
                y_ref, ys_ref,
                hn_sc, acc_sc, ymix_sc):
    step = pl.program_id(0)
    n_tiles = pl.num_programs(0) - 1

    @pl.when(step < n_tiles)
    def _():
        _prompt_mlp_tile(h_ref, nm_ref, wup_ref, wdown_ref, nf_ref, y_ref, hn_sc, acc_sc)

    @pl.when(step == n_tiles)
    def _():
        _sample_tail(xs_ref, yconv_ref, inner_ref, gate_ref, cross_ref, wout_ref, nm_ref, wup_ref,
                     wdown_ref, nf_ref, ymix_sc, ys_ref)


def _mlp_call(h, norm_mlp, w_up, w_down, norm_final, xs, yconv, inner, gate, cross, w_out):
    rows = h.shape[0]
    n = xs.shape[0]
    tm = MLP_TILE
    n_tiles = rows // tm
    last = n_tiles - 1

    def tile(i):
        return jnp.minimum(i, last)

    return pl.pallas_call(
        _mlp_kernel,
        grid=(n_tiles + 1,),
        in_specs=[
            pl.BlockSpec((tm, D_MODEL), lambda i: (tile(i), 0)),
            _full((1, D_MODEL)),
            _resident((D_MODEL, D_FF)),
            _resident((D_FF, D_MODEL)),
            _full((1, D_MODEL)),
            _full((n, 1, D_MODEL)),
            _full((n, CONV_CH)),
            _full((n, RET_WIDTH)),
            _full((n, RET_WIDTH)),
            _full((RET_HEADS, n, HEAD_DIM)),
            _resident((D_MODEL, D_MODEL)),
        ],
        out_specs=[
            pl.BlockSpec((tm, D_MODEL), lambda i: (tile(i), 0)),
            _full((n, 1, D_MODEL)),
        ],
        out_shape=(jax.ShapeDtypeStruct((rows, D_MODEL), F32),
                   jax.ShapeDtypeStruct((n, 1, D_MODEL), F32)),
        scratch_shapes=[pltpu.VMEM((2, MLP_SUBTILE, D_MODEL), BF16),
                        pltpu.VMEM((MLP_SUBTILE, D_MODEL), F32),
                        pltpu.VMEM((n, D_MODEL), BF16)],
        compiler_params=pltpu.CompilerParams(
            dimension_semantics=("arbitrary",), vmem_limit_bytes=VMEM_LIMIT_BYTES),
        name="mlp_and_sample",
    )(h, norm_mlp, w_up, w_down, norm_final, xs, yconv, inner, gate, cross, w_out)


def kernel(x_prompt, x_sample, cache_conv, state_ret, meta_tokens, norm_mix, w_in, conv_w, w_out,
           norm_mlp, w_up, w_down, norm_final):
    bsz, seq, _ = x_prompt.shape
    half = jnp.arange(0, HEAD_DIM, 2, dtype=F32) / HEAD_DIM
    inv_half = 1.0 / (ROPE_BASE ** half)
    inv_freq = jnp.concatenate([inv_half, inv_half])[None, :]

    nmix = norm_mix[0][None, :]
    nmlp = norm_mlp[0][None, :]
    nfin = norm_final[None, :]

    (w_in_b, w_out_b, cos, sin, dec, smeta, cmeta,
     q, k, v, inner, gate, yconv, cnew) = _prologue_call(
        meta_tokens, x_sample, cache_conv[0], nmix, w_in[0], w_out[0], conv_w, inv_freq, seq)
    cross, ret_s = _sample_state_sc(q, k, v, state_ret[0])
    h1, conv_p, ret_p, w_up_b, w_down_b = _mixer_call(
        x_prompt, nmix, w_in_b, conv_w, w_out_b, cos, sin, dec, smeta, cmeta, w_up[0], w_down[0])
    y_prompt, y_sample = _mlp_call(
        h1.reshape(bsz * seq, D_MODEL), nmlp, w_up_b, w_down_b, nfin,
        x_sample, yconv, inner, gate, cross, w_out_b)
    y_prompt = y_prompt.reshape(bsz, seq, D_MODEL)

    return (y_prompt, y_sample, conv_p, ret_p, cnew[None], ret_s[None])
```

```python
import functools

import numpy as np
import jax
import jax.numpy as jnp
from jax import lax
from jax.experimental import pallas as pl
from jax.experimental.pallas import tpu as pltpu

D_MODEL = 1024
N_META = 16
CONV_CH = 512
CONV_WIDTH = 3
RET_HEADS = 4
HEAD_DIM = 128
RET_WIDTH = RET_HEADS * HEAD_DIM
CHUNK = 128
D_FF = 4 * D_MODEL
EPS = 1e-6
ROPE_BASE = 10000.0
PAST_LEN = 16384
K_SCALE = HEAD_DIM ** -0.5

OFF_U, OFF_C, OFF_B, OFF_Q, OFF_K, OFF_V, OFF_G = 0, 512, 1024, 1536, 2048, 2560, 3072
IN_TOTAL = 3584

GAMMA = tuple(1.0 - 2.0 ** (-5.0 - h) for h in range(RET_HEADS))
LOG_GAMMA = tuple(float(np.log(g)) for g in GAMMA)

TOKEN_TILE = 1024
MIX_SUBTILE = 512
PROJ_BLOCK = 512
OUT_BLOCK = 256
ROPE_PIECE = 256
MLP_TILE = 1024
MLP_SUBTILE = 512
SAMPLE_BLOCK = 8
VMEM_LIMIT_BYTES = 60 * 1024 * 1024

F32 = jnp.float32
BF16 = jnp.bfloat16


def _rmsnorm(x, gain):
    return x * lax.rsqrt(jnp.mean(x * x, axis=-1, keepdims=True) + EPS) * gain


def _dot(a, b):
    return jnp.dot(a, b, preferred_element_type=F32)


def _dot_nt(a, b):
    return lax.dot_general(a, b, (((1,), (1,)), ((), ())), preferred_element_type=F32)


def _dot_tn(a, b):
    return lax.dot_general(a, b, (((0,), (0,)), ((), ())), preferred_element_type=F32)


def _rope_tables(pos, inv_freq):
    ang = pos * inv_freq
    lane = lax.broadcasted_iota(jnp.int32, ang.shape, 1)
    sin = jnp.sin(ang)
    return jnp.cos(ang), jnp.where(lane < HEAD_DIM // 2, -sin, sin)


def _rope(x, cos, sin):
    return x * cos + pltpu.roll(x, HEAD_DIM // 2, 1) * sin


def _decay_tables(dec_ref):
    row = lax.broadcasted_iota(jnp.int32, (CHUNK, CHUNK), 0).astype(F32)
    col = lax.broadcasted_iota(jnp.int32, (CHUNK, CHUNK), 1).astype(F32)
    diff = row - col
    for h in range(RET_HEADS):
        lg = LOG_GAMMA[h]
        dec_ref[h] = jnp.where(diff >= 0, jnp.exp(lg * jnp.maximum(diff, 0.0)), 0.0)
        dec_ref[RET_HEADS + h] = jnp.exp((row + 1.0) * lg)
        dec_ref[2 * RET_HEADS + h] = jnp.exp((CHUNK - 1.0 - row) * lg)


def _full(shape):
    return pl.BlockSpec(shape, lambda *_: (0,) * len(shape))


def _resident(shape):
    return pl.BlockSpec(shape, lambda *_: (0,) * len(shape), pipeline_mode=pl.Buffered(1))


def _z_slice(zbuf, rows, off, width):
    blk, inner = divmod(off, PROJ_BLOCK)
    assert inner + width <= PROJ_BLOCK
    return zbuf[blk, rows, inner:inner + width]


def _meta_tail(zbuf, invf_ref, smeta_ref, cmeta_ref):
    rows = slice(0, CHUNK)
    cu = _z_slice(zbuf, rows, OFF_C, CONV_CH) * _z_slice(zbuf, rows, OFF_U, CONV_CH)
    cmeta_ref[...] = cu[N_META - 2:N_META, :]
    row = lax.broadcasted_iota(jnp.int32, (CHUNK, HEAD_DIM), 0).astype(F32)
    cos, sin = _rope_tables(row, invf_ref[...])
    for h in range(RET_HEADS):
        kz = _z_slice(zbuf, rows, OFF_K + h * HEAD_DIM, HEAD_DIM)
        vz = _z_slice(zbuf, rows, OFF_V + h * HEAD_DIM, HEAD_DIM)
        k = _rope(kz, cos, sin) * K_SCALE
        kdec = jnp.where(row < N_META, jnp.exp((N_META - 1.0 - row) * LOG_GAMMA[h]), 0.0)
        smeta_ref[h] = _dot_tn((k * kdec).astype(BF16), vz.astype(BF16))


def _sample_proj_tail(zbuf, rows, cache_ref, cw_ref, invf_ref,
                      q_ref, kt_ref, v_ref, inner_ref, gate_ref, yconv_ref, cnew_ref):
    prev0 = cache_ref[:, 0, :]
    prev1 = cache_ref[:, 1, :]
    cu = _z_slice(zbuf, rows, OFF_C, CONV_CH) * _z_slice(zbuf, rows, OFF_U, CONV_CH)
    conv = cw_ref[0, 0:1, :] * prev0 + cw_ref[0, 1:2, :] * prev1 + cw_ref[0, 2:3, :] * cu
    yconv_ref[...] = _z_slice(zbuf, rows, OFF_B, CONV_CH) * conv
    cnew_ref[:, 0, :] = prev1
    cnew_ref[:, 1, :] = cu
    pos = jnp.full((1, HEAD_DIM), float(PAST_LEN), F32)
    cos, sin = _rope_tables(pos, invf_ref[...])
    for h in range(RET_HEADS):
        cols = slice(h * HEAD_DIM, (h + 1) * HEAD_DIM)
        q = _rope(_z_slice(zbuf, rows, OFF_Q + h * HEAD_DIM, HEAD_DIM), cos, sin)
        k = _rope(_z_slice(zbuf, rows, OFF_K + h * HEAD_DIM, HEAD_DIM), cos, sin) * K_SCALE
        v = _z_slice(zbuf, rows, OFF_V + h * HEAD_DIM, HEAD_DIM)
        g = _z_slice(zbuf, rows, OFF_G + h * HEAD_DIM, HEAD_DIM)
        q_ref[:, cols] = q
        kt_ref[h] = k.T
        v_ref[:, cols] = v
        inner_ref[:, cols] = jnp.sum(q * k, axis=-1, keepdims=True) * v
        gate_ref[:, cols] = g * jax.nn.sigmoid(g)


def _prologue_kernel(meta_ref, xs_ref, cache_ref, nm_ref, win_ref, wout_ref, cw_ref, invf_ref,
                     winb_ref, woutb_ref, cos_ref, sin_ref, dec_ref, smeta_ref, cmeta_ref,
                     q_ref, kt_ref, v_ref, inner_ref, gate_ref, yconv_ref, cnew_ref,
                     xpad, hn_sc, zbuf, rope_sc):
    j = pl.program_id(0)
    n_steps = pl.num_programs(0)
    n_dec = xs_ref.shape[0]
    n_pieces = cos_ref.shape[0] // ROPE_PIECE

    def rope_piece(p):
        row0 = pl.multiple_of(p * ROPE_PIECE, ROPE_PIECE)
        base = jnp.asarray(N_META + p * ROPE_PIECE).astype(F32) * invf_ref[...]
        cb, sb = jnp.cos(base), jnp.sin(base)
        co, so = rope_sc[0], rope_sc[1]
        lane = lax.broadcasted_iota(jnp.int32, (ROPE_PIECE, HEAD_DIM), 1)
        sin = sb * co + cb * so
        cos_ref[pl.ds(row0, ROPE_PIECE), :] = cb * co - sb * so
        sin_ref[pl.ds(row0, ROPE_PIECE), :] = jnp.where(lane < HEAD_DIM // 2, -sin, sin)

    @pl.when(j == 0)
    def _():
        xpad[...] = jnp.zeros_like(xpad)
        xpad[0:N_META, :] = meta_ref[...]
        hn_sc[0:CHUNK, :] = _rmsnorm(xpad[...], nm_ref[...]).astype(BF16)
        hn_sc[CHUNK:CHUNK + n_dec, :] = _rmsnorm(xs_ref[:, 0, :], nm_ref[...]).astype(BF16)
        woutb_ref[...] = wout_ref[...].astype(BF16)
        _decay_tables(dec_ref)
        offs = lax.broadcasted_iota(jnp.int32, (ROPE_PIECE, HEAD_DIM), 0).astype(F32)
        ang = offs * invf_ref[...]
        rope_sc[0] = jnp.cos(ang)
        rope_sc[1] = jnp.sin(ang)

    w_blk = win_ref[...].astype(BF16)
    winb_ref[...] = w_blk
    zbuf[j] = _dot(hn_sc[...], w_blk)
    rope_piece(j)

    @pl.when(j == n_steps - 1)
    def _():
        for p in range(IN_TOTAL // PROJ_BLOCK, n_pieces):
            rope_piece(p)
        _meta_tail(zbuf, invf_ref, smeta_ref, cmeta_ref)
        _sample_proj_tail(zbuf, slice(CHUNK, CHUNK + n_dec), cache_ref, cw_ref, invf_ref,
                          q_ref, kt_ref, v_ref, inner_ref, gate_ref, yconv_ref, cnew_ref)


def _prologue_call(meta, xs, cache, norm_mix, w_in, w_out, conv_w, inv_freq, seq):
    n = xs.shape[0]
    n_blk = IN_TOTAL // PROJ_BLOCK
    assert seq % ROPE_PIECE == 0 and seq // ROPE_PIECE >= n_blk
    wide = jax.ShapeDtypeStruct((n, RET_WIDTH), F32)
    return pl.pallas_call(
        _prologue_kernel,
        grid=(n_blk,),
        in_specs=[
            _full((N_META, D_MODEL)),
            _full((n, 1, D_MODEL)),
            _full((n, CONV_WIDTH - 1, CONV_CH)),
            _full((1, D_MODEL)),
            pl.BlockSpec((D_MODEL, PROJ_BLOCK), lambda j: (0, j)),
            _full((D_MODEL, D_MODEL)),
            _full((1, CONV_WIDTH, CONV_CH)),
            _full((1, HEAD_DIM)),
        ],
        out_specs=[
            pl.BlockSpec((D_MODEL, PROJ_BLOCK), lambda j: (0, j)),
            _full((D_MODEL, D_MODEL)),
            _full((seq, HEAD_DIM)),
            _full((seq, HEAD_DIM)),
            _full((3 * RET_HEADS, CHUNK, CHUNK)),
            _full((RET_HEADS, HEAD_DIM, HEAD_DIM)),
            _full((CONV_WIDTH - 1, CONV_CH)),
            _full((n, RET_WIDTH)),
            _full((RET_HEADS, HEAD_DIM, n)),
            _full((n, RET_WIDTH)),
            _full((n, RET_WIDTH)),
            _full((n, RET_WIDTH)),
            _full((n, CONV_CH)),
            _full((n, CONV_WIDTH - 1, CONV_CH)),
        ],
        out_shape=(jax.ShapeDtypeStruct((D_MODEL, IN_TOTAL), BF16),
                   jax.ShapeDtypeStruct((D_MODEL, D_MODEL), BF16),
                   jax.ShapeDtypeStruct((seq, HEAD_DIM), F32),
                   jax.ShapeDtypeStruct((seq, HEAD_DIM), F32),
                   jax.ShapeDtypeStruct((3 * RET_HEADS, CHUNK, CHUNK), F32),
                   jax.ShapeDtypeStruct((RET_HEADS, HEAD_DIM, HEAD_DIM), F32),
                   jax.ShapeDtypeStruct((CONV_WIDTH - 1, CONV_CH), F32),
                   wide,
                   jax.ShapeDtypeStruct((RET_HEADS, HEAD_DIM, n), F32),
                   wide, wide, wide,
                   jax.ShapeDtypeStruct((n, CONV_CH), F32),
                   jax.ShapeDtypeStruct((n, CONV_WIDTH - 1, CONV_CH), F32)),
        scratch_shapes=[pltpu.VMEM((CHUNK, D_MODEL), F32),
                        pltpu.VMEM((CHUNK + n, D_MODEL), BF16),
                        pltpu.VMEM((n_blk, CHUNK + n, PROJ_BLOCK), F32),
                        pltpu.VMEM((2, ROPE_PIECE, HEAD_DIM), F32)],
        compiler_params=pltpu.CompilerParams(
            dimension_semantics=("arbitrary",), vmem_limit_bytes=VMEM_LIMIT_BYTES),
        name="prologue",
    )(meta, xs, cache, norm_mix, w_in, w_out, conv_w, inv_freq)


def _mixer_kernel(x_ref, nm_ref, win_ref, cw_ref, wout_ref, cos_ref, sin_ref, dec_ref,
                  smeta_ref, cmeta_ref, wup_ref, wdown_ref,
                  h1_ref, cstate_ref, sstate_ref, wup_bf_ref, wdown_bf_ref,
                  zbuf, cbuf, ymix, hn_sc):
    t = pl.program_id(1)
    tm = TOKEN_TILE

    wup_bf_ref[...] = wup_ref[...].astype(BF16)
    wdown_bf_ref[...] = wdown_ref[...].astype(BF16)

    @pl.when(t == 0)
    def _():
        sstate_ref[0, 0] = smeta_ref[...]
        cbuf[6:8, :] = cmeta_ref[...]

    n_sub = tm // MIX_SUBTILE
    chunks_per_sub = MIX_SUBTILE // CHUNK

    def sub_rows(s):
        return slice(s * MIX_SUBTILE, (s + 1) * MIX_SUBTILE)

    def stage_rows(s):
        hn_sc[s % 2] = _rmsnorm(x_ref[0, sub_rows(s), :], nm_ref[...]).astype(BF16)

    def project(s, p):
        cols = slice(p * PROJ_BLOCK, (p + 1) * PROJ_BLOCK)
        zbuf[s % 2, :, cols] = _dot(hn_sc[s % 2], win_ref[:, cols])

    def out_project(s, p):
        cols = slice(p * OUT_BLOCK, (p + 1) * OUT_BLOCK)
        h1_ref[0, sub_rows(s), cols] = (x_ref[0, sub_rows(s), cols]
                                        + _dot(ymix[s % 2], wout_ref[:, cols]))

    def conv_chunk(s, c):
        z = zbuf.at[s % 2]
        rows = slice(c * CHUNK, (c + 1) * CHUNK)
        g0 = (s * chunks_per_sub + c) * CHUNK
        cu = z[rows, OFF_C:OFF_C + CONV_CH] * z[rows, OFF_U:OFF_U + CONV_CH]
        cbuf[8 + g0:8 + g0 + CHUNK, :] = cu
        conv = (cw_ref[0, 0:1, :] * cbuf[6 + g0:6 + g0 + CHUNK, :]
                + cw_ref[0, 1:2, :] * cbuf[7 + g0:7 + g0 + CHUNK, :]
                + cw_ref[0, 2:3, :] * cu)
        ymix[s % 2, rows, 0:CONV_CH] = (z[rows, OFF_B:OFF_B + CONV_CH] * conv).astype(BF16)

    saved = {}

    def scores_and_state(s, c):
        z_cur = zbuf.at[s % 2]
        rows = slice(c * CHUNK, (c + 1) * CHUNK)
        g0 = (s * chunks_per_sub + c) * CHUNK
        cos = cos_ref[g0:g0 + CHUNK, :]
        sin = sin_ref[g0:g0 + CHUNK, :]
        qbs, kbs, vbs, states = [], [], [], []
        for h in range(RET_HEADS):
            qz = z_cur[rows, OFF_Q + h * HEAD_DIM:OFF_Q + (h + 1) * HEAD_DIM]
            kz = z_cur[rows, OFF_K + h * HEAD_DIM:OFF_K + (h + 1) * HEAD_DIM]
            vb = z_cur[rows, OFF_V + h * HEAD_DIM:OFF_V + (h + 1) * HEAD_DIM].astype(BF16)
            k = _rope(kz, cos, sin) * K_SCALE
            state = sstate_ref[0, 0, h]
            kd = (k * dec_ref[2 * RET_HEADS + h]).astype(BF16)
            sstate_ref[0, 0, h] = state * (GAMMA[h] ** CHUNK) + _dot_tn(kd, vb)
            qbs.append(_rope(qz, cos, sin).astype(BF16))
            kbs.append(k.astype(BF16))
            vbs.append(vb)
            states.append(state)
        scores = [_dot_nt(qbs[h], kbs[h]) for h in range(RET_HEADS)]
        cross = [_dot(qbs[h], states[h].astype(BF16)) for h in range(RET_HEADS)]
        saved[(s, c)] = (scores, cross, vbs)

    def outputs(s, c):
        z_cur = zbuf.at[s % 2]
        rows = slice(c * CHUNK, (c + 1) * CHUNK)
        scores, cross, vbs = saved.pop((s, c))
        for h in range(RET_HEADS):
            gz = z_cur[rows, OFF_G + h * HEAD_DIM:OFF_G + (h + 1) * HEAD_DIM]
            p = (scores[h] * dec_ref[h]).astype(BF16)
            o = _dot(p, vbs[h]) + cross[h] * dec_ref[RET_HEADS + h]
            on = o * lax.rsqrt(jnp.mean(o * o, axis=-1, keepdims=True) + EPS)
            ymix[s % 2, rows, CONV_CH + h * HEAD_DIM:CONV_CH + (h + 1) * HEAD_DIM] = (
                on * (gz * jax.nn.sigmoid(gz))).astype(BF16)

    stage_rows(0)
    for p in range(IN_TOTAL // PROJ_BLOCK):
        project(0, p)
    for s in range(n_sub):
        fillers = []
        if s > 0:
            fillers += [functools.partial(out_project, s - 1, p)
                        for p in range(D_MODEL // OUT_BLOCK)]
        if s + 1 < n_sub:
            stage_rows(s + 1)
            fillers += [functools.partial(project, s + 1, p)
                        for p in range(IN_TOTAL // PROJ_BLOCK)]
        steps = []
        for c in range(chunks_per_sub):
            steps.append(functools.partial(scores_and_state, s, c))
            if c > 0:
                steps.append(functools.partial(outputs, s, c - 1))
        steps.append(functools.partial(outputs, s, chunks_per_sub - 1))
        n_f, n_s = len(fillers), len(steps)
        placed = 0
        for i, step in enumerate(steps):
            while placed < n_f and placed * n_s <= i * n_f:
                fillers[placed]()
                placed += 1
            step()
            if i % 2 == 0:
                conv_chunk(s, i // 2)
        while placed < n_f:
            fillers[placed]()
            placed += 1
    for p in range(D_MODEL // OUT_BLOCK):
        out_project(n_sub - 1, p)

    tail = cbuf[8 + tm - 2:8 + tm, :]
    cbuf[6:8, :] = tail
    cstate_ref[0, 0] = tail


def _mixer_call(x, norm_mix, w_in, conv_w, w_out, cos, sin, dec, smeta, cmeta, w_up, w_down):
    bsz, seq, _ = x.shape
    tm = TOKEN_TILE
    n_t = seq // tm
    grid = (bsz, n_t)
    ff_slice = D_FF // (bsz * n_t)
    return pl.pallas_call(
        _mixer_kernel,
        grid=grid,
        in_specs=[
            pl.BlockSpec((1, tm, D_MODEL), lambda b, t: (b, t, 0)),
            _full((1, D_MODEL)),
            _resident((D_MODEL, IN_TOTAL)),
            _full((1, CONV_WIDTH, CONV_CH)),
            _resident((D_MODEL, D_MODEL)),
            pl.BlockSpec((tm, HEAD_DIM), lambda b, t: (t, 0)),
            pl.BlockSpec((tm, HEAD_DIM), lambda b, t: (t, 0)),
            _full((3 * RET_HEADS, CHUNK, CHUNK)),
            _full((RET_HEADS, HEAD_DIM, HEAD_DIM)),
            _full((CONV_WIDTH - 1, CONV_CH)),
            pl.BlockSpec((D_MODEL, ff_slice), lambda b, t: (0, b * n_t + t)),
            pl.BlockSpec((ff_slice, D_MODEL), lambda b, t: (b * n_t + t, 0)),
        ],
        out_specs=[
            pl.BlockSpec((1, tm, D_MODEL), lambda b, t: (b, t, 0)),
            pl.BlockSpec((1, 1, CONV_WIDTH - 1, CONV_CH), lambda b, t: (0, b, 0, 0)),
            pl.BlockSpec((1, 1, RET_HEADS, HEAD_DIM, HEAD_DIM), lambda b, t: (0, b, 0, 0, 0)),
            pl.BlockSpec((D_MODEL, ff_slice), lambda b, t: (0, b * n_t + t)),
            pl.BlockSpec((ff_slice, D_MODEL), lambda b, t: (b * n_t + t, 0)),
        ],
        out_shape=(
            jax.ShapeDtypeStruct((bsz, seq, D_MODEL), F32),
            jax.ShapeDtypeStruct((1, bsz, CONV_WIDTH - 1, CONV_CH), F32),
            jax.ShapeDtypeStruct((1, bsz, RET_HEADS, HEAD_DIM, HEAD_DIM), F32),
            jax.ShapeDtypeStruct((D_MODEL, D_FF), BF16),
            jax.ShapeDtypeStruct((D_FF, D_MODEL), BF16),
        ),
        scratch_shapes=[
            pltpu.VMEM((2, MIX_SUBTILE, IN_TOTAL), F32),
            pltpu.VMEM((tm + 8, CONV_CH), F32),
            pltpu.VMEM((2, MIX_SUBTILE, D_MODEL), BF16),
            pltpu.VMEM((2, MIX_SUBTILE, D_MODEL), BF16),
        ],
        compiler_params=pltpu.CompilerParams(
            dimension_semantics=("arbitrary", "arbitrary"), vmem_limit_bytes=VMEM_LIMIT_BYTES),
        name="prompt_mixer",
    )(x, norm_mix, w_in, conv_w, w_out, cos, sin, dec, smeta, cmeta, w_up, w_down)


def _mlp_body(h, nm, wup_ref, wdown_ref, nf):
    hn = _rmsnorm(h, nm).astype(BF16)
    acc = h
    ff_block = 1024
    for c in range(D_FF // ff_block):
        up = _dot(hn, wup_ref[:, c * ff_block:(c + 1) * ff_block])
        act = jnp.square(jnp.maximum(up, 0.0)).astype(BF16)
        acc = acc + _dot(act, wdown_ref[c * ff_block:(c + 1) * ff_block, :])
    return _rmsnorm(acc, nf)


def _prompt_mlp_tile(h_ref, nm_ref, wup_ref, wdown_ref, nf_ref, y_ref, hn_sc, acc_sc):
    n_sub = MLP_TILE // MLP_SUBTILE
    ff_block = 1024
    n_ff = D_FF // ff_block

    def rows(s):
        return slice(s * MLP_SUBTILE, (s + 1) * MLP_SUBTILE)

    hn_sc[0] = _rmsnorm(h_ref[rows(0), :], nm_ref[...]).astype(BF16)
    for s in range(n_sub):
        hn = hn_sc[s % 2]
        acc = h_ref[rows(s), :]
        for c in range(n_ff):
            up = _dot(hn, wup_ref[:, c * ff_block:(c + 1) * ff_block])
            act = jnp.square(jnp.maximum(up, 0.0)).astype(BF16)
            acc = acc + _dot(act, wdown_ref[c * ff_block:(c + 1) * ff_block, :])
            if c == 0 and s > 0:
                y_ref[rows(s - 1), :] = _rmsnorm(acc_sc[...], nf_ref[...])
            if c == 1 and s + 1 < n_sub:
                hn_sc[(s + 1) % 2] = _rmsnorm(h_ref[rows(s + 1), :], nm_ref[...]).astype(BF16)
        if s + 1 < n_sub:
            acc_sc[...] = acc
        else:
            y_ref[rows(s), :] = _rmsnorm(acc, nf_ref[...])


def _sample_state_block(step, q_ref, kt_ref, v_ref, st_ref, new_ref, cross_sc):
    assert SAMPLE_BLOCK == 8
    lane = lax.broadcasted_iota(jnp.int32, (HEAD_DIM, HEAD_DIM), 1)
    sub = lax.broadcasted_iota(jnp.int32, (SAMPLE_BLOCK, HEAD_DIM), 0)
    row0 = pl.multiple_of(step * SAMPLE_BLOCK, SAMPLE_BLOCK)
    for h in range(RET_HEADS):
        cols = slice(h * HEAD_DIM, (h + 1) * HEAD_DIM)
        vb = v_ref[:, cols].astype(BF16)
        cross = jnp.zeros((SAMPLE_BLOCK, HEAD_DIM), F32)
        for i in range(SAMPLE_BLOCK):
            b = step * SAMPLE_BLOCK + i
            state = st_ref[i, h]
            q8 = jnp.broadcast_to(q_ref[i:i + 1, cols], (SAMPLE_BLOCK, HEAD_DIM)).astype(BF16)
            cross = jnp.where(sub == i, _dot(q8, state.astype(BF16)), cross)
            kt_b = jnp.where(lane == b, kt_ref[h], 0.0).astype(BF16)
            new_ref[i, h] = state * GAMMA[h] + _dot(kt_b, vb)
        cross_sc[pl.ds(row0, SAMPLE_BLOCK), cols] = cross * GAMMA[h]


def _sample_tail(x_ref, yconv_ref, inner_ref, gate_ref, wout_ref, nmlp_ref, wup_ref, wdown_ref,
                 nf_ref, cross_sc, ymix, y_ref):
    ymix[:, 0:CONV_CH] = yconv_ref[...].astype(BF16)
    for h in range(RET_HEADS):
        cols = slice(h * HEAD_DIM, (h + 1) * HEAD_DIM)
        o = inner_ref[:, cols] + cross_sc[:, cols]
        on = o * lax.rsqrt(jnp.mean(o * o, axis=-1, keepdims=True) + EPS)
        ymix[:, CONV_CH + h * HEAD_DIM:CONV_CH + (h + 1) * HEAD_DIM] = (
            on * gate_ref[:, cols]).astype(BF16)
    h1 = x_ref[:, 0, :] + _dot(ymix[...], wout_ref[...])
    y_ref[:, 0, :] = _mlp_body(h1, nmlp_ref[...], wup_ref, wdown_ref, nf_ref[...])


def _mlp_kernel(h_ref, nm_ref, wup_ref, wdown_ref, nf_ref,
                q_ref, kt_ref, v_ref, st_ref, xs_ref, yconv_ref, inner_ref, gate_ref, wout_ref,
                y_ref, new_ref, ys_ref,
                hn_sc, acc_sc, cross_sc, ymix_sc):
    step = pl.program_id(0)
    n_tiles = pl.num_programs(0) - 1

    @pl.when(step < n_tiles)
    def _():
        _sample_state_block(step, q_ref, kt_ref, v_ref, st_ref, new_ref, cross_sc)
        _prompt_mlp_tile(h_ref, nm_ref, wup_ref, wdown_ref, nf_ref, y_ref, hn_sc, acc_sc)

    @pl.when(step == n_tiles)
    def _():
        _sample_tail(xs_ref, yconv_ref, inner_ref, gate_ref, wout_ref, nm_ref, wup_ref, wdown_ref,
                     nf_ref, cross_sc, ymix_sc, ys_ref)


def _mlp_call(h, norm_mlp, w_up, w_down, norm_final, q, kt, v, state, xs, yconv, inner, gate, w_out):
    rows = h.shape[0]
    n = xs.shape[0]
    tm = MLP_TILE
    n_tiles = rows // tm
    bb = SAMPLE_BLOCK
    assert n_tiles * bb == n
    last = n_tiles - 1

    def tile(i):
        return jnp.minimum(i, last)

    return pl.pallas_call(
        _mlp_kernel,
        grid=(n_tiles + 1,),
        in_specs=[
            pl.BlockSpec((tm, D_MODEL), lambda i: (tile(i), 0)),
            _full((1, D_MODEL)),
            _resident((D_MODEL, D_FF)),
            _resident((D_FF, D_MODEL)),
            _full((1, D_MODEL)),
            pl.BlockSpec((bb, RET_WIDTH), lambda i: (tile(i), 0)),
            _full((RET_HEADS, HEAD_DIM, n)),
            _full((n, RET_WIDTH)),
            pl.BlockSpec((bb, RET_HEADS, HEAD_DIM, HEAD_DIM), lambda i: (tile(i), 0, 0, 0)),
            _full((n, 1, D_MODEL)),
            _full((n, CONV_CH)),
            _full((n, RET_WIDTH)),
            _full((n, RET_WIDTH)),
            _resident((D_MODEL, D_MODEL)),
        ],
        out_specs=[
            pl.BlockSpec((tm, D_MODEL), lambda i: (tile(i), 0)),
            pl.BlockSpec((bb, RET_HEADS, HEAD_DIM, HEAD_DIM), lambda i: (tile(i), 0, 0, 0)),
            _full((n, 1, D_MODEL)),
        ],
        out_shape=(jax.ShapeDtypeStruct((rows, D_MODEL), F32),
                   jax.ShapeDtypeStruct(state.shape, F32),
                   jax.ShapeDtypeStruct((n, 1, D_MODEL), F32)),
        scratch_shapes=[pltpu.VMEM((2, MLP_SUBTILE, D_MODEL), BF16),
                        pltpu.VMEM((MLP_SUBTILE, D_MODEL), F32),
                        pltpu.VMEM((n, RET_WIDTH), F32),
                        pltpu.VMEM((n, D_MODEL), BF16)],
        compiler_params=pltpu.CompilerParams(
            dimension_semantics=("arbitrary",), vmem_limit_bytes=VMEM_LIMIT_BYTES),
        name="mlp_and_sample",
    )(h, norm_mlp, w_up, w_down, norm_final, q, kt, v, state, xs, yconv, inner, gate, w_out)


def kernel(x_prompt, x_sample, cache_conv, state_ret, meta_tokens, norm_mix, w_in, conv_w, w_out,
           norm_mlp, w_up, w_down, norm_final):
    bsz, seq, _ = x_prompt.shape
    half = jnp.arange(0, HEAD_DIM, 2, dtype=F32) / HEAD_DIM
    inv_half = 1.0 / (ROPE_BASE ** half)
    inv_freq = jnp.concatenate([inv_half, inv_half])[None, :]

    nmix = norm_mix[0][None, :]
    nmlp = norm_mlp[0][None, :]
    nfin = norm_final[None, :]

    (w_in_b, w_out_b, cos, sin, dec, smeta, cmeta,
     q, kt, v, inner, gate, yconv, cnew) = _prologue_call(
        meta_tokens, x_sample, cache_conv[0], nmix, w_in[0], w_out[0], conv_w, inv_freq, seq)
    h1, conv_p, ret_p, w_up_b, w_down_b = _mixer_call(
        x_prompt, nmix, w_in_b, conv_w, w_out_b, cos, sin, dec, smeta, cmeta, w_up[0], w_down[0])
    y_prompt, ret_s, y_sample = _mlp_call(
        h1.reshape(bsz * seq, D_MODEL), nmlp, w_up_b, w_down_b, nfin,
        q, kt, v, state_ret[0], x_sample, yconv, inner, gate, w_out_b)
    y_prompt = y_prompt.reshape(bsz, seq, D_MODEL)

    return (y_prompt, y_sample, conv_p, ret_p, cnew[None], ret_s[None])
```

```python
import functools

import numpy as np
import jax
import jax.numpy as jnp
from jax import lax
from jax.experimental import pallas as pl
from jax.experimental.pallas import tpu as pltpu

D_MODEL = 1024
N_META = 16
CONV_CH = 512
CONV_WIDTH = 3
RET_HEADS = 4
HEAD_DIM = 128
RET_WIDTH = RET_HEADS * HEAD_DIM
CHUNK = 128
D_FF = 4 * D_MODEL
EPS = 1e-6
ROPE_BASE = 10000.0
PAST_LEN = 16384
K_SCALE = HEAD_DIM ** -0.5

OFF_U, OFF_C, OFF_B, OFF_Q, OFF_K, OFF_V, OFF_G = 0, 512, 1024, 1536, 2048, 2560, 3072
IN_TOTAL = 3584

GAMMA = tuple(1.0 - 2.0 ** (-5.0 - h) for h in range(RET_HEADS))
LOG_GAMMA = tuple(float(np.log(g)) for g in GAMMA)

TOKEN_TILE = 1024
MIX_SUBTILE = 512
PROJ_BLOCK = 512
OUT_BLOCK = 256
ROPE_PIECE = 256
MLP_TILE = 1024
MLP_SUBTILE = 512
SAMPLE_BLOCK = 8
VMEM_LIMIT_BYTES = 60 * 1024 * 1024

F32 = jnp.float32
BF16 = jnp.bfloat16


def _rmsnorm(x, gain):
    return x * lax.rsqrt(jnp.mean(x * x, axis=-1, keepdims=True) + EPS) * gain


def _dot(a, b):
    return jnp.dot(a, b, preferred_element_type=F32)


def _dot_nt(a, b):
    return lax.dot_general(a, b, (((1,), (1,)), ((), ())), preferred_element_type=F32)


def _dot_tn(a, b):
    return lax.dot_general(a, b, (((0,), (0,)), ((), ())), preferred_element_type=F32)


def _rope_tables(pos, inv_freq):
    ang = pos * inv_freq
    lane = lax.broadcasted_iota(jnp.int32, ang.shape, 1)
    sin = jnp.sin(ang)
    return jnp.cos(ang), jnp.where(lane < HEAD_DIM // 2, -sin, sin)


def _rope(x, cos, sin):
    return x * cos + pltpu.roll(x, HEAD_DIM // 2, 1) * sin


def _decay_tables(dec_ref):
    row = lax.broadcasted_iota(jnp.int32, (CHUNK, CHUNK), 0).astype(F32)
    col = lax.broadcasted_iota(jnp.int32, (CHUNK, CHUNK), 1).astype(F32)
    diff = row - col
    for h in range(RET_HEADS):
        lg = LOG_GAMMA[h]
        dec_ref[h] = jnp.where(diff >= 0, jnp.exp(lg * jnp.maximum(diff, 0.0)), 0.0)
        dec_ref[RET_HEADS + h] = jnp.exp((row + 1.0) * lg)
        dec_ref[2 * RET_HEADS + h] = jnp.exp((CHUNK - 1.0 - row) * lg)


def _full(shape):
    return pl.BlockSpec(shape, lambda *_: (0,) * len(shape))


def _resident(shape):
    return pl.BlockSpec(shape, lambda *_: (0,) * len(shape), pipeline_mode=pl.Buffered(1))


def _z_slice(zbuf, rows, off, width):
    blk, inner = divmod(off, PROJ_BLOCK)
    assert inner + width <= PROJ_BLOCK
    return zbuf[blk, rows, inner:inner + width]


def _meta_tail(zbuf, invf_ref, smeta_ref, cmeta_ref):
    rows = slice(0, CHUNK)
    cu = _z_slice(zbuf, rows, OFF_C, CONV_CH) * _z_slice(zbuf, rows, OFF_U, CONV_CH)
    cmeta_ref[...] = cu[N_META - 2:N_META, :]
    row = lax.broadcasted_iota(jnp.int32, (CHUNK, HEAD_DIM), 0).astype(F32)
    cos, sin = _rope_tables(row, invf_ref[...])
    for h in range(RET_HEADS):
        kz = _z_slice(zbuf, rows, OFF_K + h * HEAD_DIM, HEAD_DIM)
        vz = _z_slice(zbuf, rows, OFF_V + h * HEAD_DIM, HEAD_DIM)
        k = _rope(kz, cos, sin) * K_SCALE
        kdec = jnp.where(row < N_META, jnp.exp((N_META - 1.0 - row) * LOG_GAMMA[h]), 0.0)
        smeta_ref[h] = _dot_tn((k * kdec).astype(BF16), vz.astype(BF16))


def _sample_proj_tail(zbuf, rows, cache_ref, cw_ref, invf_ref,
                      q_ref, kt_ref, v_ref, inner_ref, gate_ref, yconv_ref, cnew_ref):
    prev0 = cache_ref[:, 0, :]
    prev1 = cache_ref[:, 1, :]
    cu = _z_slice(zbuf, rows, OFF_C, CONV_CH) * _z_slice(zbuf, rows, OFF_U, CONV_CH)
    conv = cw_ref[0, 0:1, :] * prev0 + cw_ref[0, 1:2, :] * prev1 + cw_ref[0, 2:3, :] * cu
    yconv_ref[...] = _z_slice(zbuf, rows, OFF_B, CONV_CH) * conv
    cnew_ref[:, 0, :] = prev1
    cnew_ref[:, 1, :] = cu
    pos = jnp.full((1, HEAD_DIM), float(PAST_LEN), F32)
    cos, sin = _rope_tables(pos, invf_ref[...])
    for h in range(RET_HEADS):
        cols = slice(h * HEAD_DIM, (h + 1) * HEAD_DIM)
        q = _rope(_z_slice(zbuf, rows, OFF_Q + h * HEAD_DIM, HEAD_DIM), cos, sin)
        k = _rope(_z_slice(zbuf, rows, OFF_K + h * HEAD_DIM, HEAD_DIM), cos, sin) * K_SCALE
        v = _z_slice(zbuf, rows, OFF_V + h * HEAD_DIM, HEAD_DIM)
        g = _z_slice(zbuf, rows, OFF_G + h * HEAD_DIM, HEAD_DIM)
        q_ref[:, cols] = q
        kt_ref[h] = k.T
        v_ref[:, cols] = v
        inner_ref[:, cols] = jnp.sum(q * k, axis=-1, keepdims=True) * v
        gate_ref[:, cols] = g * jax.nn.sigmoid(g)


def _prologue_kernel(meta_ref, xs_ref, cache_ref, nm_ref, win_ref, wout_ref, cw_ref, invf_ref,
                     winb_ref, woutb_ref, cos_ref, sin_ref, dec_ref, smeta_ref, cmeta_ref,
                     q_ref, kt_ref, v_ref, inner_ref, gate_ref, yconv_ref, cnew_ref,
                     xpad, hn_sc, zbuf, rope_sc):
    j = pl.program_id(0)
    n_steps = pl.num_programs(0)
    n_dec = xs_ref.shape[0]
    n_pieces = cos_ref.shape[0] // ROPE_PIECE

    def rope_piece(p):
        row0 = pl.multiple_of(p * ROPE_PIECE, ROPE_PIECE)
        base = jnp.asarray(N_META + p * ROPE_PIECE).astype(F32) * invf_ref[...]
        cb, sb = jnp.cos(base), jnp.sin(base)
        co, so = rope_sc[0], rope_sc[1]
        lane = lax.broadcasted_iota(jnp.int32, (ROPE_PIECE, HEAD_DIM), 1)
        sin = sb * co + cb * so
        cos_ref[pl.ds(row0, ROPE_PIECE), :] = cb * co - sb * so
        sin_ref[pl.ds(row0, ROPE_PIECE), :] = jnp.where(lane < HEAD_DIM // 2, -sin, sin)

    @pl.when(j == 0)
    def _():
        xpad[...] = jnp.zeros_like(xpad)
        xpad[0:N_META, :] = meta_ref[...]
        hn_sc[0:CHUNK, :] = _rmsnorm(xpad[...], nm_ref[...]).astype(BF16)
        hn_sc[CHUNK:CHUNK + n_dec, :] = _rmsnorm(xs_ref[:, 0, :], nm_ref[...]).astype(BF16)
        woutb_ref[...] = wout_ref[...].astype(BF16)
        _decay_tables(dec_ref)
        offs = lax.broadcasted_iota(jnp.int32, (ROPE_PIECE, HEAD_DIM), 0).astype(F32)
        ang = offs * invf_ref[...]
        rope_sc[0] = jnp.cos(ang)
        rope_sc[1] = jnp.sin(ang)

    w_blk = win_ref[...].astype(BF16)
    winb_ref[...] = w_blk
    zbuf[j] = _dot(hn_sc[...], w_blk)
    rope_piece(j)

    @pl.when(j == n_steps - 1)
    def _():
        for p in range(IN_TOTAL // PROJ_BLOCK, n_pieces):
            rope_piece(p)
        _meta_tail(zbuf, invf_ref, smeta_ref, cmeta_ref)
        _sample_proj_tail(zbuf, slice(CHUNK, CHUNK + n_dec), cache_ref, cw_ref, invf_ref,
                          q_ref, kt_ref, v_ref, inner_ref, gate_ref, yconv_ref, cnew_ref)


def _prologue_call(meta, xs, cache, norm_mix, w_in, w_out, conv_w, inv_freq, seq):
    n = xs.shape[0]
    n_blk = IN_TOTAL // PROJ_BLOCK
    assert seq % ROPE_PIECE == 0 and seq // ROPE_PIECE >= n_blk
    wide = jax.ShapeDtypeStruct((n, RET_WIDTH), F32)
    return pl.pallas_call(
        _prologue_kernel,
        grid=(n_blk,),
        in_specs=[
            _full((N_META, D_MODEL)),
            _full((n, 1, D_MODEL)),
            _full((n, CONV_WIDTH - 1, CONV_CH)),
            _full((1, D_MODEL)),
            pl.BlockSpec((D_MODEL, PROJ_BLOCK), lambda j: (0, j)),
            _full((D_MODEL, D_MODEL)),
            _full((1, CONV_WIDTH, CONV_CH)),
            _full((1, HEAD_DIM)),
        ],
        out_specs=[
            pl.BlockSpec((D_MODEL, PROJ_BLOCK), lambda j: (0, j)),
            _full((D_MODEL, D_MODEL)),
            _full((seq, HEAD_DIM)),
            _full((seq, HEAD_DIM)),
            _full((3 * RET_HEADS, CHUNK, CHUNK)),
            _full((RET_HEADS, HEAD_DIM, HEAD_DIM)),
            _full((CONV_WIDTH - 1, CONV_CH)),
            _full((n, RET_WIDTH)),
            _full((RET_HEADS, HEAD_DIM, n)),
            _full((n, RET_WIDTH)),
            _full((n, RET_WIDTH)),
            _full((n, RET_WIDTH)),
            _full((n, CONV_CH)),
            _full((n, CONV_WIDTH - 1, CONV_CH)),
        ],
        out_shape=(jax.ShapeDtypeStruct((D_MODEL, IN_TOTAL), BF16),
                   jax.ShapeDtypeStruct((D_MODEL, D_MODEL), BF16),
                   jax.ShapeDtypeStruct((seq, HEAD_DIM), F32),
                   jax.ShapeDtypeStruct((seq, HEAD_DIM), F32),
                   jax.ShapeDtypeStruct((3 * RET_HEADS, CHUNK, CHUNK), F32),
                   jax.ShapeDtypeStruct((RET_HEADS, HEAD_DIM, HEAD_DIM), F32),
                   jax.ShapeDtypeStruct((CONV_WIDTH - 1, CONV_CH), F32),
                   wide,
                   jax.ShapeDtypeStruct((RET_HEADS, HEAD_DIM, n), F32),
                   wide, wide, wide,
                   jax.ShapeDtypeStruct((n, CONV_CH), F32),
                   jax.ShapeDtypeStruct((n, CONV_WIDTH - 1, CONV_CH), F32)),
        scratch_shapes=[pltpu.VMEM((CHUNK, D_MODEL), F32),
                        pltpu.VMEM((CHUNK + n, D_MODEL), BF16),
                        pltpu.VMEM((n_blk, CHUNK + n, PROJ_BLOCK), F32),
                        pltpu.VMEM((2, ROPE_PIECE, HEAD_DIM), F32)],
        compiler_params=pltpu.CompilerParams(
            dimension_semantics=("arbitrary",), vmem_limit_bytes=VMEM_LIMIT_BYTES),
        name="prologue",
    )(meta, xs, cache, norm_mix, w_in, w_out, conv_w, inv_freq)


def _mixer_kernel(x_ref, nm_ref, win_ref, cw_ref, wout_ref, cos_ref, sin_ref, dec_ref,
                  smeta_ref, cmeta_ref, wup_ref, wdown_ref,
                  h1_ref, cstate_ref, sstate_ref, wup_bf_ref, wdown_bf_ref,
                  zbuf, cbuf, ymix, hn_sc):
    t = pl.program_id(1)
    tm = TOKEN_TILE

    wup_bf_ref[...] = wup_ref[...].astype(BF16)
    wdown_bf_ref[...] = wdown_ref[...].astype(BF16)

    @pl.when(t == 0)
    def _():
        sstate_ref[0, 0] = smeta_ref[...]
        cbuf[6:8, :] = cmeta_ref[...]

    n_sub = tm // MIX_SUBTILE
    chunks_per_sub = MIX_SUBTILE // CHUNK

    def sub_rows(s):
        return slice(s * MIX_SUBTILE, (s + 1) * MIX_SUBTILE)

    def stage_rows(s):
        hn_sc[s % 2] = _rmsnorm(x_ref[0, sub_rows(s), :], nm_ref[...]).astype(BF16)

    def project(s, p):
        cols = slice(p * PROJ_BLOCK, (p + 1) * PROJ_BLOCK)
        zbuf[s % 2, :, cols] = _dot(hn_sc[s % 2], win_ref[:, cols])

    def out_project(s, p):
        cols = slice(p * OUT_BLOCK, (p + 1) * OUT_BLOCK)
        h1_ref[0, sub_rows(s), cols] = (x_ref[0, sub_rows(s), cols]
                                        + _dot(ymix[s % 2], wout_ref[:, cols]))

    def conv_chunk(s, c):
        z = zbuf.at[s % 2]
        rows = slice(c * CHUNK, (c + 1) * CHUNK)
        g0 = (s * chunks_per_sub + c) * CHUNK
        cu = z[rows, OFF_C:OFF_C + CONV_CH] * z[rows, OFF_U:OFF_U + CONV_CH]
        cbuf[8 + g0:8 + g0 + CHUNK, :] = cu
        conv = (cw_ref[0, 0:1, :] * cbuf[6 + g0:6 + g0 + CHUNK, :]
                + cw_ref[0, 1:2, :] * cbuf[7 + g0:7 + g0 + CHUNK, :]
                + cw_ref[0, 2:3, :] * cu)
        ymix[s % 2, rows, 0:CONV_CH] = (z[rows, OFF_B:OFF_B + CONV_CH] * conv).astype(BF16)

    saved = {}

    def scores_and_state(s, c):
        z_cur = zbuf.at[s % 2]
        rows = slice(c * CHUNK, (c + 1) * CHUNK)
        g0 = (s * chunks_per_sub + c) * CHUNK
        cos = cos_ref[g0:g0 + CHUNK, :]
        sin = sin_ref[g0:g0 + CHUNK, :]
        qbs, kbs, vbs, states = [], [], [], []
        for h in range(RET_HEADS):
            qz = z_cur[rows, OFF_Q + h * HEAD_DIM:OFF_Q + (h + 1) * HEAD_DIM]
            kz = z_cur[rows, OFF_K + h * HEAD_DIM:OFF_K + (h + 1) * HEAD_DIM]
            vb = z_cur[rows, OFF_V + h * HEAD_DIM:OFF_V + (h + 1) * HEAD_DIM].astype(BF16)
            k = _rope(kz, cos, sin) * K_SCALE
            state = sstate_ref[0, 0, h]
            kd = (k * dec_ref[2 * RET_HEADS + h]).astype(BF16)
            sstate_ref[0, 0, h] = state * (GAMMA[h] ** CHUNK) + _dot_tn(kd, vb)
            qbs.append(_rope(qz, cos, sin).astype(BF16))
            kbs.append(jnp.concatenate([k.T.astype(BF16), state.astype(BF16)], axis=1))
            vbs.append(vb)
        both = [_dot(qbs[h], kbs[h]) for h in range(RET_HEADS)]
        scores = [b[:, 0:CHUNK] for b in both]
        cross = [b[:, CHUNK:CHUNK + HEAD_DIM] for b in both]
        saved[(s, c)] = (scores, cross, vbs)

    def outputs(s, c):
        z_cur = zbuf.at[s % 2]
        rows = slice(c * CHUNK, (c + 1) * CHUNK)
        scores, cross, vbs = saved.pop((s, c))
        for h in range(RET_HEADS):
            gz = z_cur[rows, OFF_G + h * HEAD_DIM:OFF_G + (h + 1) * HEAD_DIM]
            p = (scores[h] * dec_ref[h]).astype(BF16)
            o = _dot(p, vbs[h]) + cross[h] * dec_ref[RET_HEADS + h]
            on = o * lax.rsqrt(jnp.mean(o * o, axis=-1, keepdims=True) + EPS)
            ymix[s % 2, rows, CONV_CH + h * HEAD_DIM:CONV_CH + (h + 1) * HEAD_DIM] = (
                on * (gz * jax.nn.sigmoid(gz))).astype(BF16)

    stage_rows(0)
    for p in range(IN_TOTAL // PROJ_BLOCK):
        project(0, p)
    for s in range(n_sub):
        fillers = []
        if s > 0:
            fillers += [functools.partial(out_project, s - 1, p)
                        for p in range(D_MODEL // OUT_BLOCK)]
        if s + 1 < n_sub:
            stage_rows(s + 1)
            fillers += [functools.partial(project, s + 1, p)
                        for p in range(IN_TOTAL // PROJ_BLOCK)]
        steps = []
        for c in range(chunks_per_sub):
            steps.append(functools.partial(scores_and_state, s, c))
            if c > 0:
                steps.append(functools.partial(outputs, s, c - 1))
        steps.append(functools.partial(outputs, s, chunks_per_sub - 1))
        n_f, n_s = len(fillers), len(steps)
        placed = 0
        for i, step in enumerate(steps):
            while placed < n_f and placed * n_s <= i * n_f:
                fillers[placed]()
                placed += 1
            step()
            if i % 2 == 0:
                conv_chunk(s, i // 2)
        while placed < n_f:
            fillers[placed]()
            placed += 1
    for p in range(D_MODEL // OUT_BLOCK):
        out_project(n_sub - 1, p)

    tail = cbuf[8 + tm - 2:8 + tm, :]
    cbuf[6:8, :] = tail
    cstate_ref[0, 0] = tail


def _mixer_call(x, norm_mix, w_in, conv_w, w_out, cos, sin, dec, smeta, cmeta, w_up, w_down):
    bsz, seq, _ = x.shape
    tm = TOKEN_TILE
    n_t = seq // tm
    grid = (bsz, n_t)
    ff_slice = D_FF // (bsz * n_t)
    return pl.pallas_call(
        _mixer_kernel,
        grid=grid,
        in_specs=[
            pl.BlockSpec((1, tm, D_MODEL), lambda b, t: (b, t, 0)),
            _full((1, D_MODEL)),
            _resident((D_MODEL, IN_TOTAL)),
            _full((1, CONV_WIDTH, CONV_CH)),
            _resident((D_MODEL, D_MODEL)),
            pl.BlockSpec((tm, HEAD_DIM), lambda b, t: (t, 0)),
            pl.BlockSpec((tm, HEAD_DIM), lambda b, t: (t, 0)),
            _full((3 * RET_HEADS, CHUNK, CHUNK)),
            _full((RET_HEADS, HEAD_DIM, HEAD_DIM)),
            _full((CONV_WIDTH - 1, CONV_CH)),
            pl.BlockSpec((D_MODEL, ff_slice), lambda b, t: (0, b * n_t + t)),
            pl.BlockSpec((ff_slice, D_MODEL), lambda b, t: (b * n_t + t, 0)),
        ],
        out_specs=[
            pl.BlockSpec((1, tm, D_MODEL), lambda b, t: (b, t, 0)),
            pl.BlockSpec((1, 1, CONV_WIDTH - 1, CONV_CH), lambda b, t: (0, b, 0, 0)),
            pl.BlockSpec((1, 1, RET_HEADS, HEAD_DIM, HEAD_DIM), lambda b, t: (0, b, 0, 0, 0)),
            pl.BlockSpec((D_MODEL, ff_slice), lambda b, t: (0, b * n_t + t)),
            pl.BlockSpec((ff_slice, D_MODEL), lambda b, t: (b * n_t + t, 0)),
        ],
        out_shape=(
            jax.ShapeDtypeStruct((bsz, seq, D_MODEL), F32),
            jax.ShapeDtypeStruct((1, bsz, CONV_WIDTH - 1, CONV_CH), F32),
            jax.ShapeDtypeStruct((1, bsz, RET_HEADS, HEAD_DIM, HEAD_DIM), F32),
            jax.ShapeDtypeStruct((D_MODEL, D_FF), BF16),
            jax.ShapeDtypeStruct((D_FF, D_MODEL), BF16),
        ),
        scratch_shapes=[
            pltpu.VMEM((2, MIX_SUBTILE, IN_TOTAL), F32),
            pltpu.VMEM((tm + 8, CONV_CH), F32),
            pltpu.VMEM((2, MIX_SUBTILE, D_MODEL), BF16),
            pltpu.VMEM((2, MIX_SUBTILE, D_MODEL), BF16),
        ],
        compiler_params=pltpu.CompilerParams(
            dimension_semantics=("arbitrary", "arbitrary"), vmem_limit_bytes=VMEM_LIMIT_BYTES),
        name="prompt_mixer",
    )(x, norm_mix, w_in, conv_w, w_out, cos, sin, dec, smeta, cmeta, w_up, w_down)


def _mlp_body(h, nm, wup_ref, wdown_ref, nf):
    hn = _rmsnorm(h, nm).astype(BF16)
    acc = h
    ff_block = 1024
    for c in range(D_FF // ff_block):
        up = _dot(hn, wup_ref[:, c * ff_block:(c + 1) * ff_block])
        act = jnp.square(jnp.maximum(up, 0.0)).astype(BF16)
        acc = acc + _dot(act, wdown_ref[c * ff_block:(c + 1) * ff_block, :])
    return _rmsnorm(acc, nf)


def _prompt_mlp_tile(h_ref, nm_ref, wup_ref, wdown_ref, nf_ref, y_ref, hn_sc, acc_sc):
    n_sub = MLP_TILE // MLP_SUBTILE
    ff_block = 1024
    n_ff = D_FF // ff_block

    def rows(s):
        return slice(s * MLP_SUBTILE, (s + 1) * MLP_SUBTILE)

    hn_sc[0] = _rmsnorm(h_ref[rows(0), :], nm_ref[...]).astype(BF16)
    for s in range(n_sub):
        hn = hn_sc[s % 2]
        acc = h_ref[rows(s), :]
        for c in range(n_ff):
            up = _dot(hn, wup_ref[:, c * ff_block:(c + 1) * ff_block])
            act = jnp.square(jnp.maximum(up, 0.0)).astype(BF16)
            acc = acc + _dot(act, wdown_ref[c * ff_block:(c + 1) * ff_block, :])
            if c == 0 and s > 0:
                y_ref[rows(s - 1), :] = _rmsnorm(acc_sc[...], nf_ref[...])
            if c == 1 and s + 1 < n_sub:
                hn_sc[(s + 1) % 2] = _rmsnorm(h_ref[rows(s + 1), :], nm_ref[...]).astype(BF16)
        if s + 1 < n_sub:
            acc_sc[...] = acc
        else:
            y_ref[rows(s), :] = _rmsnorm(acc, nf_ref[...])


def _sample_state_block(step, q_ref, kt_ref, v_ref, st_ref, new_ref, cross_sc):
    assert SAMPLE_BLOCK == 8
    lane = lax.broadcasted_iota(jnp.int32, (HEAD_DIM, HEAD_DIM), 1)
    sub = lax.broadcasted_iota(jnp.int32, (SAMPLE_BLOCK, HEAD_DIM), 0)
    row0 = pl.multiple_of(step * SAMPLE_BLOCK, SAMPLE_BLOCK)
    for h in range(RET_HEADS):
        cols = slice(h * HEAD_DIM, (h + 1) * HEAD_DIM)
        vb = v_ref[:, cols].astype(BF16)
        cross = jnp.zeros((SAMPLE_BLOCK, HEAD_DIM), F32)
        for i in range(SAMPLE_BLOCK):
            b = step * SAMPLE_BLOCK + i
            state = st_ref[i, h]
            q8 = jnp.broadcast_to(q_ref[i:i + 1, cols], (SAMPLE_BLOCK, HEAD_DIM)).astype(BF16)
            cross = jnp.where(sub == i, _dot(q8, state.astype(BF16)), cross)
            kt_b = jnp.where(lane == b, kt_ref[h], 0.0).astype(BF16)
            new_ref[i, h] = state * GAMMA[h] + _dot(kt_b, vb)
        cross_sc[pl.ds(row0, SAMPLE_BLOCK), cols] = cross * GAMMA[h]


def _sample_tail(x_ref, yconv_ref, inner_ref, gate_ref, wout_ref, nmlp_ref, wup_ref, wdown_ref,
                 nf_ref, cross_sc, ymix, y_ref):
    ymix[:, 0:CONV_CH] = yconv_ref[...].astype(BF16)
    for h in range(RET_HEADS):
        cols = slice(h * HEAD_DIM, (h + 1) * HEAD_DIM)
        o = inner_ref[:, cols] + cross_sc[:, cols]
        on = o * lax.rsqrt(jnp.mean(o * o, axis=-1, keepdims=True) + EPS)
        ymix[:, CONV_CH + h * HEAD_DIM:CONV_CH + (h + 1) * HEAD_DIM] = (
            on * gate_ref[:, cols]).astype(BF16)
    h1 = x_ref[:, 0, :] + _dot(ymix[...], wout_ref[...])
    y_ref[:, 0, :] = _mlp_body(h1, nmlp_ref[...], wup_ref, wdown_ref, nf_ref[...])


def _mlp_kernel(h_ref, nm_ref, wup_ref, wdown_ref, nf_ref,
                q_ref, kt_ref, v_ref, st_ref, xs_ref, yconv_ref, inner_ref, gate_ref, wout_ref,
                y_ref, new_ref, ys_ref,
                hn_sc, acc_sc, cross_sc, ymix_sc):
    step = pl.program_id(0)
    n_tiles = pl.num_programs(0) - 1

    @pl.when(step < n_tiles)
    def _():
        _sample_state_block(step, q_ref, kt_ref, v_ref, st_ref, new_ref, cross_sc)
        _prompt_mlp_tile(h_ref, nm_ref, wup_ref, wdown_ref, nf_ref, y_ref, hn_sc, acc_sc)

    @pl.when(step == n_tiles)
    def _():
        _sample_tail(xs_ref, yconv_ref, inner_ref, gate_ref, wout_ref, nm_ref, wup_ref, wdown_ref,
                     nf_ref, cross_sc, ymix_sc, ys_ref)


def _mlp_call(h, norm_mlp, w_up, w_down, norm_final, q, kt, v, state, xs, yconv, inner, gate, w_out):
    rows = h.shape[0]
    n = xs.shape[0]
    tm = MLP_TILE
    n_tiles = rows // tm
    bb = SAMPLE_BLOCK
    assert n_tiles * bb == n
    last = n_tiles - 1

    def tile(i):
        return jnp.minimum(i, last)

    return pl.pallas_call(
        _mlp_kernel,
        grid=(n_tiles + 1,),
        in_specs=[
            pl.BlockSpec((tm, D_MODEL), lambda i: (tile(i), 0)),
            _full((1, D_MODEL)),
            _resident((D_MODEL, D_FF)),
            _resident((D_FF, D_MODEL)),
            _full((1, D_MODEL)),
            pl.BlockSpec((bb, RET_WIDTH), lambda i: (tile(i), 0)),
            _full((RET_HEADS, HEAD_DIM, n)),
            _full((n, RET_WIDTH)),
            pl.BlockSpec((bb, RET_HEADS, HEAD_DIM, HEAD_DIM), lambda i: (tile(i), 0, 0, 0)),
            _full((n, 1, D_MODEL)),
            _full((n, CONV_CH)),
            _full((n, RET_WIDTH)),
            _full((n, RET_WIDTH)),
            _resident((D_MODEL, D_MODEL)),
        ],
        out_specs=[
            pl.BlockSpec((tm, D_MODEL), lambda i: (tile(i), 0)),
            pl.BlockSpec((bb, RET_HEADS, HEAD_DIM, HEAD_DIM), lambda i: (tile(i), 0, 0, 0)),
            _full((n, 1, D_MODEL)),
        ],
        out_shape=(jax.ShapeDtypeStruct((rows, D_MODEL), F32),
                   jax.ShapeDtypeStruct(state.shape, F32),
                   jax.ShapeDtypeStruct((n, 1, D_MODEL), F32)),
        scratch_shapes=[pltpu.VMEM((2, MLP_SUBTILE, D_MODEL), BF16),
                        pltpu.VMEM((MLP_SUBTILE, D_MODEL), F32),
                        pltpu.VMEM((n, RET_WIDTH), F32),
                        pltpu.VMEM((n, D_MODEL), BF16)],
        compiler_params=pltpu.CompilerParams(
            dimension_semantics=("arbitrary",), vmem_limit_bytes=VMEM_LIMIT_BYTES),
        name="mlp_and_sample",
    )(h, norm_mlp, w_up, w_down, norm_final, q, kt, v, state, xs, yconv, inner, gate, w_out)


def kernel(x_prompt, x_sample, cache_conv, state_ret, meta_tokens, norm_mix, w_in, conv_w, w_out,
           norm_mlp, w_up, w_down, norm_final):
    bsz, seq, _ = x_prompt.shape
    half = jnp.arange(0, HEAD_DIM, 2, dtype=F32) / HEAD_DIM
    inv_half = 1.0 / (ROPE_BASE ** half)
    inv_freq = jnp.concatenate([inv_half, inv_half])[None, :]

    nmix = norm_mix[0][None, :]
    nmlp = norm_mlp[0][None, :]
    nfin = norm_final[None, :]

    (w_in_b, w_out_b, cos, sin, dec, smeta, cmeta,
     q, kt, v, inner, gate, yconv, cnew) = _prologue_call(
        meta_tokens, x_sample, cache_conv[0], nmix, w_in[0], w_out[0], conv_w, inv_freq, seq)
    h1, conv_p, ret_p, w_up_b, w_down_b = _mixer_call(
        x_prompt, nmix, w_in_b, conv_w, w_out_b, cos, sin, dec, smeta, cmeta, w_up[0], w_down[0])
    y_prompt, ret_s, y_sample = _mlp_call(
        h1.reshape(bsz * seq, D_MODEL), nmlp, w_up_b, w_down_b, nfin,
        q, kt, v, state_ret[0], x_sample, yconv, inner, gate, w_out_b)
    y_prompt = y_prompt.reshape(bsz, seq, D_MODEL)

    return (y_prompt, y_sample, conv_p, ret_p, cnew[None], ret_s[None])
```

```python
import functools

import numpy as np
import jax
import jax.numpy as jnp
from jax import lax
from jax.experimental import pallas as pl
from jax.experimental.pallas import tpu as pltpu

D_MODEL = 1024
N_META = 16
CONV_CH = 512
CONV_WIDTH = 3
RET_HEADS = 4
HEAD_DIM = 128
RET_WIDTH = RET_HEADS * HEAD_DIM
CHUNK = 128
D_FF = 4 * D_MODEL
EPS = 1e-6
ROPE_BASE = 10000.0
PAST_LEN = 16384
K_SCALE = HEAD_DIM ** -0.5

OFF_U, OFF_C, OFF_B, OFF_Q, OFF_K, OFF_V, OFF_G = 0, 512, 1024, 1536, 2048, 2560, 3072
IN_TOTAL = 3584

GAMMA = tuple(1.0 - 2.0 ** (-5.0 - h) for h in range(RET_HEADS))
LOG_GAMMA = tuple(float(np.log(g)) for g in GAMMA)

TOKEN_TILE = 1024
MIX_SUBTILE = 512
PROJ_BLOCK = 512
MIX_PROJ_BLOCK = 256
OUT_BLOCK = 256
ROPE_PIECE = 256
MLP_TILE = 1024
MLP_SUBTILE = 512
SAMPLE_BLOCK = 8
VMEM_LIMIT_BYTES = 60 * 1024 * 1024

F32 = jnp.float32
BF16 = jnp.bfloat16


def _rmsnorm(x, gain):
    return x * lax.rsqrt(jnp.mean(x * x, axis=-1, keepdims=True) + EPS) * gain


def _dot(a, b):
    return jnp.dot(a, b, preferred_element_type=F32)


def _dot_nt(a, b):
    return lax.dot_general(a, b, (((1,), (1,)), ((), ())), preferred_element_type=F32)


def _dot_tn(a, b):
    return lax.dot_general(a, b, (((0,), (0,)), ((), ())), preferred_element_type=F32)


def _rope_tables(pos, inv_freq):
    ang = pos * inv_freq
    lane = lax.broadcasted_iota(jnp.int32, ang.shape, 1)
    sin = jnp.sin(ang)
    return jnp.cos(ang), jnp.where(lane < HEAD_DIM // 2, -sin, sin)


def _rope(x, cos, sin):
    return x * cos + pltpu.roll(x, HEAD_DIM // 2, 1) * sin


def _decay_tables(dec_ref):
    row = lax.broadcasted_iota(jnp.int32, (CHUNK, CHUNK), 0).astype(F32)
    col = lax.broadcasted_iota(jnp.int32, (CHUNK, CHUNK), 1).astype(F32)
    diff = row - col
    for h in range(RET_HEADS):
        lg = LOG_GAMMA[h]
        dec_ref[h] = jnp.where(diff >= 0, jnp.exp(lg * jnp.maximum(diff, 0.0)), 0.0)
        dec_ref[RET_HEADS + h] = jnp.exp((row + 1.0) * lg)
        dec_ref[2 * RET_HEADS + h] = jnp.exp((CHUNK - 1.0 - row) * lg)


def _full(shape):
    return pl.BlockSpec(shape, lambda *_: (0,) * len(shape))


def _resident(shape):
    return pl.BlockSpec(shape, lambda *_: (0,) * len(shape), pipeline_mode=pl.Buffered(1))


def _z_slice(zbuf, rows, off, width):
    blk, inner = divmod(off, PROJ_BLOCK)
    assert inner + width <= PROJ_BLOCK
    return zbuf[blk, rows, inner:inner + width]


def _meta_tail(zbuf, invf_ref, smeta_ref, cmeta_ref):
    rows = slice(0, CHUNK)
    cu = _z_slice(zbuf, rows, OFF_C, CONV_CH) * _z_slice(zbuf, rows, OFF_U, CONV_CH)
    cmeta_ref[...] = cu[N_META - 2:N_META, :]
    row = lax.broadcasted_iota(jnp.int32, (CHUNK, HEAD_DIM), 0).astype(F32)
    cos, sin = _rope_tables(row, invf_ref[...])
    for h in range(RET_HEADS):
        kz = _z_slice(zbuf, rows, OFF_K + h * HEAD_DIM, HEAD_DIM)
        vz = _z_slice(zbuf, rows, OFF_V + h * HEAD_DIM, HEAD_DIM)
        k = _rope(kz, cos, sin) * K_SCALE
        kdec = jnp.where(row < N_META, jnp.exp((N_META - 1.0 - row) * LOG_GAMMA[h]), 0.0)
        smeta_ref[h] = _dot_tn((k * kdec).astype(BF16), vz.astype(BF16))


def _sample_proj_tail(zbuf, rows, cache_ref, cw_ref, invf_ref,
                      q_ref, kt_ref, v_ref, inner_ref, gate_ref, yconv_ref, cnew_ref):
    prev0 = cache_ref[:, 0, :]
    prev1 = cache_ref[:, 1, :]
    cu = _z_slice(zbuf, rows, OFF_C, CONV_CH) * _z_slice(zbuf, rows, OFF_U, CONV_CH)
    conv = cw_ref[0, 0:1, :] * prev0 + cw_ref[0, 1:2, :] * prev1 + cw_ref[0, 2:3, :] * cu
    yconv_ref[...] = _z_slice(zbuf, rows, OFF_B, CONV_CH) * conv
    cnew_ref[:, 0, :] = prev1
    cnew_ref[:, 1, :] = cu
    pos = jnp.full((1, HEAD_DIM), float(PAST_LEN), F32)
    cos, sin = _rope_tables(pos, invf_ref[...])
    for h in range(RET_HEADS):
        cols = slice(h * HEAD_DIM, (h + 1) * HEAD_DIM)
        q = _rope(_z_slice(zbuf, rows, OFF_Q + h * HEAD_DIM, HEAD_DIM), cos, sin)
        k = _rope(_z_slice(zbuf, rows, OFF_K + h * HEAD_DIM, HEAD_DIM), cos, sin) * K_SCALE
        v = _z_slice(zbuf, rows, OFF_V + h * HEAD_DIM, HEAD_DIM)
        g = _z_slice(zbuf, rows, OFF_G + h * HEAD_DIM, HEAD_DIM)
        q_ref[:, cols] = q
        kt_ref[h] = k.T
        v_ref[:, cols] = v
        inner_ref[:, cols] = jnp.sum(q * k, axis=-1, keepdims=True) * v
        gate_ref[:, cols] = g * jax.nn.sigmoid(g)


def _prologue_kernel(meta_ref, xs_ref, cache_ref, nm_ref, win_ref, wout_ref, cw_ref, invf_ref,
                     winb_ref, woutb_ref, cos_ref, sin_ref, dec_ref, smeta_ref, cmeta_ref,
                     q_ref, kt_ref, v_ref, inner_ref, gate_ref, yconv_ref, cnew_ref,
                     xpad, hn_sc, zbuf, rope_sc):
    j = pl.program_id(0)
    n_steps = pl.num_programs(0)
    n_dec = xs_ref.shape[0]
    n_pieces = cos_ref.shape[0] // ROPE_PIECE

    def rope_piece(p):
        row0 = pl.multiple_of(p * ROPE_PIECE, ROPE_PIECE)
        base = jnp.asarray(N_META + p * ROPE_PIECE).astype(F32) * invf_ref[...]
        cb, sb = jnp.cos(base), jnp.sin(base)
        co, so = rope_sc[0], rope_sc[1]
        lane = lax.broadcasted_iota(jnp.int32, (ROPE_PIECE, HEAD_DIM), 1)
        sin = sb * co + cb * so
        cos_ref[pl.ds(row0, ROPE_PIECE), :] = cb * co - sb * so
        sin_ref[pl.ds(row0, ROPE_PIECE), :] = jnp.where(lane < HEAD_DIM // 2, -sin, sin)

    @pl.when(j == 0)
    def _():
        xpad[...] = jnp.zeros_like(xpad)
        xpad[0:N_META, :] = meta_ref[...]
        hn_sc[0:CHUNK, :] = _rmsnorm(xpad[...], nm_ref[...]).astype(BF16)
        hn_sc[CHUNK:CHUNK + n_dec, :] = _rmsnorm(xs_ref[:, 0, :], nm_ref[...]).astype(BF16)
        woutb_ref[...] = wout_ref[...].astype(BF16)
        _decay_tables(dec_ref)
        offs = lax.broadcasted_iota(jnp.int32, (ROPE_PIECE, HEAD_DIM), 0).astype(F32)
        ang = offs * invf_ref[...]
        rope_sc[0] = jnp.cos(ang)
        rope_sc[1] = jnp.sin(ang)

    w_blk = win_ref[...].astype(BF16)
    winb_ref[...] = w_blk
    zbuf[j] = _dot(hn_sc[...], w_blk)
    rope_piece(j)

    @pl.when(j == n_steps - 1)
    def _():
        for p in range(IN_TOTAL // PROJ_BLOCK, n_pieces):
            rope_piece(p)
        _meta_tail(zbuf, invf_ref, smeta_ref, cmeta_ref)
        _sample_proj_tail(zbuf, slice(CHUNK, CHUNK + n_dec), cache_ref, cw_ref, invf_ref,
                          q_ref, kt_ref, v_ref, inner_ref, gate_ref, yconv_ref, cnew_ref)


def _prologue_call(meta, xs, cache, norm_mix, w_in, w_out, conv_w, inv_freq, seq):
    n = xs.shape[0]
    n_blk = IN_TOTAL // PROJ_BLOCK
    assert seq % ROPE_PIECE == 0 and seq // ROPE_PIECE >= n_blk
    wide = jax.ShapeDtypeStruct((n, RET_WIDTH), F32)
    return pl.pallas_call(
        _prologue_kernel,
        grid=(n_blk,),
        in_specs=[
            _full((N_META, D_MODEL)),
            _full((n, 1, D_MODEL)),
            _full((n, CONV_WIDTH - 1, CONV_CH)),
            _full((1, D_MODEL)),
            pl.BlockSpec((D_MODEL, PROJ_BLOCK), lambda j: (0, j)),
            _full((D_MODEL, D_MODEL)),
            _full((1, CONV_WIDTH, CONV_CH)),
            _full((1, HEAD_DIM)),
        ],
        out_specs=[
            pl.BlockSpec((D_MODEL, PROJ_BLOCK), lambda j: (0, j)),
            _full((D_MODEL, D_MODEL)),
            _full((seq, HEAD_DIM)),
            _full((seq, HEAD_DIM)),
            _full((3 * RET_HEADS, CHUNK, CHUNK)),
            _full((RET_HEADS, HEAD_DIM, HEAD_DIM)),
            _full((CONV_WIDTH - 1, CONV_CH)),
            _full((n, RET_WIDTH)),
            _full((RET_HEADS, HEAD_DIM, n)),
            _full((n, RET_WIDTH)),
            _full((n, RET_WIDTH)),
            _full((n, RET_WIDTH)),
            _full((n, CONV_CH)),
            _full((n, CONV_WIDTH - 1, CONV_CH)),
        ],
        out_shape=(jax.ShapeDtypeStruct((D_MODEL, IN_TOTAL), BF16),
                   jax.ShapeDtypeStruct((D_MODEL, D_MODEL), BF16),
                   jax.ShapeDtypeStruct((seq, HEAD_DIM), F32),
                   jax.ShapeDtypeStruct((seq, HEAD_DIM), F32),
                   jax.ShapeDtypeStruct((3 * RET_HEADS, CHUNK, CHUNK), F32),
                   jax.ShapeDtypeStruct((RET_HEADS, HEAD_DIM, HEAD_DIM), F32),
                   jax.ShapeDtypeStruct((CONV_WIDTH - 1, CONV_CH), F32),
                   wide,
                   jax.ShapeDtypeStruct((RET_HEADS, HEAD_DIM, n), F32),
                   wide, wide, wide,
                   jax.ShapeDtypeStruct((n, CONV_CH), F32),
                   jax.ShapeDtypeStruct((n, CONV_WIDTH - 1, CONV_CH), F32)),
        scratch_shapes=[pltpu.VMEM((CHUNK, D_MODEL), F32),
                        pltpu.VMEM((CHUNK + n, D_MODEL), BF16),
                        pltpu.VMEM((n_blk, CHUNK + n, PROJ_BLOCK), F32),
                        pltpu.VMEM((2, ROPE_PIECE, HEAD_DIM), F32)],
        compiler_params=pltpu.CompilerParams(
            dimension_semantics=("arbitrary",), vmem_limit_bytes=VMEM_LIMIT_BYTES),
        name="prologue",
    )(meta, xs, cache, norm_mix, w_in, w_out, conv_w, inv_freq)


def _mixer_kernel(x_ref, nm_ref, win_ref, cw_ref, wout_ref, cos_ref, sin_ref, dec_ref,
                  smeta_ref, cmeta_ref, wup_ref, wdown_ref,
                  h1_ref, cstate_ref, sstate_ref, wup_bf_ref, wdown_bf_ref,
                  zbuf, cbuf, ymix, hn_sc):
    t = pl.program_id(1)
    tm = TOKEN_TILE

    wup_bf_ref[...] = wup_ref[...].astype(BF16)
    wdown_bf_ref[...] = wdown_ref[...].astype(BF16)

    @pl.when(t == 0)
    def _():
        sstate_ref[0, 0] = smeta_ref[...]
        cbuf[6:8, :] = cmeta_ref[...]

    n_sub = tm // MIX_SUBTILE
    chunks_per_sub = MIX_SUBTILE // CHUNK

    def sub_rows(s):
        return slice(s * MIX_SUBTILE, (s + 1) * MIX_SUBTILE)

    def stage_rows(s):
        hn_sc[s % 2] = _rmsnorm(x_ref[0, sub_rows(s), :], nm_ref[...]).astype(BF16)

    def project(s, p):
        cols = slice(p * MIX_PROJ_BLOCK, (p + 1) * MIX_PROJ_BLOCK)
        zbuf[s % 2, :, cols] = _dot(hn_sc[s % 2], win_ref[:, cols])

    def out_project(s, p):
        cols = slice(p * OUT_BLOCK, (p + 1) * OUT_BLOCK)
        h1_ref[0, sub_rows(s), cols] = (x_ref[0, sub_rows(s), cols]
                                        + _dot(ymix[s % 2], wout_ref[:, cols]))

    def conv_chunk(s, c):
        z = zbuf.at[s % 2]
        rows = slice(c * CHUNK, (c + 1) * CHUNK)
        g0 = (s * chunks_per_sub + c) * CHUNK
        cu = z[rows, OFF_C:OFF_C + CONV_CH] * z[rows, OFF_U:OFF_U + CONV_CH]
        cbuf[8 + g0:8 + g0 + CHUNK, :] = cu
        conv = (cw_ref[0, 0:1, :] * cbuf[6 + g0:6 + g0 + CHUNK, :]
                + cw_ref[0, 1:2, :] * cbuf[7 + g0:7 + g0 + CHUNK, :]
                + cw_ref[0, 2:3, :] * cu)
        ymix[s % 2, rows, 0:CONV_CH] = (z[rows, OFF_B:OFF_B + CONV_CH] * conv).astype(BF16)

    saved = {}

    def scores_and_state(s, c):
        z_cur = zbuf.at[s % 2]
        rows = slice(c * CHUNK, (c + 1) * CHUNK)
        g0 = (s * chunks_per_sub + c) * CHUNK
        cos = cos_ref[g0:g0 + CHUNK, :]
        sin = sin_ref[g0:g0 + CHUNK, :]
        qbs, kbs, vbs = [], [], []
        for h in range(RET_HEADS):
            qz = z_cur[rows, OFF_Q + h * HEAD_DIM:OFF_Q + (h + 1) * HEAD_DIM]
            kz = z_cur[rows, OFF_K + h * HEAD_DIM:OFF_K + (h + 1) * HEAD_DIM]
            vb = z_cur[rows, OFF_V + h * HEAD_DIM:OFF_V + (h + 1) * HEAD_DIM].astype(BF16)
            k = _rope(kz, cos, sin) * K_SCALE
            state = sstate_ref[0, 0, h]
            kd = (k * dec_ref[2 * RET_HEADS + h]).astype(BF16)
            sstate_ref[0, 0, h] = state * (GAMMA[h] ** CHUNK) + _dot_tn(kd, vb)
            qbs.append(_rope(qz, cos, sin).astype(BF16))
            kbs.append(jnp.concatenate([k.T.astype(BF16), state.astype(BF16)], axis=1))
            vbs.append(vb)
        both = [_dot(qbs[h], kbs[h]) for h in range(RET_HEADS)]
        scores = [b[:, 0:CHUNK] for b in both]
        cross = [b[:, CHUNK:CHUNK + HEAD_DIM] for b in both]
        saved[(s, c)] = (scores, cross, vbs)

    def outputs(s, c):
        z_cur = zbuf.at[s % 2]
        rows = slice(c * CHUNK, (c + 1) * CHUNK)
        scores, cross, vbs = saved.pop((s, c))
        for h in range(RET_HEADS):
            gz = z_cur[rows, OFF_G + h * HEAD_DIM:OFF_G + (h + 1) * HEAD_DIM]
            p = (scores[h] * dec_ref[h]).astype(BF16)
            o = _dot(p, vbs[h]) + cross[h] * dec_ref[RET_HEADS + h]
            on = o * lax.rsqrt(jnp.mean(o * o, axis=-1, keepdims=True) + EPS)
            ymix[s % 2, rows, CONV_CH + h * HEAD_DIM:CONV_CH + (h + 1) * HEAD_DIM] = (
                on * (gz * jax.nn.sigmoid(gz))).astype(BF16)

    stage_rows(0)
    for p in range(IN_TOTAL // MIX_PROJ_BLOCK):
        project(0, p)
    for s in range(n_sub):
        fillers = []
        if s > 0:
            fillers += [functools.partial(out_project, s - 1, p)
                        for p in range(D_MODEL // OUT_BLOCK)]
        if s + 1 < n_sub:
            stage_rows(s + 1)
            fillers += [functools.partial(project, s + 1, p)
                        for p in range(IN_TOTAL // MIX_PROJ_BLOCK)]
        steps = []
        for c in range(chunks_per_sub):
            steps.append(functools.partial(scores_and_state, s, c))
            if c > 0:
                steps.append(functools.partial(outputs, s, c - 1))
        steps.append(functools.partial(outputs, s, chunks_per_sub - 1))
        n_f, n_s = len(fillers), len(steps)
        placed = 0
        for i, step in enumerate(steps):
            while placed < n_f and placed * n_s <= i * n_f:
                fillers[placed]()
                placed += 1
            step()
            if i % 2 == 0:
                conv_chunk(s, i // 2)
        while placed < n_f:
            fillers[placed]()
            placed += 1
    for p in range(D_MODEL // OUT_BLOCK):
        out_project(n_sub - 1, p)

    tail = cbuf[8 + tm - 2:8 + tm, :]
    cbuf[6:8, :] = tail
    cstate_ref[0, 0] = tail


def _mixer_call(x, norm_mix, w_in, conv_w, w_out, cos, sin, dec, smeta, cmeta, w_up, w_down):
    bsz, seq, _ = x.shape
    tm = TOKEN_TILE
    n_t = seq // tm
    grid = (bsz, n_t)
    ff_slice = D_FF // (bsz * n_t)
    return pl.pallas_call(
        _mixer_kernel,
        grid=grid,
        in_specs=[
            pl.BlockSpec((1, tm, D_MODEL), lambda b, t: (b, t, 0)),
            _full((1, D_MODEL)),
            _resident((D_MODEL, IN_TOTAL)),
            _full((1, CONV_WIDTH, CONV_CH)),
            _resident((D_MODEL, D_MODEL)),
            pl.BlockSpec((tm, HEAD_DIM), lambda b, t: (t, 0)),
            pl.BlockSpec((tm, HEAD_DIM), lambda b, t: (t, 0)),
            _full((3 * RET_HEADS, CHUNK, CHUNK)),
            _full((RET_HEADS, HEAD_DIM, HEAD_DIM)),
            _full((CONV_WIDTH - 1, CONV_CH)),
            pl.BlockSpec((D_MODEL, ff_slice), lambda b, t: (0, b * n_t + t)),
            pl.BlockSpec((ff_slice, D_MODEL), lambda b, t: (b * n_t + t, 0)),
        ],
        out_specs=[
            pl.BlockSpec((1, tm, D_MODEL), lambda b, t: (b, t, 0)),
            pl.BlockSpec((1, 1, CONV_WIDTH - 1, CONV_CH), lambda b, t: (0, b, 0, 0)),
            pl.BlockSpec((1, 1, RET_HEADS, HEAD_DIM, HEAD_DIM), lambda b, t: (0, b, 0, 0, 0)),
            pl.BlockSpec((D_MODEL, ff_slice), lambda b, t: (0, b * n_t + t)),
            pl.BlockSpec((ff_slice, D_MODEL), lambda b, t: (b * n_t + t, 0)),
        ],
        out_shape=(
            jax.ShapeDtypeStruct((bsz, seq, D_MODEL), F32),
            jax.ShapeDtypeStruct((1, bsz, CONV_WIDTH - 1, CONV_CH), F32),
            jax.ShapeDtypeStruct((1, bsz, RET_HEADS, HEAD_DIM, HEAD_DIM), F32),
            jax.ShapeDtypeStruct((D_MODEL, D_FF), BF16),
            jax.ShapeDtypeStruct((D_FF, D_MODEL), BF16),
        ),
        scratch_shapes=[
            pltpu.VMEM((2, MIX_SUBTILE, IN_TOTAL), F32),
            pltpu.VMEM((tm + 8, CONV_CH), F32),
            pltpu.VMEM((2, MIX_SUBTILE, D_MODEL), BF16),
            pltpu.VMEM((2, MIX_SUBTILE, D_MODEL), BF16),
        ],
        compiler_params=pltpu.CompilerParams(
            dimension_semantics=("arbitrary", "arbitrary"), vmem_limit_bytes=VMEM_LIMIT_BYTES),
        name="prompt_mixer",
    )(x, norm_mix, w_in, conv_w, w_out, cos, sin, dec, smeta, cmeta, w_up, w_down)


def _mlp_body(h, nm, wup_ref, wdown_ref, nf):
    hn = _rmsnorm(h, nm).astype(BF16)
    acc = h
    ff_block = 1024
    for c in range(D_FF // ff_block):
        up = _dot(hn, wup_ref[:, c * ff_block:(c + 1) * ff_block])
        act = jnp.square(jnp.maximum(up, 0.0)).astype(BF16)
        acc = acc + _dot(act, wdown_ref[c * ff_block:(c + 1) * ff_block, :])
    return _rmsnorm(acc, nf)


def _prompt_mlp_tile(h_ref, nm_ref, wup_ref, wdown_ref, nf_ref, y_ref, hn_sc, acc_sc):
    n_sub = MLP_TILE // MLP_SUBTILE
    ff_block = 1024
    n_ff = D_FF // ff_block

    def rows(s):
        return slice(s * MLP_SUBTILE, (s + 1) * MLP_SUBTILE)

    hn_sc[0] = _rmsnorm(h_ref[rows(0), :], nm_ref[...]).astype(BF16)
    for s in range(n_sub):
        hn = hn_sc[s % 2]
        acc = h_ref[rows(s), :]
        for c in range(n_ff):
            up = _dot(hn, wup_ref[:, c * ff_block:(c + 1) * ff_block])
            act = jnp.square(jnp.maximum(up, 0.0)).astype(BF16)
            acc = acc + _dot(act, wdown_ref[c * ff_block:(c + 1) * ff_block, :])
            if c == 0 and s > 0:
                y_ref[rows(s - 1), :] = _rmsnorm(acc_sc[...], nf_ref[...])
            if c == 1 and s + 1 < n_sub:
                hn_sc[(s + 1) % 2] = _rmsnorm(h_ref[rows(s + 1), :], nm_ref[...]).astype(BF16)
        if s + 1 < n_sub:
            acc_sc[...] = acc
        else:
            y_ref[rows(s), :] = _rmsnorm(acc, nf_ref[...])


def _sample_state_block(step, q_ref, kt_ref, v_ref, st_ref, new_ref, cross_sc):
    assert SAMPLE_BLOCK == 8
    lane = lax.broadcasted_iota(jnp.int32, (HEAD_DIM, HEAD_DIM), 1)
    sub = lax.broadcasted_iota(jnp.int32, (SAMPLE_BLOCK, HEAD_DIM), 0)
    row0 = pl.multiple_of(step * SAMPLE_BLOCK, SAMPLE_BLOCK)
    for h in range(RET_HEADS):
        cols = slice(h * HEAD_DIM, (h + 1) * HEAD_DIM)
        vb = v_ref[:, cols].astype(BF16)
        cross = jnp.zeros((SAMPLE_BLOCK, HEAD_DIM), F32)
        for i in range(SAMPLE_BLOCK):
            b = step * SAMPLE_BLOCK + i
            state = st_ref[i, h]
            q8 = jnp.broadcast_to(q_ref[i:i + 1, cols], (SAMPLE_BLOCK, HEAD_DIM)).astype(BF16)
            cross = jnp.where(sub == i, _dot(q8, state.astype(BF16)), cross)
            kt_b = jnp.where(lane == b, kt_ref[h], 0.0).astype(BF16)
            new_ref[i, h] = state * GAMMA[h] + _dot(kt_b, vb)
        cross_sc[pl.ds(row0, SAMPLE_BLOCK), cols] = cross * GAMMA[h]


def _sample_tail(x_ref, yconv_ref, inner_ref, gate_ref, wout_ref, nmlp_ref, wup_ref, wdown_ref,
                 nf_ref, cross_sc, ymix, y_ref):
    ymix[:, 0:CONV_CH] = yconv_ref[...].astype(BF16)
    for h in range(RET_HEADS):
        cols = slice(h * HEAD_DIM, (h + 1) * HEAD_DIM)
        o = inner_ref[:, cols] + cross_sc[:, cols]
        on = o * lax.rsqrt(jnp.mean(o * o, axis=-1, keepdims=True) + EPS)
        ymix[:, CONV_CH + h * HEAD_DIM:CONV_CH + (h + 1) * HEAD_DIM] = (
            on * gate_ref[:, cols]).astype(BF16)
    h1 = x_ref[:, 0, :] + _dot(ymix[...], wout_ref[...])
    y_ref[:, 0, :] = _mlp_body(h1, nmlp_ref[...], wup_ref, wdown_ref, nf_ref[...])


def _mlp_kernel(h_ref, nm_ref, wup_ref, wdown_ref, nf_ref,
                q_ref, kt_ref, v_ref, st_ref, xs_ref, yconv_ref, inner_ref, gate_ref, wout_ref,
                y_ref, new_ref, ys_ref,
                hn_sc, acc_sc, cross_sc, ymix_sc):
    step = pl.program_id(0)
    n_tiles = pl.num_programs(0) - 1

    @pl.when(step < n_tiles)
    def _():
        _sample_state_block(step, q_ref, kt_ref, v_ref, st_ref, new_ref, cross_sc)
        _prompt_mlp_tile(h_ref, nm_ref, wup_ref, wdown_ref, nf_ref, y_ref, hn_sc, acc_sc)

    @pl.when(step == n_tiles)
    def _():
        _sample_tail(xs_ref, yconv_ref, inner_ref, gate_ref, wout_ref, nm_ref, wup_ref, wdown_ref,
                     nf_ref, cross_sc, ymix_sc, ys_ref)


def _mlp_call(h, norm_mlp, w_up, w_down, norm_final, q, kt, v, state, xs, yconv, inner, gate, w_out):
    rows = h.shape[0]
    n = xs.shape[0]
    tm = MLP_TILE
    n_tiles = rows // tm
    bb = SAMPLE_BLOCK
    assert n_tiles * bb == n
    last = n_tiles - 1

    def tile(i):
        return jnp.minimum(i, last)

    return pl.pallas_call(
        _mlp_kernel,
        grid=(n_tiles + 1,),
        in_specs=[
            pl.BlockSpec((tm, D_MODEL), lambda i: (tile(i), 0)),
            _full((1, D_MODEL)),
            _resident((D_MODEL, D_FF)),
            _resident((D_FF, D_MODEL)),
            _full((1, D_MODEL)),
            pl.BlockSpec((bb, RET_WIDTH), lambda i: (tile(i), 0)),
            _full((RET_HEADS, HEAD_DIM, n)),
            _full((n, RET_WIDTH)),
            pl.BlockSpec((bb, RET_HEADS, HEAD_DIM, HEAD_DIM), lambda i: (tile(i), 0, 0, 0)),
            _full((n, 1, D_MODEL)),
            _full((n, CONV_CH)),
            _full((n, RET_WIDTH)),
            _full((n, RET_WIDTH)),
            _resident((D_MODEL, D_MODEL)),
        ],
        out_specs=[
            pl.BlockSpec((tm, D_MODEL), lambda i: (tile(i), 0)),
            pl.BlockSpec((bb, RET_HEADS, HEAD_DIM, HEAD_DIM), lambda i: (tile(i), 0, 0, 0)),
            _full((n, 1, D_MODEL)),
        ],
        out_shape=(jax.ShapeDtypeStruct((rows, D_MODEL), F32),
                   jax.ShapeDtypeStruct(state.shape, F32),
                   jax.ShapeDtypeStruct((n, 1, D_MODEL), F32)),
        scratch_shapes=[pltpu.VMEM((2, MLP_SUBTILE, D_MODEL), BF16),
                        pltpu.VMEM((MLP_SUBTILE, D_MODEL), F32),
                        pltpu.VMEM((n, RET_WIDTH), F32),
                        pltpu.VMEM((n, D_MODEL), BF16)],
        compiler_params=pltpu.CompilerParams(
            dimension_semantics=("arbitrary",), vmem_limit_bytes=VMEM_LIMIT_BYTES),
        name="mlp_and_sample",
    )(h, norm_mlp, w_up, w_down, norm_final, q, kt, v, state, xs, yconv, inner, gate, w_out)


def kernel(x_prompt, x_sample, cache_conv, state_ret, meta_tokens, norm_mix, w_in, conv_w, w_out,
           norm_mlp, w_up, w_down, norm_final):
    bsz, seq, _ = x_prompt.shape
    half = jnp.arange(0, HEAD_DIM, 2, dtype=F32) / HEAD_DIM
    inv_half = 1.0 / (ROPE_BASE ** half)
    inv_freq = jnp.concatenate([inv_half, inv_half])[None, :]

    nmix = norm_mix[0][None, :]
    nmlp = norm_mlp[0][None, :]
    nfin = norm_final[None, :]

    (w_in_b, w_out_b, cos, sin, dec, smeta, cmeta,
     q, kt, v, inner, gate, yconv, cnew) = _prologue_call(
        meta_tokens, x_sample, cache_conv[0], nmix, w_in[0], w_out[0], conv_w, inv_freq, seq)
    h1, conv_p, ret_p, w_up_b, w_down_b = _mixer_call(
        x_prompt, nmix, w_in_b, conv_w, w_out_b, cos, sin, dec, smeta, cmeta, w_up[0], w_down[0])
    y_prompt, ret_s, y_sample = _mlp_call(
        h1.reshape(bsz * seq, D_MODEL), nmlp, w_up_b, w_down_b, nfin,
        q, kt, v, state_ret[0], x_sample, yconv, inner, gate, w_out_b)
    y_prompt = y_prompt.reshape(bsz, seq, D_MODEL)

    return (y_prompt, y_sample, conv_p, ret_p, cnew[None], ret_s[None])
```

```python
import functools

import numpy as np
import jax
import jax.numpy as jnp
from jax import lax
from jax.experimental import pallas as pl
from jax.experimental.pallas import tpu as pltpu

D_MODEL = 1024
N_META = 16
CONV_CH = 512
CONV_WIDTH = 3
RET_HEADS = 4
HEAD_DIM = 128
RET_WIDTH = RET_HEADS * HEAD_DIM
CHUNK = 128
D_FF = 4 * D_MODEL
EPS = 1e-6
ROPE_BASE = 10000.0
PAST_LEN = 16384
K_SCALE = HEAD_DIM ** -0.5

OFF_U, OFF_C, OFF_B, OFF_Q, OFF_K, OFF_V, OFF_G = 0, 512, 1024, 1536, 2048, 2560, 3072
IN_TOTAL = 3584

GAMMA = tuple(1.0 - 2.0 ** (-5.0 - h) for h in range(RET_HEADS))
LOG_GAMMA = tuple(float(np.log(g)) for g in GAMMA)

TOKEN_TILE = 1024
MIX_SUBTILE = 512
PROJ_BLOCK = 512
OUT_BLOCK = 256
ROPE_PIECE = 256
MLP_TILE = 1024
MLP_SUBTILE = 512
FF_BLOCK = 1024
SAMPLE_BLOCK = 8
VMEM_LIMIT_BYTES = 60 * 1024 * 1024

F32 = jnp.float32
BF16 = jnp.bfloat16


def _rmsnorm(x, gain):
    return x * lax.rsqrt(jnp.mean(x * x, axis=-1, keepdims=True) + EPS) * gain


def _dot(a, b):
    return jnp.dot(a, b, preferred_element_type=F32)


def _dot_nt(a, b):
    return lax.dot_general(a, b, (((1,), (1,)), ((), ())), preferred_element_type=F32)


def _dot_tn(a, b):
    return lax.dot_general(a, b, (((0,), (0,)), ((), ())), preferred_element_type=F32)


def _rope_tables(pos, inv_freq):
    ang = pos * inv_freq
    lane = lax.broadcasted_iota(jnp.int32, ang.shape, 1)
    sin = jnp.sin(ang)
    return jnp.cos(ang), jnp.where(lane < HEAD_DIM // 2, -sin, sin)


def _rope(x, cos, sin):
    return x * cos + pltpu.roll(x, HEAD_DIM // 2, 1) * sin


def _decay_tables(dec_ref):
    row = lax.broadcasted_iota(jnp.int32, (CHUNK, CHUNK), 0).astype(F32)
    col = lax.broadcasted_iota(jnp.int32, (CHUNK, CHUNK), 1).astype(F32)
    diff = row - col
    for h in range(RET_HEADS):
        lg = LOG_GAMMA[h]
        dec_ref[h] = jnp.where(diff >= 0, jnp.exp(lg * jnp.maximum(diff, 0.0)), 0.0)
        dec_ref[RET_HEADS + h] = jnp.exp((row + 1.0) * lg)
        dec_ref[2 * RET_HEADS + h] = jnp.exp((CHUNK - 1.0 - row) * lg)


def _full(shape):
    return pl.BlockSpec(shape, lambda *_: (0,) * len(shape))


def _resident(shape):
    return pl.BlockSpec(shape, lambda *_: (0,) * len(shape), pipeline_mode=pl.Buffered(1))


def _z_slice(zbuf, rows, off, width):
    blk, inner = divmod(off, PROJ_BLOCK)
    assert inner + width <= PROJ_BLOCK
    return zbuf[blk, rows, inner:inner + width]


def _meta_tail(zbuf, invf_ref, smeta_ref, cmeta_ref):
    rows = slice(0, CHUNK)
    cu = _z_slice(zbuf, rows, OFF_C, CONV_CH) * _z_slice(zbuf, rows, OFF_U, CONV_CH)
    cmeta_ref[...] = cu[N_META - 2:N_META, :]
    row = lax.broadcasted_iota(jnp.int32, (CHUNK, HEAD_DIM), 0).astype(F32)
    cos, sin = _rope_tables(row, invf_ref[...])
    for h in range(RET_HEADS):
        kz = _z_slice(zbuf, rows, OFF_K + h * HEAD_DIM, HEAD_DIM)
        vz = _z_slice(zbuf, rows, OFF_V + h * HEAD_DIM, HEAD_DIM)
        k = _rope(kz, cos, sin) * K_SCALE
        kdec = jnp.where(row < N_META, jnp.exp((N_META - 1.0 - row) * LOG_GAMMA[h]), 0.0)
        smeta_ref[h] = _dot_tn((k * kdec).astype(BF16), vz.astype(BF16))


def _sample_proj_tail(zbuf, rows, cache_ref, cw_ref, invf_ref,
                      q_ref, kt_ref, v_ref, inner_ref, gate_ref, yconv_ref, cnew_ref):
    prev0 = cache_ref[:, 0, :]
    prev1 = cache_ref[:, 1, :]
    cu = _z_slice(zbuf, rows, OFF_C, CONV_CH) * _z_slice(zbuf, rows, OFF_U, CONV_CH)
    conv = cw_ref[0, 0:1, :] * prev0 + cw_ref[0, 1:2, :] * prev1 + cw_ref[0, 2:3, :] * cu
    yconv_ref[...] = _z_slice(zbuf, rows, OFF_B, CONV_CH) * conv
    cnew_ref[:, 0, :] = prev1
    cnew_ref[:, 1, :] = cu
    pos = jnp.full((1, HEAD_DIM), float(PAST_LEN), F32)
    cos, sin = _rope_tables(pos, invf_ref[...])
    for h in range(RET_HEADS):
        cols = slice(h * HEAD_DIM, (h + 1) * HEAD_DIM)
        q = _rope(_z_slice(zbuf, rows, OFF_Q + h * HEAD_DIM, HEAD_DIM), cos, sin)
        k = _rope(_z_slice(zbuf, rows, OFF_K + h * HEAD_DIM, HEAD_DIM), cos, sin) * K_SCALE
        v = _z_slice(zbuf, rows, OFF_V + h * HEAD_DIM, HEAD_DIM)
        g = _z_slice(zbuf, rows, OFF_G + h * HEAD_DIM, HEAD_DIM)
        q_ref[:, cols] = q
        kt_ref[h] = k.T
        v_ref[:, cols] = v
        inner_ref[:, cols] = jnp.sum(q * k, axis=-1, keepdims=True) * v
        gate_ref[:, cols] = g * jax.nn.sigmoid(g)


def _prologue_kernel(meta_ref, xs_ref, cache_ref, nm_ref, win_ref, wout_ref, cw_ref, invf_ref,
                     winb_ref, woutb_ref, cos_ref, sin_ref, dec_ref, smeta_ref, cmeta_ref,
                     q_ref, kt_ref, v_ref, inner_ref, gate_ref, yconv_ref, cnew_ref,
                     xpad, hn_sc, zbuf, rope_sc):
    j = pl.program_id(0)
    n_steps = pl.num_programs(0)
    n_dec = xs_ref.shape[0]
    n_pieces = cos_ref.shape[0] // ROPE_PIECE

    def rope_piece(p):
        row0 = pl.multiple_of(p * ROPE_PIECE, ROPE_PIECE)
        base = jnp.asarray(N_META + p * ROPE_PIECE).astype(F32) * invf_ref[...]
        cb, sb = jnp.cos(base), jnp.sin(base)
        co, so = rope_sc[0], rope_sc[1]
        lane = lax.broadcasted_iota(jnp.int32, (ROPE_PIECE, HEAD_DIM), 1)
        sin = sb * co + cb * so
        cos_ref[pl.ds(row0, ROPE_PIECE), :] = cb * co - sb * so
        sin_ref[pl.ds(row0, ROPE_PIECE), :] = jnp.where(lane < HEAD_DIM // 2, -sin, sin)

    @pl.when(j == 0)
    def _():
        xpad[...] = jnp.zeros_like(xpad)
        xpad[0:N_META, :] = meta_ref[...]
        hn_sc[0:CHUNK, :] = _rmsnorm(xpad[...], nm_ref[...]).astype(BF16)
        hn_sc[CHUNK:CHUNK + n_dec, :] = _rmsnorm(xs_ref[:, 0, :], nm_ref[...]).astype(BF16)
        woutb_ref[...] = wout_ref[...].astype(BF16)
        _decay_tables(dec_ref)
        offs = lax.broadcasted_iota(jnp.int32, (ROPE_PIECE, HEAD_DIM), 0).astype(F32)
        ang = offs * invf_ref[...]
        rope_sc[0] = jnp.cos(ang)
        rope_sc[1] = jnp.sin(ang)

    w_blk = win_ref[...].astype(BF16)
    winb_ref[...] = w_blk
    zbuf[j] = _dot(hn_sc[...], w_blk)
    rope_piece(j)

    @pl.when(j == n_steps - 1)
    def _():
        for p in range(IN_TOTAL // PROJ_BLOCK, n_pieces):
            rope_piece(p)
        _meta_tail(zbuf, invf_ref, smeta_ref, cmeta_ref)
        _sample_proj_tail(zbuf, slice(CHUNK, CHUNK + n_dec), cache_ref, cw_ref, invf_ref,
                          q_ref, kt_ref, v_ref, inner_ref, gate_ref, yconv_ref, cnew_ref)


def _prologue_call(meta, xs, cache, norm_mix, w_in, w_out, conv_w, inv_freq, seq):
    n = xs.shape[0]
    n_blk = IN_TOTAL // PROJ_BLOCK
    assert seq % ROPE_PIECE == 0 and seq // ROPE_PIECE >= n_blk
    wide = jax.ShapeDtypeStruct((n, RET_WIDTH), F32)
    return pl.pallas_call(
        _prologue_kernel,
        grid=(n_blk,),
        in_specs=[
            _full((N_META, D_MODEL)),
            _full((n, 1, D_MODEL)),
            _full((n, CONV_WIDTH - 1, CONV_CH)),
            _full((1, D_MODEL)),
            pl.BlockSpec((D_MODEL, PROJ_BLOCK), lambda j: (0, j)),
            _full((D_MODEL, D_MODEL)),
            _full((1, CONV_WIDTH, CONV_CH)),
            _full((1, HEAD_DIM)),
        ],
        out_specs=[
            pl.BlockSpec((D_MODEL, PROJ_BLOCK), lambda j: (0, j)),
            _full((D_MODEL, D_MODEL)),
            _full((seq, HEAD_DIM)),
            _full((seq, HEAD_DIM)),
            _full((3 * RET_HEADS, CHUNK, CHUNK)),
            _full((RET_HEADS, HEAD_DIM, HEAD_DIM)),
            _full((CONV_WIDTH - 1, CONV_CH)),
            _full((n, RET_WIDTH)),
            _full((RET_HEADS, HEAD_DIM, n)),
            _full((n, RET_WIDTH)),
            _full((n, RET_WIDTH)),
            _full((n, RET_WIDTH)),
            _full((n, CONV_CH)),
            _full((n, CONV_WIDTH - 1, CONV_CH)),
        ],
        out_shape=(jax.ShapeDtypeStruct((D_MODEL, IN_TOTAL), BF16),
                   jax.ShapeDtypeStruct((D_MODEL, D_MODEL), BF16),
                   jax.ShapeDtypeStruct((seq, HEAD_DIM), F32),
                   jax.ShapeDtypeStruct((seq, HEAD_DIM), F32),
                   jax.ShapeDtypeStruct((3 * RET_HEADS, CHUNK, CHUNK), F32),
                   jax.ShapeDtypeStruct((RET_HEADS, HEAD_DIM, HEAD_DIM), F32),
                   jax.ShapeDtypeStruct((CONV_WIDTH - 1, CONV_CH), F32),
                   wide,
                   jax.ShapeDtypeStruct((RET_HEADS, HEAD_DIM, n), F32),
                   wide, wide, wide,
                   jax.ShapeDtypeStruct((n, CONV_CH), F32),
                   jax.ShapeDtypeStruct((n, CONV_WIDTH - 1, CONV_CH), F32)),
        scratch_shapes=[pltpu.VMEM((CHUNK, D_MODEL), F32),
                        pltpu.VMEM((CHUNK + n, D_MODEL), BF16),
                        pltpu.VMEM((n_blk, CHUNK + n, PROJ_BLOCK), F32),
                        pltpu.VMEM((2, ROPE_PIECE, HEAD_DIM), F32)],
        compiler_params=pltpu.CompilerParams(
            dimension_semantics=("arbitrary",), vmem_limit_bytes=VMEM_LIMIT_BYTES),
        name="prologue",
    )(meta, xs, cache, norm_mix, w_in, w_out, conv_w, inv_freq)


def _mixer_kernel(x_ref, nm_ref, win_ref, cw_ref, wout_ref, cos_ref, sin_ref, dec_ref,
                  smeta_ref, cmeta_ref, wup_ref, wdown_ref,
                  h1_ref, cstate_ref, sstate_ref, wup_bf_ref, wdown_bf_ref,
                  zbuf, cbuf, ymix, hn_sc):
    t = pl.program_id(1)
    tm = TOKEN_TILE

    wup_bf_ref[...] = wup_ref[...].astype(BF16)
    wdown_bf_ref[...] = wdown_ref[...].astype(BF16)

    @pl.when(t == 0)
    def _():
        sstate_ref[0, 0] = smeta_ref[...]
        cbuf[6:8, :] = cmeta_ref[...]

    n_sub = tm // MIX_SUBTILE
    chunks_per_sub = MIX_SUBTILE // CHUNK

    def sub_rows(s):
        return slice(s * MIX_SUBTILE, (s + 1) * MIX_SUBTILE)

    def stage_rows(s):
        hn_sc[s % 2] = _rmsnorm(x_ref[0, sub_rows(s), :], nm_ref[...]).astype(BF16)

    def project(s, p):
        cols = slice(p * PROJ_BLOCK, (p + 1) * PROJ_BLOCK)
        zbuf[s % 2, :, cols] = _dot(hn_sc[s % 2], win_ref[:, cols])

    def out_project(s, p):
        cols = slice(p * OUT_BLOCK, (p + 1) * OUT_BLOCK)
        h1_ref[0, sub_rows(s), cols] = (x_ref[0, sub_rows(s), cols]
                                        + _dot(ymix[s % 2], wout_ref[:, cols]))

    def conv_chunk(s, c):
        z = zbuf.at[s % 2]
        rows = slice(c * CHUNK, (c + 1) * CHUNK)
        g0 = (s * chunks_per_sub + c) * CHUNK
        cu = z[rows, OFF_C:OFF_C + CONV_CH] * z[rows, OFF_U:OFF_U + CONV_CH]
        cbuf[8 + g0:8 + g0 + CHUNK, :] = cu
        conv = (cw_ref[0, 0:1, :] * cbuf[6 + g0:6 + g0 + CHUNK, :]
                + cw_ref[0, 1:2, :] * cbuf[7 + g0:7 + g0 + CHUNK, :]
                + cw_ref[0, 2:3, :] * cu)
        ymix[s % 2, rows, 0:CONV_CH] = (z[rows, OFF_B:OFF_B + CONV_CH] * conv).astype(BF16)

    saved = {}

    def scores_and_state(s, c):
        z_cur = zbuf.at[s % 2]
        rows = slice(c * CHUNK, (c + 1) * CHUNK)
        g0 = (s * chunks_per_sub + c) * CHUNK
        cos = cos_ref[g0:g0 + CHUNK, :]
        sin = sin_ref[g0:g0 + CHUNK, :]
        qbs, kbs, vbs, states = [], [], [], []
        for h in range(RET_HEADS):
            qz = z_cur[rows, OFF_Q + h * HEAD_DIM:OFF_Q + (h + 1) * HEAD_DIM]
            kz = z_cur[rows, OFF_K + h * HEAD_DIM:OFF_K + (h + 1) * HEAD_DIM]
            vb = z_cur[rows, OFF_V + h * HEAD_DIM:OFF_V + (h + 1) * HEAD_DIM].astype(BF16)
            k = _rope(kz, cos, sin) * K_SCALE
            state = sstate_ref[0, 0, h]
            kd = (k * dec_ref[2 * RET_HEADS + h]).astype(BF16)
            sstate_ref[0, 0, h] = state * (GAMMA[h] ** CHUNK) + _dot_tn(kd, vb)
            qbs.append(_rope(qz, cos, sin).astype(BF16))
            kbs.append(jnp.concatenate([k.T.astype(BF16), state.astype(BF16)], axis=1))
            vbs.append(vb)
        both = [_dot(qbs[h], kbs[h]) for h in range(RET_HEADS)]
        scores = [b[:, 0:CHUNK] for b in both]
        cross = [b[:, CHUNK:CHUNK + HEAD_DIM] for b in both]
        saved[(s, c)] = (scores, cross, vbs)

    def outputs(s, c):
        z_cur = zbuf.at[s % 2]
        rows = slice(c * CHUNK, (c + 1) * CHUNK)
        scores, cross, vbs = saved.pop((s, c))
        for h in range(RET_HEADS):
            gz = z_cur[rows, OFF_G + h * HEAD_DIM:OFF_G + (h + 1) * HEAD_DIM]
            p = (scores[h] * dec_ref[h]).astype(BF16)
            o = _dot(p, vbs[h]) + cross[h] * dec_ref[RET_HEADS + h]
            on = o * lax.rsqrt(jnp.mean(o * o, axis=-1, keepdims=True) + EPS)
            ymix[s % 2, rows, CONV_CH + h * HEAD_DIM:CONV_CH + (h + 1) * HEAD_DIM] = (
                on * (gz * jax.nn.sigmoid(gz))).astype(BF16)

    stage_rows(0)
    for p in range(IN_TOTAL // PROJ_BLOCK):
        project(0, p)
    for s in range(n_sub):
        fillers = []
        if s > 0:
            fillers += [functools.partial(out_project, s - 1, p)
                        for p in range(D_MODEL // OUT_BLOCK)]
        if s + 1 < n_sub:
            stage_rows(s + 1)
            fillers += [functools.partial(project, s + 1, p)
                        for p in range(IN_TOTAL // PROJ_BLOCK)]
        steps = []
        for c in range(chunks_per_sub):
            steps.append(functools.partial(scores_and_state, s, c))
            if c > 0:
                steps.append(functools.partial(outputs, s, c - 1))
        steps.append(functools.partial(outputs, s, chunks_per_sub - 1))
        n_f, n_s = len(fillers), len(steps)
        placed = 0
        for i, step in enumerate(steps):
            while placed < n_f and placed * n_s <= i * n_f:
                fillers[placed]()
                placed += 1
            step()
            if i % 2 == 0:
                conv_chunk(s, i // 2)
        while placed < n_f:
            fillers[placed]()
            placed += 1
    for p in range(D_MODEL // OUT_BLOCK):
        out_project(n_sub - 1, p)

    tail = cbuf[8 + tm - 2:8 + tm, :]
    cbuf[6:8, :] = tail
    cstate_ref[0, 0] = tail


def _mixer_call(x, norm_mix, w_in, conv_w, w_out, cos, sin, dec, smeta, cmeta, w_up, w_down):
    bsz, seq, _ = x.shape
    tm = TOKEN_TILE
    n_t = seq // tm
    grid = (bsz, n_t)
    ff_slice = D_FF // (bsz * n_t)
    return pl.pallas_call(
        _mixer_kernel,
        grid=grid,
        in_specs=[
            pl.BlockSpec((1, tm, D_MODEL), lambda b, t: (b, t, 0)),
            _full((1, D_MODEL)),
            _resident((D_MODEL, IN_TOTAL)),
            _full((1, CONV_WIDTH, CONV_CH)),
            _resident((D_MODEL, D_MODEL)),
            pl.BlockSpec((tm, HEAD_DIM), lambda b, t: (t, 0)),
            pl.BlockSpec((tm, HEAD_DIM), lambda b, t: (t, 0)),
            _full((3 * RET_HEADS, CHUNK, CHUNK)),
            _full((RET_HEADS, HEAD_DIM, HEAD_DIM)),
            _full((CONV_WIDTH - 1, CONV_CH)),
            pl.BlockSpec((D_MODEL, ff_slice), lambda b, t: (0, b * n_t + t)),
            pl.BlockSpec((ff_slice, D_MODEL), lambda b, t: (b * n_t + t, 0)),
        ],
        out_specs=[
            pl.BlockSpec((1, tm, D_MODEL), lambda b, t: (b, t, 0)),
            pl.BlockSpec((1, 1, CONV_WIDTH - 1, CONV_CH), lambda b, t: (0, b, 0, 0)),
            pl.BlockSpec((1, 1, RET_HEADS, HEAD_DIM, HEAD_DIM), lambda b, t: (0, b, 0, 0, 0)),
            pl.BlockSpec((D_MODEL, ff_slice), lambda b, t: (0, b * n_t + t)),
            pl.BlockSpec((ff_slice, D_MODEL), lambda b, t: (b * n_t + t, 0)),
        ],
        out_shape=(
            jax.ShapeDtypeStruct((bsz, seq, D_MODEL), F32),
            jax.ShapeDtypeStruct((1, bsz, CONV_WIDTH - 1, CONV_CH), F32),
            jax.ShapeDtypeStruct((1, bsz, RET_HEADS, HEAD_DIM, HEAD_DIM), F32),
            jax.ShapeDtypeStruct((D_MODEL, D_FF), BF16),
            jax.ShapeDtypeStruct((D_FF, D_MODEL), BF16),
        ),
        scratch_shapes=[
            pltpu.VMEM((2, MIX_SUBTILE, IN_TOTAL), F32),
            pltpu.VMEM((tm + 8, CONV_CH), F32),
            pltpu.VMEM((2, MIX_SUBTILE, D_MODEL), BF16),
            pltpu.VMEM((2, MIX_SUBTILE, D_MODEL), BF16),
        ],
        compiler_params=pltpu.CompilerParams(
            dimension_semantics=("arbitrary", "arbitrary"), vmem_limit_bytes=VMEM_LIMIT_BYTES),
        name="prompt_mixer",
    )(x, norm_mix, w_in, conv_w, w_out, cos, sin, dec, smeta, cmeta, w_up, w_down)


def _mlp_body(h, nm, wup_ref, wdown_ref, nf):
    hn = _rmsnorm(h, nm).astype(BF16)
    acc = h
    ff_block = FF_BLOCK
    for c in range(D_FF // ff_block):
        up = _dot(hn, wup_ref[:, c * ff_block:(c + 1) * ff_block])
        act = jnp.square(jnp.maximum(up, 0.0)).astype(BF16)
        acc = acc + _dot(act, wdown_ref[c * ff_block:(c + 1) * ff_block, :])
    return _rmsnorm(acc, nf)


def _prompt_mlp_tile(h_ref, nm_ref, wup_ref, wdown_ref, nf_ref, y_ref, hn_sc, acc_sc,
                     before_up=None, before_down=None):
    n_sub = MLP_TILE // MLP_SUBTILE
    ff_block = FF_BLOCK
    n_ff = D_FF // ff_block

    def rows(s):
        return slice(s * MLP_SUBTILE, (s + 1) * MLP_SUBTILE)

    hn_sc[0] = _rmsnorm(h_ref[rows(0), :], nm_ref[...]).astype(BF16)
    for s in range(n_sub):
        hn = hn_sc[s % 2]
        acc = h_ref[rows(s), :]
        for c in range(n_ff):
            if s == 0 and before_up is not None:
                before_up(c)
            up = _dot(hn, wup_ref[:, c * ff_block:(c + 1) * ff_block])
            act = jnp.square(jnp.maximum(up, 0.0)).astype(BF16)
            if s == 0 and before_down is not None:
                before_down(c)
            acc = acc + _dot(act, wdown_ref[c * ff_block:(c + 1) * ff_block, :])
            if c == 0 and s > 0:
                y_ref[rows(s - 1), :] = _rmsnorm(acc_sc[...], nf_ref[...])
            if c == 1 and s + 1 < n_sub:
                hn_sc[(s + 1) % 2] = _rmsnorm(h_ref[rows(s + 1), :], nm_ref[...]).astype(BF16)
        if s + 1 < n_sub:
            acc_sc[...] = acc
        else:
            y_ref[rows(s), :] = _rmsnorm(acc, nf_ref[...])


def _sample_state_block(step, q_ref, kt_ref, v_ref, st_ref, new_ref, cross_sc):
    assert SAMPLE_BLOCK == 8
    lane = lax.broadcasted_iota(jnp.int32, (HEAD_DIM, HEAD_DIM), 1)
    sub = lax.broadcasted_iota(jnp.int32, (SAMPLE_BLOCK, HEAD_DIM), 0)
    row0 = pl.multiple_of(step * SAMPLE_BLOCK, SAMPLE_BLOCK)
    for h in range(RET_HEADS):
        cols = slice(h * HEAD_DIM, (h + 1) * HEAD_DIM)
        vb = v_ref[:, cols].astype(BF16)
        cross = jnp.zeros((SAMPLE_BLOCK, HEAD_DIM), F32)
        for i in range(SAMPLE_BLOCK):
            b = step * SAMPLE_BLOCK + i
            state = st_ref[i, h]
            q8 = jnp.broadcast_to(q_ref[i:i + 1, cols], (SAMPLE_BLOCK, HEAD_DIM)).astype(BF16)
            cross = jnp.where(sub == i, _dot(q8, state.astype(BF16)), cross)
            kt_b = jnp.where(lane == b, kt_ref[h], 0.0).astype(BF16)
            new_ref[i, h] = state * GAMMA[h] + _dot(kt_b, vb)
        cross_sc[pl.ds(row0, SAMPLE_BLOCK), cols] = cross * GAMMA[h]


def _sample_tail(x_ref, yconv_ref, inner_ref, gate_ref, wout_ref, nmlp_ref, wup_ref, wdown_ref,
                 nf_ref, cross_sc, ymix, y_ref):
    ymix[:, 0:CONV_CH] = yconv_ref[...].astype(BF16)
    for h in range(RET_HEADS):
        cols = slice(h * HEAD_DIM, (h + 1) * HEAD_DIM)
        o = inner_ref[:, cols] + cross_sc[:, cols]
        on = o * lax.rsqrt(jnp.mean(o * o, axis=-1, keepdims=True) + EPS)
        ymix[:, CONV_CH + h * HEAD_DIM:CONV_CH + (h + 1) * HEAD_DIM] = (
            on * gate_ref[:, cols]).astype(BF16)
    h1 = x_ref[:, 0, :] + _dot(ymix[...], wout_ref[...])
    y_ref[:, 0, :] = _mlp_body(h1, nmlp_ref[...], wup_ref, wdown_ref, nf_ref[...])


def _mlp_kernel(h_ref, nm_ref, wup_hbm, wdown_hbm, nf_ref,
                q_ref, kt_ref, v_ref, st_ref, xs_ref, yconv_ref, inner_ref, gate_ref, wout_ref,
                y_ref, new_ref, ys_ref,
                hn_sc, acc_sc, cross_sc, ymix_sc, wup_ref, wdown_ref, wsem):
    step = pl.program_id(0)
    n_tiles = pl.num_programs(0) - 1
    n_ff = D_FF // FF_BLOCK

    def up_copy(c):
        cols = slice(c * FF_BLOCK, (c + 1) * FF_BLOCK)
        return pltpu.make_async_copy(wup_hbm.at[:, cols], wup_ref.at[:, cols], wsem.at[0, c])

    def down_copy(c):
        rows = slice(c * FF_BLOCK, (c + 1) * FF_BLOCK)
        return pltpu.make_async_copy(wdown_hbm.at[rows, :], wdown_ref.at[rows, :], wsem.at[1, c])

    @pl.when(step == 0)
    def _():
        for c in range(n_ff):
            up_copy(c).start()
            down_copy(c).start()
        _sample_state_block(step, q_ref, kt_ref, v_ref, st_ref, new_ref, cross_sc)
        _prompt_mlp_tile(h_ref, nm_ref, wup_ref, wdown_ref, nf_ref, y_ref, hn_sc, acc_sc,
                         before_up=lambda c: up_copy(c).wait(),
                         before_down=lambda c: down_copy(c).wait())

    @pl.when(jnp.logical_and(step > 0, step < n_tiles))
    def _():
        _sample_state_block(step, q_ref, kt_ref, v_ref, st_ref, new_ref, cross_sc)
        _prompt_mlp_tile(h_ref, nm_ref, wup_ref, wdown_ref, nf_ref, y_ref, hn_sc, acc_sc)

    @pl.when(step == n_tiles)
    def _():
        _sample_tail(xs_ref, yconv_ref, inner_ref, gate_ref, wout_ref, nm_ref, wup_ref, wdown_ref,
                     nf_ref, cross_sc, ymix_sc, ys_ref)


def _mlp_call(h, norm_mlp, w_up, w_down, norm_final, q, kt, v, state, xs, yconv, inner, gate, w_out):
    rows = h.shape[0]
    n = xs.shape[0]
    tm = MLP_TILE
    n_tiles = rows // tm
    bb = SAMPLE_BLOCK
    assert n_tiles * bb == n
    last = n_tiles - 1

    def tile(i):
        return jnp.minimum(i, last)

    return pl.pallas_call(
        _mlp_kernel,
        grid=(n_tiles + 1,),
        in_specs=[
            pl.BlockSpec((tm, D_MODEL), lambda i: (tile(i), 0)),
            _full((1, D_MODEL)),
            pl.BlockSpec(memory_space=pl.ANY),
            pl.BlockSpec(memory_space=pl.ANY),
            _full((1, D_MODEL)),
            pl.BlockSpec((bb, RET_WIDTH), lambda i: (tile(i), 0)),
            _full((RET_HEADS, HEAD_DIM, n)),
            _full((n, RET_WIDTH)),
            pl.BlockSpec((bb, RET_HEADS, HEAD_DIM, HEAD_DIM), lambda i: (tile(i), 0, 0, 0)),
            _full((n, 1, D_MODEL)),
            _full((n, CONV_CH)),
            _full((n, RET_WIDTH)),
            _full((n, RET_WIDTH)),
            _resident((D_MODEL, D_MODEL)),
        ],
        out_specs=[
            pl.BlockSpec((tm, D_MODEL), lambda i: (tile(i), 0)),
            pl.BlockSpec((bb, RET_HEADS, HEAD_DIM, HEAD_DIM), lambda i: (tile(i), 0, 0, 0)),
            _full((n, 1, D_MODEL)),
        ],
        out_shape=(jax.ShapeDtypeStruct((rows, D_MODEL), F32),
                   jax.ShapeDtypeStruct(state.shape, F32),
                   jax.ShapeDtypeStruct((n, 1, D_MODEL), F32)),
        scratch_shapes=[pltpu.VMEM((2, MLP_SUBTILE, D_MODEL), BF16),
                        pltpu.VMEM((MLP_SUBTILE, D_MODEL), F32),
                        pltpu.VMEM((n, RET_WIDTH), F32),
                        pltpu.VMEM((n, D_MODEL), BF16),
                        pltpu.VMEM((D_MODEL, D_FF), BF16),
                        pltpu.VMEM((D_FF, D_MODEL), BF16),
                        pltpu.SemaphoreType.DMA((2, D_FF // FF_BLOCK))],
        compiler_params=pltpu.CompilerParams(
            dimension_semantics=("arbitrary",), vmem_limit_bytes=VMEM_LIMIT_BYTES),
        name="mlp_and_sample",
    )(h, norm_mlp, w_up, w_down, norm_final, q, kt, v, state, xs, yconv, inner, gate, w_out)


def kernel(x_prompt, x_sample, cache_conv, state_ret, meta_tokens, norm_mix, w_in, conv_w, w_out,
           norm_mlp, w_up, w_down, norm_final):
    bsz, seq, _ = x_prompt.shape
    half = jnp.arange(0, HEAD_DIM, 2, dtype=F32) / HEAD_DIM
    inv_half = 1.0 / (ROPE_BASE ** half)
    inv_freq = jnp.concatenate([inv_half, inv_half])[None, :]

    nmix = norm_mix[0][None, :]
    nmlp = norm_mlp[0][None, :]
    nfin = norm_final[None, :]

    (w_in_b, w_out_b, cos, sin, dec, smeta, cmeta,
     q, kt, v, inner, gate, yconv, cnew) = _prologue_call(
        meta_tokens, x_sample, cache_conv[0], nmix, w_in[0], w_out[0], conv_w, inv_freq, seq)
    h1, conv_p, ret_p, w_up_b, w_down_b = _mixer_call(
        x_prompt, nmix, w_in_b, conv_w, w_out_b, cos, sin, dec, smeta, cmeta, w_up[0], w_down[0])
    y_prompt, ret_s, y_sample = _mlp_call(
        h1.reshape(bsz * seq, D_MODEL), nmlp, w_up_b, w_down_b, nfin,
        q, kt, v, state_ret[0], x_sample, yconv, inner, gate, w_out_b)
    y_prompt = y_prompt.reshape(bsz, seq, D_MODEL)

    return (y_prompt, y_sample, conv_p, ret_p, cnew[None], ret_s[None])
```

```python
import functools

import numpy as np
import jax
import jax.numpy as jnp
from jax import lax
from jax.experimental import pallas as pl
from jax.experimental.pallas import tpu as pltpu

D_MODEL = 1024
N_META = 16
CONV_CH = 512
CONV_WIDTH = 3
RET_HEADS = 4
HEAD_DIM = 128
RET_WIDTH = RET_HEADS * HEAD_DIM
CHUNK = 128
D_FF = 4 * D_MODEL
EPS = 1e-6
ROPE_BASE = 10000.0
PAST_LEN = 16384
K_SCALE = HEAD_DIM ** -0.5

OFF_U, OFF_C, OFF_B, OFF_Q, OFF_K, OFF_V, OFF_G = 0, 512, 1024, 1536, 2048, 2560, 3072
IN_TOTAL = 3584

GAMMA = tuple(1.0 - 2.0 ** (-5.0 - h) for h in range(RET_HEADS))
LOG_GAMMA = tuple(float(np.log(g)) for g in GAMMA)

TOKEN_TILE = 1024
MIX_SUBTILE = 512
PROJ_BLOCK = 512
OUT_BLOCK = 256
ROPE_PIECE = 256
MLP_TILE = 1024
MLP_SUBTILE = 512
SAMPLE_BLOCK = 8
VMEM_LIMIT_BYTES = 60 * 1024 * 1024

F32 = jnp.float32
BF16 = jnp.bfloat16


def _rmsnorm(x, gain):
    return x * lax.rsqrt(jnp.mean(x * x, axis=-1, keepdims=True) + EPS) * gain


def _dot(a, b):
    return jnp.dot(a, b, preferred_element_type=F32)


def _dot_nt(a, b):
    return lax.dot_general(a, b, (((1,), (1,)), ((), ())), preferred_element_type=F32)


def _dot_tn(a, b):
    return lax.dot_general(a, b, (((0,), (0,)), ((), ())), preferred_element_type=F32)


def _rope_tables(pos, inv_freq):
    ang = pos * inv_freq
    lane = lax.broadcasted_iota(jnp.int32, ang.shape, 1)
    sin = jnp.sin(ang)
    return jnp.cos(ang), jnp.where(lane < HEAD_DIM // 2, -sin, sin)


def _rope(x, cos, sin):
    return x * cos + pltpu.roll(x, HEAD_DIM // 2, 1) * sin


def _decay_tables(dec_ref):
    row = lax.broadcasted_iota(jnp.int32, (CHUNK, CHUNK), 0).astype(F32)
    col = lax.broadcasted_iota(jnp.int32, (CHUNK, CHUNK), 1).astype(F32)
    diff = row - col
    for h in range(RET_HEADS):
        lg = LOG_GAMMA[h]
        dec_ref[h] = jnp.where(diff >= 0, jnp.exp(lg * jnp.maximum(diff, 0.0)), 0.0)
        dec_ref[RET_HEADS + h] = jnp.exp((row + 1.0) * lg)
        dec_ref[2 * RET_HEADS + h] = jnp.exp((CHUNK - 1.0 - row) * lg)


def _full(shape):
    return pl.BlockSpec(shape, lambda *_: (0,) * len(shape))


def _resident(shape):
    return pl.BlockSpec(shape, lambda *_: (0,) * len(shape), pipeline_mode=pl.Buffered(1))


def _z_slice(zbuf, rows, off, width):
    blk, inner = divmod(off, PROJ_BLOCK)
    assert inner + width <= PROJ_BLOCK
    return zbuf[blk, rows, inner:inner + width]


def _meta_tail(zbuf, invf_ref, smeta_ref, cmeta_ref):
    rows = slice(0, CHUNK)
    cu = _z_slice(zbuf, rows, OFF_C, CONV_CH) * _z_slice(zbuf, rows, OFF_U, CONV_CH)
    cmeta_ref[...] = cu[N_META - 2:N_META, :]
    row = lax.broadcasted_iota(jnp.int32, (CHUNK, HEAD_DIM), 0).astype(F32)
    cos, sin = _rope_tables(row, invf_ref[...])
    for h in range(RET_HEADS):
        kz = _z_slice(zbuf, rows, OFF_K + h * HEAD_DIM, HEAD_DIM)
        vz = _z_slice(zbuf, rows, OFF_V + h * HEAD_DIM, HEAD_DIM)
        k = _rope(kz, cos, sin) * K_SCALE
        kdec = jnp.where(row < N_META, jnp.exp((N_META - 1.0 - row) * LOG_GAMMA[h]), 0.0)
        smeta_ref[h] = _dot_tn((k * kdec).astype(BF16), vz.astype(BF16))


def _sample_proj_tail(zbuf, rows, cache_ref, cw_ref, invf_ref,
                      q_ref, kt_ref, v_ref, inner_ref, gate_ref, yconv_ref, cnew_ref):
    prev0 = cache_ref[:, 0, :]
    prev1 = cache_ref[:, 1, :]
    cu = _z_slice(zbuf, rows, OFF_C, CONV_CH) * _z_slice(zbuf, rows, OFF_U, CONV_CH)
    conv = cw_ref[0, 0:1, :] * prev0 + cw_ref[0, 1:2, :] * prev1 + cw_ref[0, 2:3, :] * cu
    yconv_ref[...] = _z_slice(zbuf, rows, OFF_B, CONV_CH) * conv
    cnew_ref[:, 0, :] = prev1
    cnew_ref[:, 1, :] = cu
    pos = jnp.full((1, HEAD_DIM), float(PAST_LEN), F32)
    cos, sin = _rope_tables(pos, invf_ref[...])
    for h in range(RET_HEADS):
        cols = slice(h * HEAD_DIM, (h + 1) * HEAD_DIM)
        q = _rope(_z_slice(zbuf, rows, OFF_Q + h * HEAD_DIM, HEAD_DIM), cos, sin)
        k = _rope(_z_slice(zbuf, rows, OFF_K + h * HEAD_DIM, HEAD_DIM), cos, sin) * K_SCALE
        v = _z_slice(zbuf, rows, OFF_V + h * HEAD_DIM, HEAD_DIM)
        g = _z_slice(zbuf, rows, OFF_G + h * HEAD_DIM, HEAD_DIM)
        q_ref[:, cols] = q
        kt_ref[h] = k.T
        v_ref[:, cols] = v
        inner_ref[:, cols] = jnp.sum(q * k, axis=-1, keepdims=True) * v
        gate_ref[:, cols] = g * jax.nn.sigmoid(g)


def _prologue_kernel(meta_ref, xs_ref, cache_ref, nm_ref, win_ref, wout_ref, cw_ref, invf_ref,
                     winb_ref, woutb_ref, cos_ref, sin_ref, dec_ref, smeta_ref, cmeta_ref,
                     q_ref, kt_ref, v_ref, inner_ref, gate_ref, yconv_ref, cnew_ref,
                     xpad, hn_sc, zbuf, rope_sc):
    j = pl.program_id(0)
    n_steps = pl.num_programs(0)
    n_dec = xs_ref.shape[0]
    n_pieces = cos_ref.shape[0] // ROPE_PIECE

    def rope_piece(p):
        row0 = pl.multiple_of(p * ROPE_PIECE, ROPE_PIECE)
        base = jnp.asarray(N_META + p * ROPE_PIECE).astype(F32) * invf_ref[...]
        cb, sb = jnp.cos(base), jnp.sin(base)
        co, so = rope_sc[0], rope_sc[1]
        lane = lax.broadcasted_iota(jnp.int32, (ROPE_PIECE, HEAD_DIM), 1)
        sin = sb * co + cb * so
        cos_ref[pl.ds(row0, ROPE_PIECE), :] = cb * co - sb * so
        sin_ref[pl.ds(row0, ROPE_PIECE), :] = jnp.where(lane < HEAD_DIM // 2, -sin, sin)

    @pl.when(j == 0)
    def _():
        xpad[...] = jnp.zeros_like(xpad)
        xpad[0:N_META, :] = meta_ref[...]
        hn_sc[0:CHUNK, :] = _rmsnorm(xpad[...], nm_ref[...]).astype(BF16)
        hn_sc[CHUNK:CHUNK + n_dec, :] = _rmsnorm(xs_ref[:, 0, :], nm_ref[...]).astype(BF16)
        woutb_ref[...] = wout_ref[...].astype(BF16)
        _decay_tables(dec_ref)
        offs = lax.broadcasted_iota(jnp.int32, (ROPE_PIECE, HEAD_DIM), 0).astype(F32)
        ang = offs * invf_ref[...]
        rope_sc[0] = jnp.cos(ang)
        rope_sc[1] = jnp.sin(ang)

    w_blk = win_ref[...].astype(BF16)
    winb_ref[...] = w_blk
    zbuf[j] = _dot(hn_sc[...], w_blk)
    rope_piece(j)

    @pl.when(j == n_steps - 1)
    def _():
        for p in range(IN_TOTAL // PROJ_BLOCK, n_pieces):
            rope_piece(p)
        _meta_tail(zbuf, invf_ref, smeta_ref, cmeta_ref)
        _sample_proj_tail(zbuf, slice(CHUNK, CHUNK + n_dec), cache_ref, cw_ref, invf_ref,
                          q_ref, kt_ref, v_ref, inner_ref, gate_ref, yconv_ref, cnew_ref)


def _prologue_call(meta, xs, cache, norm_mix, w_in, w_out, conv_w, inv_freq, seq):
    n = xs.shape[0]
    n_blk = IN_TOTAL // PROJ_BLOCK
    assert seq % ROPE_PIECE == 0 and seq // ROPE_PIECE >= n_blk
    wide = jax.ShapeDtypeStruct((n, RET_WIDTH), F32)
    return pl.pallas_call(
        _prologue_kernel,
        grid=(n_blk,),
        in_specs=[
            _full((N_META, D_MODEL)),
            _full((n, 1, D_MODEL)),
            _full((n, CONV_WIDTH - 1, CONV_CH)),
            _full((1, D_MODEL)),
            pl.BlockSpec((D_MODEL, PROJ_BLOCK), lambda j: (0, j)),
            _full((D_MODEL, D_MODEL)),
            _full((1, CONV_WIDTH, CONV_CH)),
            _full((1, HEAD_DIM)),
        ],
        out_specs=[
            pl.BlockSpec((D_MODEL, PROJ_BLOCK), lambda j: (0, j)),
            _full((D_MODEL, D_MODEL)),
            _full((seq, HEAD_DIM)),
            _full((seq, HEAD_DIM)),
            _full((3 * RET_HEADS, CHUNK, CHUNK)),
            _full((RET_HEADS, HEAD_DIM, HEAD_DIM)),
            _full((CONV_WIDTH - 1, CONV_CH)),
            _full((n, RET_WIDTH)),
            _full((RET_HEADS, HEAD_DIM, n)),
            _full((n, RET_WIDTH)),
            _full((n, RET_WIDTH)),
            _full((n, RET_WIDTH)),
            _full((n, CONV_CH)),
            _full((n, CONV_WIDTH - 1, CONV_CH)),
        ],
        out_shape=(jax.ShapeDtypeStruct((D_MODEL, IN_TOTAL), BF16),
                   jax.ShapeDtypeStruct((D_MODEL, D_MODEL), BF16),
                   jax.ShapeDtypeStruct((seq, HEAD_DIM), F32),
                   jax.ShapeDtypeStruct((seq, HEAD_DIM), F32),
                   jax.ShapeDtypeStruct((3 * RET_HEADS, CHUNK, CHUNK), F32),
                   jax.ShapeDtypeStruct((RET_HEADS, HEAD_DIM, HEAD_DIM), F32),
                   jax.ShapeDtypeStruct((CONV_WIDTH - 1, CONV_CH), F32),
                   wide,
                   jax.ShapeDtypeStruct((RET_HEADS, HEAD_DIM, n), F32),
                   wide, wide, wide,
                   jax.ShapeDtypeStruct((n, CONV_CH), F32),
                   jax.ShapeDtypeStruct((n, CONV_WIDTH - 1, CONV_CH), F32)),
        scratch_shapes=[pltpu.VMEM((CHUNK, D_MODEL), F32),
                        pltpu.VMEM((CHUNK + n, D_MODEL), BF16),
                        pltpu.VMEM((n_blk, CHUNK + n, PROJ_BLOCK), F32),
                        pltpu.VMEM((2, ROPE_PIECE, HEAD_DIM), F32)],
        compiler_params=pltpu.CompilerParams(
            dimension_semantics=("arbitrary",), vmem_limit_bytes=VMEM_LIMIT_BYTES),
        name="prologue",
    )(meta, xs, cache, norm_mix, w_in, w_out, conv_w, inv_freq)


def _mixer_kernel(x_ref, nm_ref, win_ref, cw_ref, wout_ref, cos_ref, sin_ref, dec_ref,
                  smeta_ref, cmeta_ref, wup_ref, wdown_ref,
                  h1_ref, cstate_ref, sstate_ref, wup_bf_ref, wdown_bf_ref,
                  zbuf, cbuf, ymix, hn_sc):
    t = pl.program_id(1)
    tm = TOKEN_TILE

    wup_bf_ref[...] = wup_ref[...].astype(BF16)
    wdown_bf_ref[...] = wdown_ref[...].astype(BF16)

    @pl.when(t == 0)
    def _():
        sstate_ref[0, 0] = smeta_ref[...]
        cbuf[6:8, :] = cmeta_ref[...]

    n_sub = tm // MIX_SUBTILE
    chunks_per_sub = MIX_SUBTILE // CHUNK

    def sub_rows(s):
        return slice(s * MIX_SUBTILE, (s + 1) * MIX_SUBTILE)

    def stage_rows(s):
        hn_sc[s % 2] = _rmsnorm(x_ref[0, sub_rows(s), :], nm_ref[...]).astype(BF16)

    def project(s, p):
        cols = slice(p * PROJ_BLOCK, (p + 1) * PROJ_BLOCK)
        zbuf[s % 2, :, cols] = _dot(hn_sc[s % 2], win_ref[:, cols])

    def out_project(s, p):
        cols = slice(p * OUT_BLOCK, (p + 1) * OUT_BLOCK)
        h1_ref[0, sub_rows(s), cols] = (x_ref[0, sub_rows(s), cols]
                                        + _dot(ymix[s % 2], wout_ref[:, cols]))

    def conv_chunk(s, c):
        z = zbuf.at[s % 2]
        rows = slice(c * CHUNK, (c + 1) * CHUNK)
        g0 = (s * chunks_per_sub + c) * CHUNK
        cu = z[rows, OFF_C:OFF_C + CONV_CH] * z[rows, OFF_U:OFF_U + CONV_CH]
        cbuf[8 + g0:8 + g0 + CHUNK, :] = cu
        conv = (cw_ref[0, 0:1, :] * cbuf[6 + g0:6 + g0 + CHUNK, :]
                + cw_ref[0, 1:2, :] * cbuf[7 + g0:7 + g0 + CHUNK, :]
                + cw_ref[0, 2:3, :] * cu)
        ymix[s % 2, rows, 0:CONV_CH] = (z[rows, OFF_B:OFF_B + CONV_CH] * conv).astype(BF16)

    saved = {}

    def scores_and_state(s, c):
        z_cur = zbuf.at[s % 2]
        rows = slice(c * CHUNK, (c + 1) * CHUNK)
        g0 = (s * chunks_per_sub + c) * CHUNK
        cos = cos_ref[g0:g0 + CHUNK, :]
        sin = sin_ref[g0:g0 + CHUNK, :]
        qbs, kbs, vbs, states = [], [], [], []
        for h in range(RET_HEADS):
            qz = z_cur[rows, OFF_Q + h * HEAD_DIM:OFF_Q + (h + 1) * HEAD_DIM]
            kz = z_cur[rows, OFF_K + h * HEAD_DIM:OFF_K + (h + 1) * HEAD_DIM]
            vb = z_cur[rows, OFF_V + h * HEAD_DIM:OFF_V + (h + 1) * HEAD_DIM].astype(BF16)
            k = _rope(kz, cos, sin) * K_SCALE
            state = sstate_ref[0, 0, h]
            kd = (k * dec_ref[2 * RET_HEADS + h]).astype(BF16)
            sstate_ref[0, 0, h] = state * (GAMMA[h] ** CHUNK) + _dot_tn(kd, vb)
            qbs.append(_rope(qz, cos, sin).astype(BF16))
            kbs.append(jnp.concatenate([k.T.astype(BF16), state.astype(BF16)], axis=1))
            vbs.append(vb)
        both = [_dot(qbs[h], kbs[h]) for h in range(RET_HEADS)]
        scores = [b[:, 0:CHUNK] for b in both]
        cross = [b[:, CHUNK:CHUNK + HEAD_DIM] for b in both]
        saved[(s, c)] = (scores, cross, vbs)

    def outputs(s, c):
        z_cur = zbuf.at[s % 2]
        rows = slice(c * CHUNK, (c + 1) * CHUNK)
        scores, cross, vbs = saved.pop((s, c))
        for h in range(RET_HEADS):
            gz = z_cur[rows, OFF_G + h * HEAD_DIM:OFF_G + (h + 1) * HEAD_DIM]
            p = (scores[h] * dec_ref[h]).astype(BF16)
            o = _dot(p, vbs[h]) + cross[h] * dec_ref[RET_HEADS + h]
            on = o * lax.rsqrt(jnp.mean(o * o, axis=-1, keepdims=True) + EPS)
            ymix[s % 2, rows, CONV_CH + h * HEAD_DIM:CONV_CH + (h + 1) * HEAD_DIM] = (
                on * (gz * jax.nn.sigmoid(gz))).astype(BF16)

    stage_rows(0)
    for p in range(IN_TOTAL // PROJ_BLOCK):
        project(0, p)
    for s in range(n_sub):
        fillers = []
        if s > 0:
            fillers += [functools.partial(out_project, s - 1, p)
                        for p in range(D_MODEL // OUT_BLOCK)]
        if s + 1 < n_sub:
            stage_rows(s + 1)
            fillers += [functools.partial(project, s + 1, p)
                        for p in range(IN_TOTAL // PROJ_BLOCK)]
        steps = []
        for c in range(chunks_per_sub):
            steps.append(functools.partial(scores_and_state, s, c))
            if c > 0:
                steps.append(functools.partial(outputs, s, c - 1))
        steps.append(functools.partial(outputs, s, chunks_per_sub - 1))
        n_f, n_s = len(fillers), len(steps)
        placed = 0
        for i, step in enumerate(steps):
            while placed < n_f and placed * n_s <= i * n_f:
                fillers[placed]()
                placed += 1
            step()
            if i % 2 == 0:
                conv_chunk(s, i // 2)
        while placed < n_f:
            fillers[placed]()
            placed += 1
    for p in range(D_MODEL // OUT_BLOCK):
        out_project(n_sub - 1, p)

    tail = cbuf[8 + tm - 2:8 + tm, :]
    cbuf[6:8, :] = tail
    cstate_ref[0, 0] = tail


def _mixer_call(x, norm_mix, w_in, conv_w, w_out, cos, sin, dec, smeta, cmeta, w_up, w_down):
    bsz, seq, _ = x.shape
    tm = TOKEN_TILE
    n_t = seq // tm
    grid = (bsz, n_t)
    ff_slice = D_FF // (bsz * n_t)
    return pl.pallas_call(
        _mixer_kernel,
        grid=grid,
        in_specs=[
            pl.BlockSpec((1, tm, D_MODEL), lambda b, t: (b, t, 0)),
            _full((1, D_MODEL)),
            _resident((D_MODEL, IN_TOTAL)),
            _full((1, CONV_WIDTH, CONV_CH)),
            _resident((D_MODEL, D_MODEL)),
            pl.BlockSpec((tm, HEAD_DIM), lambda b, t: (t, 0)),
            pl.BlockSpec((tm, HEAD_DIM), lambda b, t: (t, 0)),
            _full((3 * RET_HEADS, CHUNK, CHUNK)),
            _full((RET_HEADS, HEAD_DIM, HEAD_DIM)),
            _full((CONV_WIDTH - 1, CONV_CH)),
            pl.BlockSpec((D_MODEL, ff_slice), lambda b, t: (0, b * n_t + t)),
            pl.BlockSpec((ff_slice, D_MODEL), lambda b, t: (b * n_t + t, 0)),
        ],
        out_specs=[
            pl.BlockSpec((1, tm, D_MODEL), lambda b, t: (b, t, 0)),
            pl.BlockSpec((1, 1, CONV_WIDTH - 1, CONV_CH), lambda b, t: (0, b, 0, 0)),
            pl.BlockSpec((1, 1, RET_HEADS, HEAD_DIM, HEAD_DIM), lambda b, t: (0, b, 0, 0, 0)),
            pl.BlockSpec((D_MODEL, ff_slice), lambda b, t: (0, b * n_t + t)),
            pl.BlockSpec((ff_slice, D_MODEL), lambda b, t: (b * n_t + t, 0)),
        ],
        out_shape=(
            jax.ShapeDtypeStruct((bsz, seq, D_MODEL), F32),
            jax.ShapeDtypeStruct((1, bsz, CONV_WIDTH - 1, CONV_CH), F32),
            jax.ShapeDtypeStruct((1, bsz, RET_HEADS, HEAD_DIM, HEAD_DIM), F32),
            jax.ShapeDtypeStruct((D_MODEL, D_FF), BF16),
            jax.ShapeDtypeStruct((D_FF, D_MODEL), BF16),
        ),
        scratch_shapes=[
            pltpu.VMEM((2, MIX_SUBTILE, IN_TOTAL), F32),
            pltpu.VMEM((tm + 8, CONV_CH), F32),
            pltpu.VMEM((2, MIX_SUBTILE, D_MODEL), BF16),
            pltpu.VMEM((2, MIX_SUBTILE, D_MODEL), BF16),
        ],
        compiler_params=pltpu.CompilerParams(
            dimension_semantics=("arbitrary", "arbitrary"), vmem_limit_bytes=VMEM_LIMIT_BYTES),
        name="prompt_mixer",
    )(x, norm_mix, w_in, conv_w, w_out, cos, sin, dec, smeta, cmeta, w_up, w_down)


def _mlp_body(h, nm, wup_ref, wdown_ref, nf):
    hn = _rmsnorm(h, nm).astype(BF16)
    acc = h
    ff_block = 1024
    for c in range(D_FF // ff_block):
        up = _dot(hn, wup_ref[:, c * ff_block:(c + 1) * ff_block])
        act = jnp.square(jnp.maximum(up, 0.0)).astype(BF16)
        acc = acc + _dot(act, wdown_ref[c * ff_block:(c + 1) * ff_block, :])
    return _rmsnorm(acc, nf)


def _prompt_mlp_tile(h_ref, nm_ref, wup_ref, wdown_ref, nf_ref, y_ref, hn_sc, acc_sc):
    n_sub = MLP_TILE // MLP_SUBTILE
    ff_block = 1024
    n_ff = D_FF // ff_block

    def rows(s):
        return slice(s * MLP_SUBTILE, (s + 1) * MLP_SUBTILE)

    hn_sc[0] = _rmsnorm(h_ref[rows(0), :], nm_ref[...]).astype(BF16)
    for s in range(n_sub):
        hn = hn_sc[s % 2]
        acc = h_ref[rows(s), :]
        for c in range(n_ff):
            up = _dot(hn, wup_ref[:, c * ff_block:(c + 1) * ff_block])
            act = jnp.square(jnp.maximum(up, 0.0)).astype(BF16)
            acc = acc + _dot(act, wdown_ref[c * ff_block:(c + 1) * ff_block, :])
            if c == 0 and s > 0:
                y_ref[rows(s - 1), :] = _rmsnorm(acc_sc[...], nf_ref[...])
            if c == 1 and s + 1 < n_sub:
                hn_sc[(s + 1) % 2] = _rmsnorm(h_ref[rows(s + 1), :], nm_ref[...]).astype(BF16)
        if s + 1 < n_sub:
            acc_sc[...] = acc
        else:
            y_ref[rows(s), :] = _rmsnorm(acc, nf_ref[...])


def _sample_state_block(step, q_ref, kt_ref, v_ref, st_ref, new_ref, cross_sc):
    assert SAMPLE_BLOCK == 8
    lane = lax.broadcasted_iota(jnp.int32, (HEAD_DIM, HEAD_DIM), 1)
    sub = lax.broadcasted_iota(jnp.int32, (SAMPLE_BLOCK, HEAD_DIM), 0)
    row0 = pl.multiple_of(step * SAMPLE_BLOCK, SAMPLE_BLOCK)
    for h in range(RET_HEADS):
        cols = slice(h * HEAD_DIM, (h + 1) * HEAD_DIM)
        vb = v_ref[:, cols].astype(BF16)
        q_blk = q_ref[:, cols]
        q_diag = jnp.concatenate(
            [jnp.where(sub == i, q_blk, 0.0) for i in range(SAMPLE_BLOCK)], axis=1).astype(BF16)
        s_stack = st_ref[:, h].reshape(SAMPLE_BLOCK * HEAD_DIM, HEAD_DIM).astype(BF16)
        cross_sc[pl.ds(row0, SAMPLE_BLOCK), cols] = _dot(q_diag, s_stack) * GAMMA[h]
        for i in range(SAMPLE_BLOCK):
            b = step * SAMPLE_BLOCK + i
            kt_b = jnp.where(lane == b, kt_ref[h], 0.0).astype(BF16)
            new_ref[i, h] = st_ref[i, h] * GAMMA[h] + _dot(kt_b, vb)


def _sample_tail(x_ref, yconv_ref, inner_ref, gate_ref, wout_ref, nmlp_ref, wup_ref, wdown_ref,
                 nf_ref, cross_sc, ymix, y_ref):
    ymix[:, 0:CONV_CH] = yconv_ref[...].astype(BF16)
    for h in range(RET_HEADS):
        cols = slice(h * HEAD_DIM, (h + 1) * HEAD_DIM)
        o = inner_ref[:, cols] + cross_sc[:, cols]
        on = o * lax.rsqrt(jnp.mean(o * o, axis=-1, keepdims=True) + EPS)
        ymix[:, CONV_CH + h * HEAD_DIM:CONV_CH + (h + 1) * HEAD_DIM] = (
            on * gate_ref[:, cols]).astype(BF16)
    h1 = x_ref[:, 0, :] + _dot(ymix[...], wout_ref[...])
    y_ref[:, 0, :] = _mlp_body(h1, nmlp_ref[...], wup_ref, wdown_ref, nf_ref[...])


def _mlp_kernel(h_ref, nm_ref, wup_ref, wdown_ref, nf_ref,
                q_ref, kt_ref, v_ref, st_ref, xs_ref, yconv_ref, inner_ref, gate_ref, wout_ref,
                y_ref, new_ref, ys_ref,
                hn_sc, acc_sc, cross_sc, ymix_sc):
    step = pl.program_id(0)
    n_tiles = pl.num_programs(0) - 1

    @pl.when(step < n_tiles)
    def _():
        _sample_state_block(step, q_ref, kt_ref, v_ref, st_ref, new_ref, cross_sc)
        _prompt_mlp_tile(h_ref, nm_ref, wup_ref, wdown_ref, nf_ref, y_ref, hn_sc, acc_sc)

    @pl.when(step == n_tiles)
    def _():
        _sample_tail(xs_ref, yconv_ref, inner_ref, gate_ref, wout_ref, nm_ref, wup_ref, wdown_ref,
                     nf_ref, cross_sc, ymix_sc, ys_ref)


def _mlp_call(h, norm_mlp, w_up, w_down, norm_final, q, kt, v, state, xs, yconv, inner, gate, w_out):
    rows = h.shape[0]
    n = xs.shape[0]
    tm = MLP_TILE
    n_tiles = rows // tm
    bb = SAMPLE_BLOCK
    assert n_tiles * bb == n
    last = n_tiles - 1

    def tile(i):
        return jnp.minimum(i, last)

    return pl.pallas_call(
        _mlp_kernel,
        grid=(n_tiles + 1,),
        in_specs=[
            pl.BlockSpec((tm, D_MODEL), lambda i: (tile(i), 0)),
            _full((1, D_MODEL)),
            _resident((D_MODEL, D_FF)),
            _resident((D_FF, D_MODEL)),
            _full((1, D_MODEL)),
            pl.BlockSpec((bb, RET_WIDTH), lambda i: (tile(i), 0)),
            _full((RET_HEADS, HEAD_DIM, n)),
            _full((n, RET_WIDTH)),
            pl.BlockSpec((bb, RET_HEADS, HEAD_DIM, HEAD_DIM), lambda i: (tile(i), 0, 0, 0)),
            _full((n, 1, D_MODEL)),
            _full((n, CONV_CH)),
            _full((n, RET_WIDTH)),
            _full((n, RET_WIDTH)),
            _resident((D_MODEL, D_MODEL)),
        ],
        out_specs=[
            pl.BlockSpec((tm, D_MODEL), lambda i: (tile(i), 0)),
            pl.BlockSpec((bb, RET_HEADS, HEAD_DIM, HEAD_DIM), lambda i: (tile(i), 0, 0, 0)),
            _full((n, 1, D_MODEL)),
        ],
        out_shape=(jax.ShapeDtypeStruct((rows, D_MODEL), F32),
                   jax.ShapeDtypeStruct(state.shape, F32),
                   jax.ShapeDtypeStruct((n, 1, D_MODEL), F32)),
        scratch_shapes=[pltpu.VMEM((2, MLP_SUBTILE, D_MODEL), BF16),
                        pltpu.VMEM((MLP_SUBTILE, D_MODEL), F32),
                        pltpu.VMEM((n, RET_WIDTH), F32),
                        pltpu.VMEM((n, D_MODEL), BF16)],
        compiler_params=pltpu.CompilerParams(
            dimension_semantics=("arbitrary",), vmem_limit_bytes=VMEM_LIMIT_BYTES),
        name="mlp_and_sample",
    )(h, norm_mlp, w_up, w_down, norm_final, q, kt, v, state, xs, yconv, inner, gate, w_out)


def kernel(x_prompt, x_sample, cache_conv, state_ret, meta_tokens, norm_mix, w_in, conv_w, w_out,
           norm_mlp, w_up, w_down, norm_final):
    bsz, seq, _ = x_prompt.shape
    half = jnp.arange(0, HEAD_DIM, 2, dtype=F32) / HEAD_DIM
    inv_half = 1.0 / (ROPE_BASE ** half)
    inv_freq = jnp.concatenate([inv_half, inv_half])[None, :]

    nmix = norm_mix[0][None, :]
    nmlp = norm_mlp[0][None, :]
    nfin = norm_final[None, :]

    (w_in_b, w_out_b, cos, sin, dec, smeta, cmeta,
     q, kt, v, inner, gate, yconv, cnew) = _prologue_call(
        meta_tokens, x_sample, cache_conv[0], nmix, w_in[0], w_out[0], conv_w, inv_freq, seq)
    h1, conv_p, ret_p, w_up_b, w_down_b = _mixer_call(
        x_prompt, nmix, w_in_b, conv_w, w_out_b, cos, sin, dec, smeta, cmeta, w_up[0], w_down[0])
    y_prompt, ret_s, y_sample = _mlp_call(
        h1.reshape(bsz * seq, D_MODEL), nmlp, w_up_b, w_down_b, nfin,
        q, kt, v, state_ret[0], x_sample, yconv, inner, gate, w_out_b)
    y_prompt = y_prompt.reshape(bsz, seq, D_MODEL)

    return (y_prompt, y_sample, conv_p, ret_p, cnew[None], ret_s[None])
```

```python
import functools

import numpy as np
import jax
import jax.numpy as jnp
from jax import lax
from jax.experimental import pallas as pl
from jax.experimental.pallas import tpu as pltpu

D_MODEL = 1024
N_META = 16
CONV_CH = 512
CONV_WIDTH = 3
RET_HEADS = 4
HEAD_DIM = 128
RET_WIDTH = RET_HEADS * HEAD_DIM
CHUNK = 128
D_FF = 4 * D_MODEL
EPS = 1e-6
ROPE_BASE = 10000.0
PAST_LEN = 16384
K_SCALE = HEAD_DIM ** -0.5

OFF_U, OFF_C, OFF_B, OFF_Q, OFF_K, OFF_V, OFF_G = 0, 512, 1024, 1536, 2048, 2560, 3072
IN_TOTAL = 3584

GAMMA = tuple(1.0 - 2.0 ** (-5.0 - h) for h in range(RET_HEADS))
LOG_GAMMA = tuple(float(np.log(g)) for g in GAMMA)

TOKEN_TILE = 1024
MIX_SUBTILE = 512
PROJ_BLOCK = 512
OUT_BLOCK = 256
ROPE_PIECE = 256
MLP_TILE = 1024
MLP_SUBTILE = 512
SAMPLE_BLOCK = 8
VMEM_LIMIT_BYTES = 60 * 1024 * 1024

F32 = jnp.float32
BF16 = jnp.bfloat16


def _rmsnorm(x, gain):
    return x * lax.rsqrt(jnp.mean(x * x, axis=-1, keepdims=True) + EPS) * gain


def _dot(a, b):
    return jnp.dot(a, b, preferred_element_type=F32)


def _dot_nt(a, b):
    return lax.dot_general(a, b, (((1,), (1,)), ((), ())), preferred_element_type=F32)


def _dot_tn(a, b):
    return lax.dot_general(a, b, (((0,), (0,)), ((), ())), preferred_element_type=F32)


def _rope_tables(pos, inv_freq):
    ang = pos * inv_freq
    lane = lax.broadcasted_iota(jnp.int32, ang.shape, 1)
    sin = jnp.sin(ang)
    return jnp.cos(ang), jnp.where(lane < HEAD_DIM // 2, -sin, sin)


def _rope(x, cos, sin):
    return x * cos + pltpu.roll(x, HEAD_DIM // 2, 1) * sin


def _decay_tables(dec_ref):
    row = lax.broadcasted_iota(jnp.int32, (CHUNK, CHUNK), 0).astype(F32)
    col = lax.broadcasted_iota(jnp.int32, (CHUNK, CHUNK), 1).astype(F32)
    diff = row - col
    for h in range(RET_HEADS):
        lg = LOG_GAMMA[h]
        dec_ref[h] = jnp.where(diff >= 0, jnp.exp(lg * jnp.maximum(diff, 0.0)), 0.0)
        dec_ref[RET_HEADS + h] = jnp.exp((row + 1.0) * lg)
        dec_ref[2 * RET_HEADS + h] = jnp.exp((CHUNK - 1.0 - row) * lg)


def _full(shape):
    return pl.BlockSpec(shape, lambda *_: (0,) * len(shape))


def _resident(shape):
    return pl.BlockSpec(shape, lambda *_: (0,) * len(shape), pipeline_mode=pl.Buffered(1))


def _z_slice(zbuf, rows, off, width):
    blk, inner = divmod(off, PROJ_BLOCK)
    assert inner + width <= PROJ_BLOCK
    return zbuf[blk, rows, inner:inner + width]


def _meta_tail(zbuf, invf_ref, smeta_ref, cmeta_ref):
    rows = slice(0, CHUNK)
    cu = _z_slice(zbuf, rows, OFF_C, CONV_CH) * _z_slice(zbuf, rows, OFF_U, CONV_CH)
    cmeta_ref[...] = cu[N_META - 2:N_META, :]
    row = lax.broadcasted_iota(jnp.int32, (CHUNK, HEAD_DIM), 0).astype(F32)
    cos, sin = _rope_tables(row, invf_ref[...])
    for h in range(RET_HEADS):
        kz = _z_slice(zbuf, rows, OFF_K + h * HEAD_DIM, HEAD_DIM)
        vz = _z_slice(zbuf, rows, OFF_V + h * HEAD_DIM, HEAD_DIM)
        k = _rope(kz, cos, sin) * K_SCALE
        kdec = jnp.where(row < N_META, jnp.exp((N_META - 1.0 - row) * LOG_GAMMA[h]), 0.0)
        smeta_ref[h] = _dot_tn((k * kdec).astype(BF16), vz.astype(BF16))


def _sample_proj_tail(zbuf, rows, cache_ref, cw_ref, invf_ref,
                      q_ref, kt_ref, v_ref, inner_ref, gate_ref, yconv_ref, cnew_ref):
    prev0 = cache_ref[:, 0, :]
    prev1 = cache_ref[:, 1, :]
    cu = _z_slice(zbuf, rows, OFF_C, CONV_CH) * _z_slice(zbuf, rows, OFF_U, CONV_CH)
    conv = cw_ref[0, 0:1, :] * prev0 + cw_ref[0, 1:2, :] * prev1 + cw_ref[0, 2:3, :] * cu
    yconv_ref[...] = _z_slice(zbuf, rows, OFF_B, CONV_CH) * conv
    cnew_ref[:, 0, :] = prev1
    cnew_ref[:, 1, :] = cu
    pos = jnp.full((1, HEAD_DIM), float(PAST_LEN), F32)
    cos, sin = _rope_tables(pos, invf_ref[...])
    for h in range(RET_HEADS):
        cols = slice(h * HEAD_DIM, (h + 1) * HEAD_DIM)
        q = _rope(_z_slice(zbuf, rows, OFF_Q + h * HEAD_DIM, HEAD_DIM), cos, sin)
        k = _rope(_z_slice(zbuf, rows, OFF_K + h * HEAD_DIM, HEAD_DIM), cos, sin) * K_SCALE
        v = _z_slice(zbuf, rows, OFF_V + h * HEAD_DIM, HEAD_DIM)
        g = _z_slice(zbuf, rows, OFF_G + h * HEAD_DIM, HEAD_DIM)
        q_ref[:, cols] = q
        kt_ref[h] = k.T
        v_ref[:, cols] = v
        inner_ref[:, cols] = jnp.sum(q * k, axis=-1, keepdims=True) * v
        gate_ref[:, cols] = g * jax.nn.sigmoid(g)


def _prologue_kernel(meta_ref, xs_ref, cache_ref, nm_ref, win_ref, wout_ref, cw_ref, invf_ref,
                     winb_ref, woutb_ref, cos_ref, sin_ref, dec_ref, smeta_ref, cmeta_ref,
                     q_ref, kt_ref, v_ref, inner_ref, gate_ref, yconv_ref, cnew_ref,
                     xpad, hn_sc, zbuf, rope_sc):
    j = pl.program_id(0)
    n_steps = pl.num_programs(0)
    n_dec = xs_ref.shape[0]
    n_pieces = cos_ref.shape[0] // ROPE_PIECE

    def rope_piece(p):
        row0 = pl.multiple_of(p * ROPE_PIECE, ROPE_PIECE)
        base = jnp.asarray(N_META + p * ROPE_PIECE).astype(F32) * invf_ref[...]
        cb, sb = jnp.cos(base), jnp.sin(base)
        co, so = rope_sc[0], rope_sc[1]
        lane = lax.broadcasted_iota(jnp.int32, (ROPE_PIECE, HEAD_DIM), 1)
        sin = sb * co + cb * so
        cos_ref[pl.ds(row0, ROPE_PIECE), :] = cb * co - sb * so
        sin_ref[pl.ds(row0, ROPE_PIECE), :] = jnp.where(lane < HEAD_DIM // 2, -sin, sin)

    @pl.when(j == 0)
    def _():
        xpad[...] = jnp.zeros_like(xpad)
        xpad[0:N_META, :] = meta_ref[...]
        hn_sc[0:CHUNK, :] = _rmsnorm(xpad[...], nm_ref[...]).astype(BF16)
        hn_sc[CHUNK:CHUNK + n_dec, :] = _rmsnorm(xs_ref[:, 0, :], nm_ref[...]).astype(BF16)
        woutb_ref[...] = wout_ref[...].astype(BF16)
        _decay_tables(dec_ref)
        offs = lax.broadcasted_iota(jnp.int32, (ROPE_PIECE, HEAD_DIM), 0).astype(F32)
        ang = offs * invf_ref[...]
        rope_sc[0] = jnp.cos(ang)
        rope_sc[1] = jnp.sin(ang)

    w_blk = win_ref[...].astype(BF16)
    winb_ref[...] = w_blk
    zbuf[j] = _dot(hn_sc[...], w_blk)
    rope_piece(j)

    @pl.when(j == n_steps - 1)
    def _():
        for p in range(IN_TOTAL // PROJ_BLOCK, n_pieces):
            rope_piece(p)
        _meta_tail(zbuf, invf_ref, smeta_ref, cmeta_ref)
        _sample_proj_tail(zbuf, slice(CHUNK, CHUNK + n_dec), cache_ref, cw_ref, invf_ref,
                          q_ref, kt_ref, v_ref, inner_ref, gate_ref, yconv_ref, cnew_ref)


def _prologue_call(meta, xs, cache, norm_mix, w_in, w_out, conv_w, inv_freq, seq):
    n = xs.shape[0]
    n_blk = IN_TOTAL // PROJ_BLOCK
    assert seq % ROPE_PIECE == 0 and seq // ROPE_PIECE >= n_blk
    wide = jax.ShapeDtypeStruct((n, RET_WIDTH), F32)
    return pl.pallas_call(
        _prologue_kernel,
        grid=(n_blk,),
        in_specs=[
            _full((N_META, D_MODEL)),
            _full((n, 1, D_MODEL)),
            _full((n, CONV_WIDTH - 1, CONV_CH)),
            _full((1, D_MODEL)),
            pl.BlockSpec((D_MODEL, PROJ_BLOCK), lambda j: (0, j)),
            _full((D_MODEL, D_MODEL)),
            _full((1, CONV_WIDTH, CONV_CH)),
            _full((1, HEAD_DIM)),
        ],
        out_specs=[
            pl.BlockSpec((D_MODEL, PROJ_BLOCK), lambda j: (0, j)),
            _full((D_MODEL, D_MODEL)),
            _full((seq, HEAD_DIM)),
            _full((seq, HEAD_DIM)),
            _full((3 * RET_HEADS, CHUNK, CHUNK)),
            _full((RET_HEADS, HEAD_DIM, HEAD_DIM)),
            _full((CONV_WIDTH - 1, CONV_CH)),
            _full((n, RET_WIDTH)),
            _full((RET_HEADS, HEAD_DIM, n)),
            _full((n, RET_WIDTH)),
            _full((n, RET_WIDTH)),
            _full((n, RET_WIDTH)),
            _full((n, CONV_CH)),
            _full((n, CONV_WIDTH - 1, CONV_CH)),
        ],
        out_shape=(jax.ShapeDtypeStruct((D_MODEL, IN_TOTAL), BF16),
                   jax.ShapeDtypeStruct((D_MODEL, D_MODEL), BF16),
                   jax.ShapeDtypeStruct((seq, HEAD_DIM), F32),
                   jax.ShapeDtypeStruct((seq, HEAD_DIM), F32),
                   jax.ShapeDtypeStruct((3 * RET_HEADS, CHUNK, CHUNK), F32),
                   jax.ShapeDtypeStruct((RET_HEADS, HEAD_DIM, HEAD_DIM), F32),
                   jax.ShapeDtypeStruct((CONV_WIDTH - 1, CONV_CH), F32),
                   wide,
                   jax.ShapeDtypeStruct((RET_HEADS, HEAD_DIM, n), F32),
                   wide, wide, wide,
                   jax.ShapeDtypeStruct((n, CONV_CH), F32),
                   jax.ShapeDtypeStruct((n, CONV_WIDTH - 1, CONV_CH), F32)),
        scratch_shapes=[pltpu.VMEM((CHUNK, D_MODEL), F32),
                        pltpu.VMEM((CHUNK + n, D_MODEL), BF16),
                        pltpu.VMEM((n_blk, CHUNK + n, PROJ_BLOCK), F32),
                        pltpu.VMEM((2, ROPE_PIECE, HEAD_DIM), F32)],
        compiler_params=pltpu.CompilerParams(
            dimension_semantics=("arbitrary",), vmem_limit_bytes=VMEM_LIMIT_BYTES),
        name="prologue",
    )(meta, xs, cache, norm_mix, w_in, w_out, conv_w, inv_freq)


def _mixer_kernel(x_ref, nm_ref, win_ref, cw_ref, wout_ref, cos_ref, sin_ref, dec_ref,
                  smeta_ref, cmeta_ref, wup_ref, wdown_ref,
                  h1_ref, cstate_ref, sstate_ref, wup_bf_ref, wdown_bf_ref,
                  zbuf, cbuf, ymix, hn_sc):
    t = pl.program_id(1)
    tm = TOKEN_TILE

    wup_bf_ref[...] = wup_ref[...].astype(BF16)
    wdown_bf_ref[...] = wdown_ref[...].astype(BF16)

    @pl.when(t == 0)
    def _():
        sstate_ref[0, 0] = smeta_ref[...]
        cbuf[6:8, :] = cmeta_ref[...]

    n_sub = tm // MIX_SUBTILE
    chunks_per_sub = MIX_SUBTILE // CHUNK

    def sub_rows(s):
        return slice(s * MIX_SUBTILE, (s + 1) * MIX_SUBTILE)

    def stage_rows(s):
        hn_sc[s % 2] = _rmsnorm(x_ref[0, sub_rows(s), :], nm_ref[...]).astype(BF16)

    def project(s, p):
        cols = slice(p * PROJ_BLOCK, (p + 1) * PROJ_BLOCK)
        zbuf[s % 2, :, cols] = _dot(hn_sc[s % 2], win_ref[:, cols])

    def out_project(s, p):
        cols = slice(p * OUT_BLOCK, (p + 1) * OUT_BLOCK)
        h1_ref[0, sub_rows(s), cols] = (x_ref[0, sub_rows(s), cols]
                                        + _dot(ymix[s % 2], wout_ref[:, cols]))

    def conv_chunk(s, c):
        z = zbuf.at[s % 2]
        rows = slice(c * CHUNK, (c + 1) * CHUNK)
        g0 = (s * chunks_per_sub + c) * CHUNK
        cu = z[rows, OFF_C:OFF_C + CONV_CH] * z[rows, OFF_U:OFF_U + CONV_CH]
        cbuf[8 + g0:8 + g0 + CHUNK, :] = cu
        conv = (cw_ref[0, 0:1, :] * cbuf[6 + g0:6 + g0 + CHUNK, :]
                + cw_ref[0, 1:2, :] * cbuf[7 + g0:7 + g0 + CHUNK, :]
                + cw_ref[0, 2:3, :] * cu)
        ymix[s % 2, rows, 0:CONV_CH] = (z[rows, OFF_B:OFF_B + CONV_CH] * conv).astype(BF16)

    saved = {}

    def scores_and_state(s, c):
        z_cur = zbuf.at[s % 2]
        rows = slice(c * CHUNK, (c + 1) * CHUNK)
        g0 = (s * chunks_per_sub + c) * CHUNK
        cos = cos_ref[g0:g0 + CHUNK, :]
        sin = sin_ref[g0:g0 + CHUNK, :]
        qbs, kbs, vbs, states = [], [], [], []
        for h in range(RET_HEADS):
            qz = z_cur[rows, OFF_Q + h * HEAD_DIM:OFF_Q + (h + 1) * HEAD_DIM]
            kz = z_cur[rows, OFF_K + h * HEAD_DIM:OFF_K + (h + 1) * HEAD_DIM]
            vb = z_cur[rows, OFF_V + h * HEAD_DIM:OFF_V + (h + 1) * HEAD_DIM].astype(BF16)
            k = _rope(kz, cos, sin) * K_SCALE
            state = sstate_ref[0, 0, h]
            kd = (k * dec_ref[2 * RET_HEADS + h]).astype(BF16)
            sstate_ref[0, 0, h] = state * (GAMMA[h] ** CHUNK) + _dot_tn(kd, vb)
            qbs.append(_rope(qz, cos, sin).astype(BF16))
            kbs.append(jnp.concatenate([k.T.astype(BF16), state.astype(BF16)], axis=1))
            vbs.append(vb)
        both = [_dot(qbs[h], kbs[h]) for h in range(RET_HEADS)]
        scores = [b[:, 0:CHUNK] for b in both]
        cross = [b[:, CHUNK:CHUNK + HEAD_DIM] for b in both]
        saved[(s, c)] = (scores, cross, vbs)

    def outputs(s, c):
        z_cur = zbuf.at[s % 2]
        rows = slice(c * CHUNK, (c + 1) * CHUNK)
        scores, cross, vbs = saved.pop((s, c))
        for h in range(RET_HEADS):
            gz = z_cur[rows, OFF_G + h * HEAD_DIM:OFF_G + (h + 1) * HEAD_DIM]
            p = (scores[h] * dec_ref[h]).astype(BF16)
            o = _dot(p, vbs[h]) + cross[h] * dec_ref[RET_HEADS + h]
            on = o * lax.rsqrt(jnp.mean(o * o, axis=-1, keepdims=True) + EPS)
            ymix[s % 2, rows, CONV_CH + h * HEAD_DIM:CONV_CH + (h + 1) * HEAD_DIM] = (
                on * (gz * jax.nn.sigmoid(gz))).astype(BF16)

    stage_rows(0)
    for p in range(IN_TOTAL // PROJ_BLOCK):
        project(0, p)
    for s in range(n_sub):
        fillers = []
        if s > 0:
            fillers += [functools.partial(out_project, s - 1, p)
                        for p in range(D_MODEL // OUT_BLOCK)]
        if s + 1 < n_sub:
            stage_rows(s + 1)
            fillers += [functools.partial(project, s + 1, p)
                        for p in range(IN_TOTAL // PROJ_BLOCK)]
        steps = []
        for c in range(chunks_per_sub):
            steps.append(functools.partial(scores_and_state, s, c))
            if c > 0:
                steps.append(functools.partial(outputs, s, c - 1))
        steps.append(functools.partial(outputs, s, chunks_per_sub - 1))
        n_f, n_s = len(fillers), len(steps)
        placed = 0
        for i, step in enumerate(steps):
            while placed < n_f and placed * n_s <= i * n_f:
                fillers[placed]()
                placed += 1
            step()
            if i % 2 == 0:
                conv_chunk(s, i // 2)
        while placed < n_f:
            fillers[placed]()
            placed += 1
    for p in range(D_MODEL // OUT_BLOCK):
        out_project(n_sub - 1, p)

    tail = cbuf[8 + tm - 2:8 + tm, :]
    cbuf[6:8, :] = tail
    cstate_ref[0, 0] = tail


def _mixer_call(x, norm_mix, w_in, conv_w, w_out, cos, sin, dec, smeta, cmeta, w_up, w_down):
    bsz, seq, _ = x.shape
    tm = TOKEN_TILE
    n_t = seq // tm
    grid = (bsz, n_t)
    ff_slice = D_FF // (bsz * n_t)
    return pl.pallas_call(
        _mixer_kernel,
        grid=grid,
        in_specs=[
            pl.BlockSpec((1, tm, D_MODEL), lambda b, t: (b, t, 0)),
            _full((1, D_MODEL)),
            _resident((D_MODEL, IN_TOTAL)),
            _full((1, CONV_WIDTH, CONV_CH)),
            _resident((D_MODEL, D_MODEL)),
            pl.BlockSpec((tm, HEAD_DIM), lambda b, t: (t, 0)),
            pl.BlockSpec((tm, HEAD_DIM), lambda b, t: (t, 0)),
            _full((3 * RET_HEADS, CHUNK, CHUNK)),
            _full((RET_HEADS, HEAD_DIM, HEAD_DIM)),
            _full((CONV_WIDTH - 1, CONV_CH)),
            pl.BlockSpec((D_MODEL, ff_slice), lambda b, t: (0, b * n_t + t)),
            pl.BlockSpec((ff_slice, D_MODEL), lambda b, t: (b * n_t + t, 0)),
        ],
        out_specs=[
            pl.BlockSpec((1, tm, D_MODEL), lambda b, t: (b, t, 0)),
            pl.BlockSpec((1, 1, CONV_WIDTH - 1, CONV_CH), lambda b, t: (0, b, 0, 0)),
            pl.BlockSpec((1, 1, RET_HEADS, HEAD_DIM, HEAD_DIM), lambda b, t: (0, b, 0, 0, 0)),
            pl.BlockSpec((D_MODEL, ff_slice), lambda b, t: (0, b * n_t + t)),
            pl.BlockSpec((ff_slice, D_MODEL), lambda b, t: (b * n_t + t, 0)),
        ],
        out_shape=(
            jax.ShapeDtypeStruct((bsz, seq, D_MODEL), F32),
            jax.ShapeDtypeStruct((1, bsz, CONV_WIDTH - 1, CONV_CH), F32),
            jax.ShapeDtypeStruct((1, bsz, RET_HEADS, HEAD_DIM, HEAD_DIM), F32),
            jax.ShapeDtypeStruct((D_MODEL, D_FF), BF16),
            jax.ShapeDtypeStruct((D_FF, D_MODEL), BF16),
        ),
        scratch_shapes=[
            pltpu.VMEM((2, MIX_SUBTILE, IN_TOTAL), F32),
            pltpu.VMEM((tm + 8, CONV_CH), F32),
            pltpu.VMEM((2, MIX_SUBTILE, D_MODEL), BF16),
            pltpu.VMEM((2, MIX_SUBTILE, D_MODEL), BF16),
        ],
        compiler_params=pltpu.CompilerParams(
            dimension_semantics=("arbitrary", "arbitrary"), vmem_limit_bytes=VMEM_LIMIT_BYTES),
        name="prompt_mixer",
    )(x, norm_mix, w_in, conv_w, w_out, cos, sin, dec, smeta, cmeta, w_up, w_down)


def _mlp_body(h, nm, wup_ref, wdown_ref, nf):
    hn = _rmsnorm(h, nm).astype(BF16)
    acc = h
    ff_block = 1024
    for c in range(D_FF // ff_block):
        up = _dot(hn, wup_ref[:, c * ff_block:(c + 1) * ff_block])
        act = jnp.square(jnp.maximum(up, 0.0)).astype(BF16)
        acc = acc + _dot(act, wdown_ref[c * ff_block:(c + 1) * ff_block, :])
    return _rmsnorm(acc, nf)


def _prompt_mlp_tile(h_ref, nm_ref, wup_ref, wdown_ref, nf_ref, y_ref, hn_sc, acc_sc):
    n_sub = MLP_TILE // MLP_SUBTILE
    ff_block = 1024
    n_ff = D_FF // ff_block

    def rows(s):
        return slice(s * MLP_SUBTILE, (s + 1) * MLP_SUBTILE)

    hn_sc[0] = _rmsnorm(h_ref[rows(0), :], nm_ref[...]).astype(BF16)
    for s in range(n_sub):
        hn = hn_sc[s % 2]
        acc = h_ref[rows(s), :]
        for c in range(n_ff):
            up = _dot(hn, wup_ref[:, c * ff_block:(c + 1) * ff_block])
            act = jnp.square(jnp.maximum(up, 0.0)).astype(BF16)
            if s + 1 == n_sub and c + 1 == n_ff:
                half = MLP_SUBTILE // 2
                for r in range(2):
                    part = slice(r * half, (r + 1) * half)
                    out = acc[part] + _dot(act[part], wdown_ref[c * ff_block:(c + 1) * ff_block, :])
                    y_ref[s * MLP_SUBTILE + r * half:s * MLP_SUBTILE + (r + 1) * half, :] = (
                        _rmsnorm(out, nf_ref[...]))
                break
            acc = acc + _dot(act, wdown_ref[c * ff_block:(c + 1) * ff_block, :])
            if c == 0 and s > 0:
                y_ref[rows(s - 1), :] = _rmsnorm(acc_sc[...], nf_ref[...])
            if c == 1 and s + 1 < n_sub:
                hn_sc[(s + 1) % 2] = _rmsnorm(h_ref[rows(s + 1), :], nm_ref[...]).astype(BF16)
        if s + 1 < n_sub:
            acc_sc[...] = acc


def _sample_state_block(step, heads, q_ref, kt_ref, v_ref, st_ref, new_ref, cross_sc):
    assert SAMPLE_BLOCK == 8
    lane = lax.broadcasted_iota(jnp.int32, (HEAD_DIM, HEAD_DIM), 1)
    sub = lax.broadcasted_iota(jnp.int32, (SAMPLE_BLOCK, HEAD_DIM), 0)
    row0 = pl.multiple_of(step * SAMPLE_BLOCK, SAMPLE_BLOCK)
    for h in heads:
        cols = slice(h * HEAD_DIM, (h + 1) * HEAD_DIM)
        vb = v_ref[:, cols].astype(BF16)
        q_blk = q_ref[:, cols]
        q_diag = jnp.concatenate(
            [jnp.where(sub == i, q_blk, 0.0) for i in range(SAMPLE_BLOCK)], axis=1).astype(BF16)
        s_stack = st_ref[:, h].reshape(SAMPLE_BLOCK * HEAD_DIM, HEAD_DIM).astype(BF16)
        cross_sc[pl.ds(row0, SAMPLE_BLOCK), cols] = _dot(q_diag, s_stack) * GAMMA[h]
        for i in range(SAMPLE_BLOCK):
            b = step * SAMPLE_BLOCK + i
            kt_b = jnp.where(lane == b, kt_ref[h], 0.0).astype(BF16)
            new_ref[i, h] = st_ref[i, h] * GAMMA[h] + _dot(kt_b, vb)


def _sample_tail(x_ref, yconv_ref, inner_ref, gate_ref, wout_ref, nmlp_ref, wup_ref, wdown_ref,
                 nf_ref, cross_sc, ymix, y_ref):
    ymix[:, 0:CONV_CH] = yconv_ref[...].astype(BF16)
    for h in range(RET_HEADS):
        cols = slice(h * HEAD_DIM, (h + 1) * HEAD_DIM)
        o = inner_ref[:, cols] + cross_sc[:, cols]
        on = o * lax.rsqrt(jnp.mean(o * o, axis=-1, keepdims=True) + EPS)
        ymix[:, CONV_CH + h * HEAD_DIM:CONV_CH + (h + 1) * HEAD_DIM] = (
            on * gate_ref[:, cols]).astype(BF16)
    h1 = x_ref[:, 0, :] + _dot(ymix[...], wout_ref[...])
    y_ref[:, 0, :] = _mlp_body(h1, nmlp_ref[...], wup_ref, wdown_ref, nf_ref[...])


def _mlp_kernel(h_ref, nm_ref, wup_ref, wdown_ref, nf_ref,
                q_ref, kt_ref, v_ref, st_ref, xs_ref, yconv_ref, inner_ref, gate_ref, wout_ref,
                y_ref, new_ref, ys_ref,
                hn_sc, acc_sc, cross_sc, ymix_sc):
    step = pl.program_id(0)
    n_tiles = pl.num_programs(0) - 1

    @pl.when(step < n_tiles)
    def _():
        _sample_state_block(step, (0, 1), q_ref, kt_ref, v_ref, st_ref, new_ref, cross_sc)
        _prompt_mlp_tile(h_ref, nm_ref, wup_ref, wdown_ref, nf_ref, y_ref, hn_sc, acc_sc)
        _sample_state_block(step, (2, 3), q_ref, kt_ref, v_ref, st_ref, new_ref, cross_sc)

    @pl.when(step == n_tiles)
    def _():
        _sample_tail(xs_ref, yconv_ref, inner_ref, gate_ref, wout_ref, nm_ref, wup_ref, wdown_ref,
                     nf_ref, cross_sc, ymix_sc, ys_ref)


def _mlp_call(h, norm_mlp, w_up, w_down, norm_final, q, kt, v, state, xs, yconv, inner, gate, w_out):
    rows = h.shape[0]
    n = xs.shape[0]
    tm = MLP_TILE
    n_tiles = rows // tm
    bb = SAMPLE_BLOCK
    assert n_tiles * bb == n
    last = n_tiles - 1

    def tile(i):
        return jnp.minimum(i, last)

    return pl.pallas_call(
        _mlp_kernel,
        grid=(n_tiles + 1,),
        in_specs=[
            pl.BlockSpec((tm, D_MODEL), lambda i: (tile(i), 0)),
            _full((1, D_MODEL)),
            _resident((D_MODEL, D_FF)),
            _resident((D_FF, D_MODEL)),
            _full((1, D_MODEL)),
            pl.BlockSpec((bb, RET_WIDTH), lambda i: (tile(i), 0)),
            _full((RET_HEADS, HEAD_DIM, n)),
            _full((n, RET_WIDTH)),
            pl.BlockSpec((bb, RET_HEADS, HEAD_DIM, HEAD_DIM), lambda i: (tile(i), 0, 0, 0)),
            _full((n, 1, D_MODEL)),
            _full((n, CONV_CH)),
            _full((n, RET_WIDTH)),
            _full((n, RET_WIDTH)),
            _resident((D_MODEL, D_MODEL)),
        ],
        out_specs=[
            pl.BlockSpec((tm, D_MODEL), lambda i: (tile(i), 0)),
            pl.BlockSpec((bb, RET_HEADS, HEAD_DIM, HEAD_DIM), lambda i: (tile(i), 0, 0, 0)),
            _full((n, 1, D_MODEL)),
        ],
        out_shape=(jax.ShapeDtypeStruct((rows, D_MODEL), F32),
                   jax.ShapeDtypeStruct(state.shape, F32),
                   jax.ShapeDtypeStruct((n, 1, D_MODEL), F32)),
        scratch_shapes=[pltpu.VMEM((2, MLP_SUBTILE, D_MODEL), BF16),
                        pltpu.VMEM((MLP_SUBTILE, D_MODEL), F32),
                        pltpu.VMEM((n, RET_WIDTH), F32),
                        pltpu.VMEM((n, D_MODEL), BF16)],
        compiler_params=pltpu.CompilerParams(
            dimension_semantics=("arbitrary",), vmem_limit_bytes=VMEM_LIMIT_BYTES),
        name="mlp_and_sample",
    )(h, norm_mlp, w_up, w_down, norm_final, q, kt, v, state, xs, yconv, inner, gate, w_out)


def kernel(x_prompt, x_sample, cache_conv, state_ret, meta_tokens, norm_mix, w_in, conv_w, w_out,
           norm_mlp, w_up, w_down, norm_final):
    bsz, seq, _ = x_prompt.shape
    half = jnp.arange(0, HEAD_DIM, 2, dtype=F32) / HEAD_DIM
    inv_half = 1.0 / (ROPE_BASE ** half)
    inv_freq = jnp.concatenate([inv_half, inv_half])[None, :]

    nmix = norm_mix[0][None, :]
    nmlp = norm_mlp[0][None, :]
    nfin = norm_final[None, :]

    (w_in_b, w_out_b, cos, sin, dec, smeta, cmeta,
     q, kt, v, inner, gate, yconv, cnew) = _prologue_call(
        meta_tokens, x_sample, cache_conv[0], nmix, w_in[0], w_out[0], conv_w, inv_freq, seq)
    h1, conv_p, ret_p, w_up_b, w_down_b = _mixer_call(
        x_prompt, nmix, w_in_b, conv_w, w_out_b, cos, sin, dec, smeta, cmeta, w_up[0], w_down[0])
    y_prompt, ret_s, y_sample = _mlp_call(
        h1.reshape(bsz * seq, D_MODEL), nmlp, w_up_b, w_down_b, nfin,
        q, kt, v, state_ret[0], x_sample, yconv, inner, gate, w_out_b)
    y_prompt = y_prompt.reshape(bsz, seq, D_MODEL)

    return (y_prompt, y_sample, conv_p, ret_p, cnew[None], ret_s[None])
```

```python
import functools

import numpy as np
import jax
import jax.numpy as jnp
from jax import lax
from jax.experimental import pallas as pl
from jax.experimental.pallas import tpu as pltpu

D_MODEL = 1024
N_META = 16
CONV_CH = 512
CONV_WIDTH = 3
RET_HEADS = 4
HEAD_DIM = 128
RET_WIDTH = RET_HEADS * HEAD_DIM
CHUNK = 128
D_FF = 4 * D_MODEL
EPS = 1e-6
ROPE_BASE = 10000.0
PAST_LEN = 16384
K_SCALE = HEAD_DIM ** -0.5

OFF_U, OFF_C, OFF_B, OFF_Q, OFF_K, OFF_V, OFF_G = 0, 512, 1024, 1536, 2048, 2560, 3072
IN_TOTAL = 3584

GAMMA = tuple(1.0 - 2.0 ** (-5.0 - h) for h in range(RET_HEADS))
LOG_GAMMA = tuple(float(np.log(g)) for g in GAMMA)

TOKEN_TILE = 1024
MIX_SUBTILE = 512
PROJ_BLOCK = 512
OUT_BLOCK = 256
ROPE_PIECE = 256
MLP_TILE = 1024
MLP_SUBTILE = 512
SAMPLE_BLOCK = 8
VMEM_LIMIT_BYTES = 60 * 1024 * 1024

F32 = jnp.float32
BF16 = jnp.bfloat16


def _rmsnorm(x, gain):
    return x * lax.rsqrt(jnp.mean(x * x, axis=-1, keepdims=True) + EPS) * gain


def _dot(a, b):
    return jnp.dot(a, b, preferred_element_type=F32)


def _dot_nt(a, b):
    return lax.dot_general(a, b, (((1,), (1,)), ((), ())), preferred_element_type=F32)


def _dot_tn(a, b):
    return lax.dot_general(a, b, (((0,), (0,)), ((), ())), preferred_element_type=F32)


def _rope_tables(pos, inv_freq):
    ang = pos * inv_freq
    lane = lax.broadcasted_iota(jnp.int32, ang.shape, 1)
    sin = jnp.sin(ang)
    return jnp.cos(ang), jnp.where(lane < HEAD_DIM // 2, -sin, sin)


def _rope(x, cos, sin):
    return x * cos + pltpu.roll(x, HEAD_DIM // 2, 1) * sin


def _decay_tables(dec_ref):
    row = lax.broadcasted_iota(jnp.int32, (CHUNK, CHUNK), 0).astype(F32)
    col = lax.broadcasted_iota(jnp.int32, (CHUNK, CHUNK), 1).astype(F32)
    diff = row - col
    for h in range(RET_HEADS):
        lg = LOG_GAMMA[h]
        dec_ref[h] = jnp.where(diff >= 0, jnp.exp(lg * jnp.maximum(diff, 0.0)), 0.0)
        dec_ref[RET_HEADS + h] = jnp.exp((row + 1.0) * lg)
        dec_ref[2 * RET_HEADS + h] = jnp.exp((CHUNK - 1.0 - row) * lg)


def _full(shape):
    return pl.BlockSpec(shape, lambda *_: (0,) * len(shape))


def _resident(shape):
    return pl.BlockSpec(shape, lambda *_: (0,) * len(shape), pipeline_mode=pl.Buffered(1))


def _z_slice(zbuf, rows, off, width):
    blk, inner = divmod(off, PROJ_BLOCK)
    assert inner + width <= PROJ_BLOCK
    return zbuf[blk, rows, inner:inner + width]


def _meta_tail(zbuf, invf_ref, smeta_ref, cmeta_ref):
    rows = slice(0, CHUNK)
    cu = _z_slice(zbuf, rows, OFF_C, CONV_CH) * _z_slice(zbuf, rows, OFF_U, CONV_CH)
    cmeta_ref[...] = cu[N_META - 2:N_META, :]
    row = lax.broadcasted_iota(jnp.int32, (CHUNK, HEAD_DIM), 0).astype(F32)
    cos, sin = _rope_tables(row, invf_ref[...])
    for h in range(RET_HEADS):
        kz = _z_slice(zbuf, rows, OFF_K + h * HEAD_DIM, HEAD_DIM)
        vz = _z_slice(zbuf, rows, OFF_V + h * HEAD_DIM, HEAD_DIM)
        k = _rope(kz, cos, sin) * K_SCALE
        kdec = jnp.where(row < N_META, jnp.exp((N_META - 1.0 - row) * LOG_GAMMA[h]), 0.0)
        smeta_ref[h] = _dot_tn((k * kdec).astype(BF16), vz.astype(BF16))


def _sample_proj_tail(zbuf, rows, cache_ref, cw_ref, invf_ref,
                      q_ref, kt_ref, v_ref, inner_ref, gate_ref, yconv_ref, cnew_ref):
    prev0 = cache_ref[:, 0, :]
    prev1 = cache_ref[:, 1, :]
    cu = _z_slice(zbuf, rows, OFF_C, CONV_CH) * _z_slice(zbuf, rows, OFF_U, CONV_CH)
    conv = cw_ref[0, 0:1, :] * prev0 + cw_ref[0, 1:2, :] * prev1 + cw_ref[0, 2:3, :] * cu
    yconv_ref[...] = _z_slice(zbuf, rows, OFF_B, CONV_CH) * conv
    cnew_ref[:, 0, :] = prev1
    cnew_ref[:, 1, :] = cu
    pos = jnp.full((1, HEAD_DIM), float(PAST_LEN), F32)
    cos, sin = _rope_tables(pos, invf_ref[...])
    for h in range(RET_HEADS):
        cols = slice(h * HEAD_DIM, (h + 1) * HEAD_DIM)
        q = _rope(_z_slice(zbuf, rows, OFF_Q + h * HEAD_DIM, HEAD_DIM), cos, sin)
        k = _rope(_z_slice(zbuf, rows, OFF_K + h * HEAD_DIM, HEAD_DIM), cos, sin) * K_SCALE
        v = _z_slice(zbuf, rows, OFF_V + h * HEAD_DIM, HEAD_DIM)
        g = _z_slice(zbuf, rows, OFF_G + h * HEAD_DIM, HEAD_DIM)
        q_ref[:, cols] = q
        kt_ref[h] = k.T
        v_ref[:, cols] = v
        inner_ref[:, cols] = jnp.sum(q * k, axis=-1, keepdims=True) * v
        gate_ref[:, cols] = g * jax.nn.sigmoid(g)


def _prologue_kernel(meta_ref, xs_ref, cache_ref, nm_ref, win_ref, wout_ref, cw_ref, invf_ref,
                     winb_ref, woutb_ref, cos_ref, sin_ref, dec_ref, smeta_ref, cmeta_ref,
                     q_ref, kt_ref, v_ref, inner_ref, gate_ref, yconv_ref, cnew_ref,
                     xpad, hn_sc, zbuf, rope_sc):
    j = pl.program_id(0)
    n_steps = pl.num_programs(0)
    n_dec = xs_ref.shape[0]
    n_pieces = cos_ref.shape[0] // ROPE_PIECE

    def rope_piece(p):
        row0 = pl.multiple_of(p * ROPE_PIECE, ROPE_PIECE)
        base = jnp.asarray(N_META + p * ROPE_PIECE).astype(F32) * invf_ref[...]
        cb, sb = jnp.cos(base), jnp.sin(base)
        co, so = rope_sc[0], rope_sc[1]
        lane = lax.broadcasted_iota(jnp.int32, (ROPE_PIECE, HEAD_DIM), 1)
        sin = sb * co + cb * so
        cos_ref[pl.ds(row0, ROPE_PIECE), :] = cb * co - sb * so
        sin_ref[pl.ds(row0, ROPE_PIECE), :] = jnp.where(lane < HEAD_DIM // 2, -sin, sin)

    @pl.when(j == 0)
    def _():
        xpad[...] = jnp.zeros_like(xpad)
        xpad[0:N_META, :] = meta_ref[...]
        hn_sc[0:CHUNK, :] = _rmsnorm(xpad[...], nm_ref[...]).astype(BF16)
        hn_sc[CHUNK:CHUNK + n_dec, :] = _rmsnorm(xs_ref[:, 0, :], nm_ref[...]).astype(BF16)
        woutb_ref[...] = wout_ref[...].astype(BF16)
        _decay_tables(dec_ref)
        offs = lax.broadcasted_iota(jnp.int32, (ROPE_PIECE, HEAD_DIM), 0).astype(F32)
        ang = offs * invf_ref[...]
        rope_sc[0] = jnp.cos(ang)
        rope_sc[1] = jnp.sin(ang)

    w_blk = win_ref[...].astype(BF16)
    winb_ref[...] = w_blk
    zbuf[j] = _dot(hn_sc[...], w_blk)
    rope_piece(j)

    @pl.when(j == n_steps - 1)
    def _():
        for p in range(IN_TOTAL // PROJ_BLOCK, n_pieces):
            rope_piece(p)
        _meta_tail(zbuf, invf_ref, smeta_ref, cmeta_ref)
        _sample_proj_tail(zbuf, slice(CHUNK, CHUNK + n_dec), cache_ref, cw_ref, invf_ref,
                          q_ref, kt_ref, v_ref, inner_ref, gate_ref, yconv_ref, cnew_ref)


def _prologue_call(meta, xs, cache, norm_mix, w_in, w_out, conv_w, inv_freq, seq):
    n = xs.shape[0]
    n_blk = IN_TOTAL // PROJ_BLOCK
    assert seq % ROPE_PIECE == 0 and seq // ROPE_PIECE >= n_blk
    wide = jax.ShapeDtypeStruct((n, RET_WIDTH), F32)
    return pl.pallas_call(
        _prologue_kernel,
        grid=(n_blk,),
        in_specs=[
            _full((N_META, D_MODEL)),
            _full((n, 1, D_MODEL)),
            _full((n, CONV_WIDTH - 1, CONV_CH)),
            _full((1, D_MODEL)),
            pl.BlockSpec((D_MODEL, PROJ_BLOCK), lambda j: (0, j)),
            _full((D_MODEL, D_MODEL)),
            _full((1, CONV_WIDTH, CONV_CH)),
            _full((1, HEAD_DIM)),
        ],
        out_specs=[
            pl.BlockSpec((D_MODEL, PROJ_BLOCK), lambda j: (0, j)),
            _full((D_MODEL, D_MODEL)),
            _full((seq, HEAD_DIM)),
            _full((seq, HEAD_DIM)),
            _full((3 * RET_HEADS, CHUNK, CHUNK)),
            _full((RET_HEADS, HEAD_DIM, HEAD_DIM)),
            _full((CONV_WIDTH - 1, CONV_CH)),
            _full((n, RET_WIDTH)),
            _full((RET_HEADS, HEAD_DIM, n)),
            _full((n, RET_WIDTH)),
            _full((n, RET_WIDTH)),
            _full((n, RET_WIDTH)),
            _full((n, CONV_CH)),
            _full((n, CONV_WIDTH - 1, CONV_CH)),
        ],
        out_shape=(jax.ShapeDtypeStruct((D_MODEL, IN_TOTAL), BF16),
                   jax.ShapeDtypeStruct((D_MODEL, D_MODEL), BF16),
                   jax.ShapeDtypeStruct((seq, HEAD_DIM), F32),
                   jax.ShapeDtypeStruct((seq, HEAD_DIM), F32),
                   jax.ShapeDtypeStruct((3 * RET_HEADS, CHUNK, CHUNK), F32),
                   jax.ShapeDtypeStruct((RET_HEADS, HEAD_DIM, HEAD_DIM), F32),
                   jax.ShapeDtypeStruct((CONV_WIDTH - 1, CONV_CH), F32),
                   wide,
                   jax.ShapeDtypeStruct((RET_HEADS, HEAD_DIM, n), F32),
                   wide, wide, wide,
                   jax.ShapeDtypeStruct((n, CONV_CH), F32),
                   jax.ShapeDtypeStruct((n, CONV_WIDTH - 1, CONV_CH), F32)),
        scratch_shapes=[pltpu.VMEM((CHUNK, D_MODEL), F32),
                        pltpu.VMEM((CHUNK + n, D_MODEL), BF16),
                        pltpu.VMEM((n_blk, CHUNK + n, PROJ_BLOCK), F32),
                        pltpu.VMEM((2, ROPE_PIECE, HEAD_DIM), F32)],
        compiler_params=pltpu.CompilerParams(
            dimension_semantics=("arbitrary",), vmem_limit_bytes=VMEM_LIMIT_BYTES),
        name="prologue",
    )(meta, xs, cache, norm_mix, w_in, w_out, conv_w, inv_freq)


def _mixer_kernel(x_ref, nm_ref, win_ref, cw_ref, wout_ref, cos_ref, sin_ref, dec_ref,
                  smeta_ref, cmeta_ref, wup_ref, wdown_ref,
                  h1_ref, cstate_ref, sstate_ref, wup_bf_ref, wdown_bf_ref,
                  zbuf, cbuf, ymix, hn_sc):
    t = pl.program_id(1)
    tm = TOKEN_TILE

    wup_bf_ref[...] = wup_ref[...].astype(BF16)
    wdown_bf_ref[...] = wdown_ref[...].astype(BF16)

    @pl.when(t == 0)
    def _():
        sstate_ref[0, 0] = smeta_ref[...]
        cbuf[6:8, :] = cmeta_ref[...]

    n_sub = tm // MIX_SUBTILE
    chunks_per_sub = MIX_SUBTILE // CHUNK

    def sub_rows(s):
        return slice(s * MIX_SUBTILE, (s + 1) * MIX_SUBTILE)

    def stage_rows(s):
        hn_sc[s % 2] = _rmsnorm(x_ref[0, sub_rows(s), :], nm_ref[...]).astype(BF16)

    def project(s, p):
        cols = slice(p * PROJ_BLOCK, (p + 1) * PROJ_BLOCK)
        zbuf[s % 2, :, cols] = _dot(hn_sc[s % 2], win_ref[:, cols])

    def out_project(s, p):
        cols = slice(p * OUT_BLOCK, (p + 1) * OUT_BLOCK)
        h1_ref[0, sub_rows(s), cols] = (x_ref[0, sub_rows(s), cols]
                                        + _dot(ymix[s % 2], wout_ref[:, cols]))

    def conv_chunk(s, c):
        z = zbuf.at[s % 2]
        rows = slice(c * CHUNK, (c + 1) * CHUNK)
        g0 = (s * chunks_per_sub + c) * CHUNK
        cu = z[rows, OFF_C:OFF_C + CONV_CH] * z[rows, OFF_U:OFF_U + CONV_CH]
        cbuf[8 + g0:8 + g0 + CHUNK, :] = cu
        conv = (cw_ref[0, 0:1, :] * cbuf[6 + g0:6 + g0 + CHUNK, :]
                + cw_ref[0, 1:2, :] * cbuf[7 + g0:7 + g0 + CHUNK, :]
                + cw_ref[0, 2:3, :] * cu)
        ymix[s % 2, rows, 0:CONV_CH] = (z[rows, OFF_B:OFF_B + CONV_CH] * conv).astype(BF16)

    saved = {}

    def scores_and_state(s, c):
        z_cur = zbuf.at[s % 2]
        rows = slice(c * CHUNK, (c + 1) * CHUNK)
        g0 = (s * chunks_per_sub + c) * CHUNK
        cos = cos_ref[g0:g0 + CHUNK, :]
        sin = sin_ref[g0:g0 + CHUNK, :]
        qbs, kbs, vbs, states = [], [], [], []
        for h in range(RET_HEADS):
            qz = z_cur[rows, OFF_Q + h * HEAD_DIM:OFF_Q + (h + 1) * HEAD_DIM]
            kz = z_cur[rows, OFF_K + h * HEAD_DIM:OFF_K + (h + 1) * HEAD_DIM]
            vb = z_cur[rows, OFF_V + h * HEAD_DIM:OFF_V + (h + 1) * HEAD_DIM].astype(BF16)
            k = _rope(kz, cos, sin) * K_SCALE
            state = sstate_ref[0, 0, h]
            kd = (k * dec_ref[2 * RET_HEADS + h]).astype(BF16)
            sstate_ref[0, 0, h] = state * (GAMMA[h] ** CHUNK) + _dot_tn(kd, vb)
            qbs.append(_rope(qz, cos, sin).astype(BF16))
            kbs.append(jnp.concatenate([k.T.astype(BF16), state.astype(BF16)], axis=1))
            vbs.append(vb)
        both = [_dot(qbs[h], kbs[h]) for h in range(RET_HEADS)]
        scores = [b[:, 0:CHUNK] for b in both]
        cross = [b[:, CHUNK:CHUNK + HEAD_DIM] for b in both]
        saved[(s, c)] = (scores, cross, vbs)

    def outputs(s, c):
        z_cur = zbuf.at[s % 2]
        rows = slice(c * CHUNK, (c + 1) * CHUNK)
        scores, cross, vbs = saved.pop((s, c))
        for h in range(RET_HEADS):
            gz = z_cur[rows, OFF_G + h * HEAD_DIM:OFF_G + (h + 1) * HEAD_DIM]
            p = (scores[h] * dec_ref[h]).astype(BF16)
            o = _dot(p, vbs[h]) + cross[h] * dec_ref[RET_HEADS + h]
            on = o * lax.rsqrt(jnp.mean(o * o, axis=-1, keepdims=True) + EPS)
            ymix[s % 2, rows, CONV_CH + h * HEAD_DIM:CONV_CH + (h + 1) * HEAD_DIM] = (
                on * (gz * jax.nn.sigmoid(gz))).astype(BF16)

    stage_rows(0)
    for p in range(IN_TOTAL // PROJ_BLOCK):
        project(0, p)
    for s in range(n_sub):
        fillers = []
        if s > 0:
            fillers += [functools.partial(out_project, s - 1, p)
                        for p in range(D_MODEL // OUT_BLOCK)]
        if s + 1 < n_sub:
            stage_rows(s + 1)
            fillers += [functools.partial(project, s + 1, p)
                        for p in range(IN_TOTAL // PROJ_BLOCK)]
        steps = []
        for c in range(chunks_per_sub):
            steps.append(functools.partial(scores_and_state, s, c))
            if c > 0:
                steps.append(functools.partial(outputs, s, c - 1))
        steps.append(functools.partial(outputs, s, chunks_per_sub - 1))
        n_f, n_s = len(fillers), len(steps)
        placed = 0
        for i, step in enumerate(steps):
            while placed < n_f and placed * n_s <= i * n_f:
                fillers[placed]()
                placed += 1
            step()
            if i % 2 == 0:
                conv_chunk(s, i // 2)
        while placed < n_f:
            fillers[placed]()
            placed += 1
    for p in range(D_MODEL // OUT_BLOCK):
        out_project(n_sub - 1, p)

    tail = cbuf[8 + tm - 2:8 + tm, :]
    cbuf[6:8, :] = tail
    cstate_ref[0, 0] = tail


def _mixer_call(x, norm_mix, w_in, conv_w, w_out, cos, sin, dec, smeta, cmeta, w_up, w_down):
    bsz, seq, _ = x.shape
    tm = TOKEN_TILE
    n_t = seq // tm
    grid = (bsz, n_t)
    ff_slice = D_FF // (bsz * n_t)
    return pl.pallas_call(
        _mixer_kernel,
        grid=grid,
        in_specs=[
            pl.BlockSpec((1, tm, D_MODEL), lambda b, t: (b, t, 0)),
            _full((1, D_MODEL)),
            _resident((D_MODEL, IN_TOTAL)),
            _full((1, CONV_WIDTH, CONV_CH)),
            _resident((D_MODEL, D_MODEL)),
            pl.BlockSpec((tm, HEAD_DIM), lambda b, t: (t, 0)),
            pl.BlockSpec((tm, HEAD_DIM), lambda b, t: (t, 0)),
            _full((3 * RET_HEADS, CHUNK, CHUNK)),
            _full((RET_HEADS, HEAD_DIM, HEAD_DIM)),
            _full((CONV_WIDTH - 1, CONV_CH)),
            pl.BlockSpec((D_MODEL, ff_slice), lambda b, t: (0, b * n_t + t)),
            pl.BlockSpec((ff_slice, D_MODEL), lambda b, t: (b * n_t + t, 0)),
        ],
        out_specs=[
            pl.BlockSpec((1, tm, D_MODEL), lambda b, t: (b, t, 0)),
            pl.BlockSpec((1, 1, CONV_WIDTH - 1, CONV_CH), lambda b, t: (0, b, 0, 0)),
            pl.BlockSpec((1, 1, RET_HEADS, HEAD_DIM, HEAD_DIM), lambda b, t: (0, b, 0, 0, 0)),
            pl.BlockSpec((D_MODEL, ff_slice), lambda b, t: (0, b * n_t + t)),
            pl.BlockSpec((ff_slice, D_MODEL), lambda b, t: (b * n_t + t, 0)),
        ],
        out_shape=(
            jax.ShapeDtypeStruct((bsz, seq, D_MODEL), F32),
            jax.ShapeDtypeStruct((1, bsz, CONV_WIDTH - 1, CONV_CH), F32),
            jax.ShapeDtypeStruct((1, bsz, RET_HEADS, HEAD_DIM, HEAD_DIM), F32),
            jax.ShapeDtypeStruct((D_MODEL, D_FF), BF16),
            jax.ShapeDtypeStruct((D_FF, D_MODEL), BF16),
        ),
        scratch_shapes=[
            pltpu.VMEM((2, MIX_SUBTILE, IN_TOTAL), F32),
            pltpu.VMEM((tm + 8, CONV_CH), F32),
            pltpu.VMEM((2, MIX_SUBTILE, D_MODEL), BF16),
            pltpu.VMEM((2, MIX_SUBTILE, D_MODEL), BF16),
        ],
        compiler_params=pltpu.CompilerParams(
            dimension_semantics=("arbitrary", "arbitrary"), vmem_limit_bytes=VMEM_LIMIT_BYTES),
        name="prompt_mixer",
    )(x, norm_mix, w_in, conv_w, w_out, cos, sin, dec, smeta, cmeta, w_up, w_down)


def _mlp_body(h, nm, wup_ref, wdown_ref, nf):
    hn = _rmsnorm(h, nm).astype(BF16)
    acc = h
    ff_block = 1024
    for c in range(D_FF // ff_block):
        up = _dot(hn, wup_ref[:, c * ff_block:(c + 1) * ff_block])
        act = jnp.square(jnp.maximum(up, 0.0)).astype(BF16)
        acc = acc + _dot(act, wdown_ref[c * ff_block:(c + 1) * ff_block, :])
    return _rmsnorm(acc, nf)


def _prompt_mlp_tile(h_ref, nm_ref, wup_ref, wdown_ref, nf_ref, y_ref, hn_sc, acc_sc):
    n_sub = MLP_TILE // MLP_SUBTILE
    ff_block = 1024
    n_ff = D_FF // ff_block

    def rows(s):
        return slice(s * MLP_SUBTILE, (s + 1) * MLP_SUBTILE)

    hn_sc[0] = _rmsnorm(h_ref[rows(0), :], nm_ref[...]).astype(BF16)
    for s in range(n_sub):
        hn = hn_sc[s % 2]
        acc = h_ref[rows(s), :]
        for c in range(n_ff):
            if s == 0 and c == 0:
                half = MLP_SUBTILE // 2
                up = jnp.concatenate(
                    [_dot(hn[r * half:(r + 1) * half], wup_ref[:, 0:ff_block]) for r in range(2)],
                    axis=0)
            else:
                up = _dot(hn, wup_ref[:, c * ff_block:(c + 1) * ff_block])
            act = jnp.square(jnp.maximum(up, 0.0)).astype(BF16)
            if s + 1 == n_sub and c + 1 == n_ff:
                half = MLP_SUBTILE // 2
                for r in range(2):
                    part = slice(r * half, (r + 1) * half)
                    out = acc[part] + _dot(act[part], wdown_ref[c * ff_block:(c + 1) * ff_block, :])
                    y_ref[s * MLP_SUBTILE + r * half:s * MLP_SUBTILE + (r + 1) * half, :] = (
                        _rmsnorm(out, nf_ref[...]))
                break
            acc = acc + _dot(act, wdown_ref[c * ff_block:(c + 1) * ff_block, :])
            if c == 0 and s > 0:
                y_ref[rows(s - 1), :] = _rmsnorm(acc_sc[...], nf_ref[...])
            if c == 1 and s + 1 < n_sub:
                hn_sc[(s + 1) % 2] = _rmsnorm(h_ref[rows(s + 1), :], nm_ref[...]).astype(BF16)
        if s + 1 < n_sub:
            acc_sc[...] = acc


def _sample_state_block(step, heads, q_ref, kt_ref, v_ref, st_ref, new_ref, cross_sc):
    assert SAMPLE_BLOCK == 8
    lane = lax.broadcasted_iota(jnp.int32, (HEAD_DIM, HEAD_DIM), 1)
    sub = lax.broadcasted_iota(jnp.int32, (SAMPLE_BLOCK, HEAD_DIM), 0)
    row0 = pl.multiple_of(step * SAMPLE_BLOCK, SAMPLE_BLOCK)
    for h in heads:
        cols = slice(h * HEAD_DIM, (h + 1) * HEAD_DIM)
        vb = v_ref[:, cols].astype(BF16)
        q_blk = q_ref[:, cols]
        q_diag = jnp.concatenate(
            [jnp.where(sub == i, q_blk, 0.0) for i in range(SAMPLE_BLOCK)], axis=1).astype(BF16)
        s_stack = st_ref[:, h].reshape(SAMPLE_BLOCK * HEAD_DIM, HEAD_DIM).astype(BF16)
        cross_sc[pl.ds(row0, SAMPLE_BLOCK), cols] = _dot(q_diag, s_stack) * GAMMA[h]
        for i in range(SAMPLE_BLOCK):
            b = step * SAMPLE_BLOCK + i
            kt_b = jnp.where(lane == b, kt_ref[h], 0.0).astype(BF16)
            new_ref[i, h] = st_ref[i, h] * GAMMA[h] + _dot(kt_b, vb)


def _sample_tail(x_ref, yconv_ref, inner_ref, gate_ref, wout_ref, nmlp_ref, wup_ref, wdown_ref,
                 nf_ref, cross_sc, ymix, y_ref):
    ymix[:, 0:CONV_CH] = yconv_ref[...].astype(BF16)
    for h in range(RET_HEADS):
        cols = slice(h * HEAD_DIM, (h + 1) * HEAD_DIM)
        o = inner_ref[:, cols] + cross_sc[:, cols]
        on = o * lax.rsqrt(jnp.mean(o * o, axis=-1, keepdims=True) + EPS)
        ymix[:, CONV_CH + h * HEAD_DIM:CONV_CH + (h + 1) * HEAD_DIM] = (
            on * gate_ref[:, cols]).astype(BF16)
    h1 = x_ref[:, 0, :] + _dot(ymix[...], wout_ref[...])
    y_ref[:, 0, :] = _mlp_body(h1, nmlp_ref[...], wup_ref, wdown_ref, nf_ref[...])


def _mlp_kernel(h_ref, nm_ref, wup_ref, wdown_ref, nf_ref,
                q_ref, kt_ref, v_ref, st_ref, xs_ref, yconv_ref, inner_ref, gate_ref, wout_ref,
                y_ref, new_ref, ys_ref,
                hn_sc, acc_sc, cross_sc, ymix_sc):
    step = pl.program_id(0)
    n_tiles = pl.num_programs(0) - 1

    @pl.when(step < n_tiles)
    def _():
        _sample_state_block(step, (0, 1), q_ref, kt_ref, v_ref, st_ref, new_ref, cross_sc)
        _prompt_mlp_tile(h_ref, nm_ref, wup_ref, wdown_ref, nf_ref, y_ref, hn_sc, acc_sc)
        _sample_state_block(step, (2, 3), q_ref, kt_ref, v_ref, st_ref, new_ref, cross_sc)

    @pl.when(step == n_tiles)
    def _():
        _sample_tail(xs_ref, yconv_ref, inner_ref, gate_ref, wout_ref, nm_ref, wup_ref, wdown_ref,
                     nf_ref, cross_sc, ymix_sc, ys_ref)


def _mlp_call(h, norm_mlp, w_up, w_down, norm_final, q, kt, v, state, xs, yconv, inner, gate, w_out):
    rows = h.shape[0]
    n = xs.shape[0]
    tm = MLP_TILE
    n_tiles = rows // tm
    bb = SAMPLE_BLOCK
    assert n_tiles * bb == n
    last = n_tiles - 1

    def tile(i):
        return jnp.minimum(i, last)

    return pl.pallas_call(
        _mlp_kernel,
        grid=(n_tiles + 1,),
        in_specs=[
            pl.BlockSpec((tm, D_MODEL), lambda i: (tile(i), 0)),
            _full((1, D_MODEL)),
            _resident((D_MODEL, D_FF)),
            _resident((D_FF, D_MODEL)),
            _full((1, D_MODEL)),
            pl.BlockSpec((bb, RET_WIDTH), lambda i: (tile(i), 0)),
            _full((RET_HEADS, HEAD_DIM, n)),
            _full((n, RET_WIDTH)),
            pl.BlockSpec((bb, RET_HEADS, HEAD_DIM, HEAD_DIM), lambda i: (tile(i), 0, 0, 0)),
            _full((n, 1, D_MODEL)),
            _full((n, CONV_CH)),
            _full((n, RET_WIDTH)),
            _full((n, RET_WIDTH)),
            _resident((D_MODEL, D_MODEL)),
        ],
        out_specs=[
            pl.BlockSpec((tm, D_MODEL), lambda i: (tile(i), 0)),
            pl.BlockSpec((bb, RET_HEADS, HEAD_DIM, HEAD_DIM), lambda i: (tile(i), 0, 0, 0)),
            _full((n, 1, D_MODEL)),
        ],
        out_shape=(jax.ShapeDtypeStruct((rows, D_MODEL), F32),
                   jax.ShapeDtypeStruct(state.shape, F32),
                   jax.ShapeDtypeStruct((n, 1, D_MODEL), F32)),
        scratch_shapes=[pltpu.VMEM((2, MLP_SUBTILE, D_MODEL), BF16),
                        pltpu.VMEM((MLP_SUBTILE, D_MODEL), F32),
                        pltpu.VMEM((n, RET_WIDTH), F32),
                        pltpu.VMEM((n, D_MODEL), BF16)],
        compiler_params=pltpu.CompilerParams(
            dimension_semantics=("arbitrary",), vmem_limit_bytes=VMEM_LIMIT_BYTES),
        name="mlp_and_sample",
    )(h, norm_mlp, w_up, w_down, norm_final, q, kt, v, state, xs, yconv, inner, gate, w_out)


def kernel(x_prompt, x_sample, cache_conv, state_ret, meta_tokens, norm_mix, w_in, conv_w, w_out,
           norm_mlp, w_up, w_down, norm_final):
    bsz, seq, _ = x_prompt.shape
    half = jnp.arange(0, HEAD_DIM, 2, dtype=F32) / HEAD_DIM
    inv_half = 1.0 / (ROPE_BASE ** half)
    inv_freq = jnp.concatenate([inv_half, inv_half])[None, :]

    nmix = norm_mix[0][None, :]
    nmlp = norm_mlp[0][None, :]
    nfin = norm_final[None, :]

    (w_in_b, w_out_b, cos, sin, dec, smeta, cmeta,
     q, kt, v, inner, gate, yconv, cnew) = _prologue_call(
        meta_tokens, x_sample, cache_conv[0], nmix, w_in[0], w_out[0], conv_w, inv_freq, seq)
    h1, conv_p, ret_p, w_up_b, w_down_b = _mixer_call(
        x_prompt, nmix, w_in_b, conv_w, w_out_b, cos, sin, dec, smeta, cmeta, w_up[0], w_down[0])
    y_prompt, ret_s, y_sample = _mlp_call(
        h1.reshape(bsz * seq, D_MODEL), nmlp, w_up_b, w_down_b, nfin,
        q, kt, v, state_ret[0], x_sample, yconv, inner, gate, w_out_b)
    y_prompt = y_prompt.reshape(bsz, seq, D_MODEL)

    return (y_prompt, y_sample, conv_p, ret_p, cnew[None], ret_s[None])
```

```python
import functools

import numpy as np
import jax
import jax.numpy as jnp
from jax import lax
from jax.experimental import pallas as pl
from jax.experimental.pallas import tpu as pltpu

D_MODEL = 1024
N_META = 16
CONV_CH = 512
CONV_WIDTH = 3
RET_HEADS = 4
HEAD_DIM = 128
RET_WIDTH = RET_HEADS * HEAD_DIM
CHUNK = 128
D_FF = 4 * D_MODEL
EPS = 1e-6
ROPE_BASE = 10000.0
PAST_LEN = 16384
K_SCALE = HEAD_DIM ** -0.5

OFF_U, OFF_C, OFF_B, OFF_Q, OFF_K, OFF_V, OFF_G = 0, 512, 1024, 1536, 2048, 2560, 3072
IN_TOTAL = 3584

GAMMA = tuple(1.0 - 2.0 ** (-5.0 - h) for h in range(RET_HEADS))
LOG_GAMMA = tuple(float(np.log(g)) for g in GAMMA)

TOKEN_TILE = 1024
MIX_SUBTILE = 512
PROJ_BLOCK = 512
OUT_BLOCK = 256
ROPE_PIECE = 256
MLP_TILE = 1024
MLP_SUBTILE = 512
SAMPLE_BLOCK = 8
VMEM_LIMIT_BYTES = 60 * 1024 * 1024

F32 = jnp.float32
BF16 = jnp.bfloat16


def _rmsnorm(x, gain):
    return x * lax.rsqrt(jnp.mean(x * x, axis=-1, keepdims=True) + EPS) * gain


def _dot(a, b):
    return jnp.dot(a, b, preferred_element_type=F32)


def _dot_nt(a, b):
    return lax.dot_general(a, b, (((1,), (1,)), ((), ())), preferred_element_type=F32)


def _dot_tn(a, b):
    return lax.dot_general(a, b, (((0,), (0,)), ((), ())), preferred_element_type=F32)


def _rope_tables(pos, inv_freq):
    ang = pos * inv_freq
    lane = lax.broadcasted_iota(jnp.int32, ang.shape, 1)
    sin = jnp.sin(ang)
    return jnp.cos(ang), jnp.where(lane < HEAD_DIM // 2, -sin, sin)


def _rope(x, cos, sin):
    return x * cos + pltpu.roll(x, HEAD_DIM // 2, 1) * sin


def _decay_tables(dec_ref):
    row = lax.broadcasted_iota(jnp.int32, (CHUNK, CHUNK), 0).astype(F32)
    col = lax.broadcasted_iota(jnp.int32, (CHUNK, CHUNK), 1).astype(F32)
    diff = row - col
    for h in range(RET_HEADS):
        lg = LOG_GAMMA[h]
        dec_ref[h] = jnp.where(diff >= 0, jnp.exp(lg * jnp.maximum(diff, 0.0)), 0.0)
        dec_ref[RET_HEADS + h] = jnp.exp((row + 1.0) * lg)
        dec_ref[2 * RET_HEADS + h] = jnp.exp((CHUNK - 1.0 - row) * lg)


def _full(shape):
    return pl.BlockSpec(shape, lambda *_: (0,) * len(shape))


def _resident(shape):
    return pl.BlockSpec(shape, lambda *_: (0,) * len(shape), pipeline_mode=pl.Buffered(1))


def _z_slice(zbuf, rows, off, width):
    blk, inner = divmod(off, PROJ_BLOCK)
    assert inner + width <= PROJ_BLOCK
    return zbuf[blk, rows, inner:inner + width]


def _meta_tail(zbuf, invf_ref, smeta_ref, cmeta_ref):
    rows = slice(0, CHUNK)
    cu = _z_slice(zbuf, rows, OFF_C, CONV_CH) * _z_slice(zbuf, rows, OFF_U, CONV_CH)
    cmeta_ref[...] = cu[N_META - 2:N_META, :]
    row = lax.broadcasted_iota(jnp.int32, (CHUNK, HEAD_DIM), 0).astype(F32)
    cos, sin = _rope_tables(row, invf_ref[...])
    for h in range(RET_HEADS):
        kz = _z_slice(zbuf, rows, OFF_K + h * HEAD_DIM, HEAD_DIM)
        vz = _z_slice(zbuf, rows, OFF_V + h * HEAD_DIM, HEAD_DIM)
        k = _rope(kz, cos, sin) * K_SCALE
        kdec = jnp.where(row < N_META, jnp.exp((N_META - 1.0 - row) * LOG_GAMMA[h]), 0.0)
        smeta_ref[h] = _dot_tn((k * kdec).astype(BF16), vz.astype(BF16))


def _sample_proj_tail(zbuf, rows, cache_ref, cw_ref, invf_ref,
                      q_ref, kt_ref, v_ref, inner_ref, gate_ref, yconv_ref, cnew_ref):
    prev0 = cache_ref[:, 0, :]
    prev1 = cache_ref[:, 1, :]
    cu = _z_slice(zbuf, rows, OFF_C, CONV_CH) * _z_slice(zbuf, rows, OFF_U, CONV_CH)
    conv = cw_ref[0, 0:1, :] * prev0 + cw_ref[0, 1:2, :] * prev1 + cw_ref[0, 2:3, :] * cu
    yconv_ref[...] = _z_slice(zbuf, rows, OFF_B, CONV_CH) * conv
    cnew_ref[:, 0, :] = prev1
    cnew_ref[:, 1, :] = cu
    pos = jnp.full((1, HEAD_DIM), float(PAST_LEN), F32)
    cos, sin = _rope_tables(pos, invf_ref[...])
    for h in range(RET_HEADS):
        cols = slice(h * HEAD_DIM, (h + 1) * HEAD_DIM)
        q = _rope(_z_slice(zbuf, rows, OFF_Q + h * HEAD_DIM, HEAD_DIM), cos, sin)
        k = _rope(_z_slice(zbuf, rows, OFF_K + h * HEAD_DIM, HEAD_DIM), cos, sin) * K_SCALE
        v = _z_slice(zbuf, rows, OFF_V + h * HEAD_DIM, HEAD_DIM)
        g = _z_slice(zbuf, rows, OFF_G + h * HEAD_DIM, HEAD_DIM)
        q_ref[:, cols] = q
        kt_ref[h] = k.T
        v_ref[:, cols] = v
        inner_ref[:, cols] = jnp.sum(q * k, axis=-1, keepdims=True) * v
        gate_ref[:, cols] = g * jax.nn.sigmoid(g)


def _prologue_kernel(meta_ref, xs_ref, cache_ref, nm_ref, win_ref, wout_ref, cw_ref, invf_ref,
                     winb_ref, woutb_ref, cos_ref, sin_ref, dec_ref, smeta_ref, cmeta_ref,
                     q_ref, kt_ref, v_ref, inner_ref, gate_ref, yconv_ref, cnew_ref,
                     xpad, hn_sc, zbuf, rope_sc):
    j = pl.program_id(0)
    n_steps = pl.num_programs(0)
    n_dec = xs_ref.shape[0]
    n_pieces = cos_ref.shape[0] // ROPE_PIECE

    def rope_piece(p):
        row0 = pl.multiple_of(p * ROPE_PIECE, ROPE_PIECE)
        base = jnp.asarray(N_META + p * ROPE_PIECE).astype(F32) * invf_ref[...]
        cb, sb = jnp.cos(base), jnp.sin(base)
        co, so = rope_sc[0], rope_sc[1]
        lane = lax.broadcasted_iota(jnp.int32, (ROPE_PIECE, HEAD_DIM), 1)
        sin = sb * co + cb * so
        cos_ref[pl.ds(row0, ROPE_PIECE), :] = cb * co - sb * so
        sin_ref[pl.ds(row0, ROPE_PIECE), :] = jnp.where(lane < HEAD_DIM // 2, -sin, sin)

    @pl.when(j == 0)
    def _():
        xpad[...] = jnp.zeros_like(xpad)
        xpad[0:N_META, :] = meta_ref[...]
        hn_sc[0:CHUNK, :] = _rmsnorm(xpad[...], nm_ref[...]).astype(BF16)
        hn_sc[CHUNK:CHUNK + n_dec, :] = _rmsnorm(xs_ref[:, 0, :], nm_ref[...]).astype(BF16)
        woutb_ref[...] = wout_ref[...].astype(BF16)
        _decay_tables(dec_ref)
        offs = lax.broadcasted_iota(jnp.int32, (ROPE_PIECE, HEAD_DIM), 0).astype(F32)
        ang = offs * invf_ref[...]
        rope_sc[0] = jnp.cos(ang)
        rope_sc[1] = jnp.sin(ang)

    w_blk = win_ref[...].astype(BF16)
    winb_ref[...] = w_blk
    zbuf[j] = _dot(hn_sc[...], w_blk)
    rope_piece(j)

    @pl.when(j == n_steps - 1)
    def _():
        for p in range(IN_TOTAL // PROJ_BLOCK, n_pieces):
            rope_piece(p)
        _meta_tail(zbuf, invf_ref, smeta_ref, cmeta_ref)
        _sample_proj_tail(zbuf, slice(CHUNK, CHUNK + n_dec), cache_ref, cw_ref, invf_ref,
                          q_ref, kt_ref, v_ref, inner_ref, gate_ref, yconv_ref, cnew_ref)


def _prologue_call(meta, xs, cache, norm_mix, w_in, w_out, conv_w, inv_freq, seq):
    n = xs.shape[0]
    n_blk = IN_TOTAL // PROJ_BLOCK
    assert seq % ROPE_PIECE == 0 and seq // ROPE_PIECE >= n_blk
    wide = jax.ShapeDtypeStruct((n, RET_WIDTH), F32)
    return pl.pallas_call(
        _prologue_kernel,
        grid=(n_blk,),
        in_specs=[
            _full((N_META, D_MODEL)),
            _full((n, 1, D_MODEL)),
            _full((n, CONV_WIDTH - 1, CONV_CH)),
            _full((1, D_MODEL)),
            pl.BlockSpec((D_MODEL, PROJ_BLOCK), lambda j: (0, j)),
            _full((D_MODEL, D_MODEL)),
            _full((1, CONV_WIDTH, CONV_CH)),
            _full((1, HEAD_DIM)),
        ],
        out_specs=[
            pl.BlockSpec((D_MODEL, PROJ_BLOCK), lambda j: (0, j)),
            _full((D_MODEL, D_MODEL)),
            _full((seq, HEAD_DIM)),
            _full((seq, HEAD_DIM)),
            _full((3 * RET_HEADS, CHUNK, CHUNK)),
            _full((RET_HEADS, HEAD_DIM, HEAD_DIM)),
            _full((CONV_WIDTH - 1, CONV_CH)),
            _full((n, RET_WIDTH)),
            _full((RET_HEADS, HEAD_DIM, n)),
            _full((n, RET_WIDTH)),
            _full((n, RET_WIDTH)),
            _full((n, RET_WIDTH)),
            _full((n, CONV_CH)),
            _full((n, CONV_WIDTH - 1, CONV_CH)),
        ],
        out_shape=(jax.ShapeDtypeStruct((D_MODEL, IN_TOTAL), BF16),
                   jax.ShapeDtypeStruct((D_MODEL, D_MODEL), BF16),
                   jax.ShapeDtypeStruct((seq, HEAD_DIM), F32),
                   jax.ShapeDtypeStruct((seq, HEAD_DIM), F32),
                   jax.ShapeDtypeStruct((3 * RET_HEADS, CHUNK, CHUNK), F32),
                   jax.ShapeDtypeStruct((RET_HEADS, HEAD_DIM, HEAD_DIM), F32),
                   jax.ShapeDtypeStruct((CONV_WIDTH - 1, CONV_CH), F32),
                   wide,
                   jax.ShapeDtypeStruct((RET_HEADS, HEAD_DIM, n), F32),
                   wide, wide, wide,
                   jax.ShapeDtypeStruct((n, CONV_CH), F32),
                   jax.ShapeDtypeStruct((n, CONV_WIDTH - 1, CONV_CH), F32)),
        scratch_shapes=[pltpu.VMEM((CHUNK, D_MODEL), F32),
                        pltpu.VMEM((CHUNK + n, D_MODEL), BF16),
                        pltpu.VMEM((n_blk, CHUNK + n, PROJ_BLOCK), F32),
                        pltpu.VMEM((2, ROPE_PIECE, HEAD_DIM), F32)],
        compiler_params=pltpu.CompilerParams(
            dimension_semantics=("arbitrary",), vmem_limit_bytes=VMEM_LIMIT_BYTES),
        name="prologue",
    )(meta, xs, cache, norm_mix, w_in, w_out, conv_w, inv_freq)


def _mixer_kernel(x_ref, nm_ref, win_ref, cw_ref, wout_ref, cos_ref, sin_ref, dec_ref,
                  smeta_ref, cmeta_ref, wup_ref, wdown_ref,
                  h1_ref, cstate_ref, sstate_ref, wup_bf_ref, wdown_bf_ref,
                  zbuf, cbuf, ymix, hn_sc):
    t = pl.program_id(1)
    tm = TOKEN_TILE

    wup_bf_ref[...] = wup_ref[...].astype(BF16)
    wdown_bf_ref[...] = wdown_ref[...].astype(BF16)

    @pl.when(t == 0)
    def _():
        sstate_ref[0, 0] = smeta_ref[...]
        cbuf[6:8, :] = cmeta_ref[...]

    n_sub = tm // MIX_SUBTILE
    chunks_per_sub = MIX_SUBTILE // CHUNK

    def sub_rows(s):
        return slice(s * MIX_SUBTILE, (s + 1) * MIX_SUBTILE)

    def stage_rows(s):
        hn_sc[s % 2] = _rmsnorm(x_ref[0, sub_rows(s), :], nm_ref[...]).astype(BF16)

    def project(s, p):
        cols = slice(p * PROJ_BLOCK, (p + 1) * PROJ_BLOCK)
        zbuf[s % 2, :, cols] = _dot(hn_sc[s % 2], win_ref[:, cols])

    def out_project(s, p):
        cols = slice(p * OUT_BLOCK, (p + 1) * OUT_BLOCK)
        h1_ref[0, sub_rows(s), cols] = (x_ref[0, sub_rows(s), cols]
                                        + _dot(ymix[s % 2], wout_ref[:, cols]))

    def conv_chunk(s, c):
        z = zbuf.at[s % 2]
        rows = slice(c * CHUNK, (c + 1) * CHUNK)
        g0 = (s * chunks_per_sub + c) * CHUNK
        cu = z[rows, OFF_C:OFF_C + CONV_CH] * z[rows, OFF_U:OFF_U + CONV_CH]
        cbuf[8 + g0:8 + g0 + CHUNK, :] = cu
        conv = (cw_ref[0, 0:1, :] * cbuf[6 + g0:6 + g0 + CHUNK, :]
                + cw_ref[0, 1:2, :] * cbuf[7 + g0:7 + g0 + CHUNK, :]
                + cw_ref[0, 2:3, :] * cu)
        ymix[s % 2, rows, 0:CONV_CH] = (z[rows, OFF_B:OFF_B + CONV_CH] * conv).astype(BF16)

    saved = {}

    def scores_and_state(s, c):
        z_cur = zbuf.at[s % 2]
        rows = slice(c * CHUNK, (c + 1) * CHUNK)
        g0 = (s * chunks_per_sub + c) * CHUNK
        cos = cos_ref[g0:g0 + CHUNK, :]
        sin = sin_ref[g0:g0 + CHUNK, :]
        qbs, kbs, vbs = [], [], []
        for h in range(RET_HEADS):
            qz = z_cur[rows, OFF_Q + h * HEAD_DIM:OFF_Q + (h + 1) * HEAD_DIM]
            kz = z_cur[rows, OFF_K + h * HEAD_DIM:OFF_K + (h + 1) * HEAD_DIM]
            vb = z_cur[rows, OFF_V + h * HEAD_DIM:OFF_V + (h + 1) * HEAD_DIM].astype(BF16)
            k = _rope(kz, cos, sin) * K_SCALE
            state = sstate_ref[0, 0, h]
            kd = (k * dec_ref[2 * RET_HEADS + h]).astype(BF16)
            sstate_ref[0, 0, h] = state * (GAMMA[h] ** CHUNK) + _dot_tn(kd, vb)
            qbs.append(_rope(qz, cos, sin).astype(BF16))
            kbs.append(jnp.concatenate([k.T.astype(BF16), state.astype(BF16)], axis=1))
            vbs.append(vb)
        both = [_dot(qbs[h], kbs[h]) for h in range(RET_HEADS)]
        scores = [b[:, 0:CHUNK] for b in both]
        cross = [b[:, CHUNK:CHUNK + HEAD_DIM] for b in both]
        saved[(s, c)] = (scores, cross, vbs)

    def outputs(s, c):
        z_cur = zbuf.at[s % 2]
        rows = slice(c * CHUNK, (c + 1) * CHUNK)
        scores, cross, vbs = saved.pop((s, c))
        for h in range(RET_HEADS):
            gz = z_cur[rows, OFF_G + h * HEAD_DIM:OFF_G + (h + 1) * HEAD_DIM]
            p = (scores[h] * dec_ref[h]).astype(BF16)
            o = _dot(p, vbs[h]) + cross[h] * dec_ref[RET_HEADS + h]
            on = o * lax.rsqrt(jnp.mean(o * o, axis=-1, keepdims=True) + EPS)
            ymix[s % 2, rows, CONV_CH + h * HEAD_DIM:CONV_CH + (h + 1) * HEAD_DIM] = (
                on * (gz * jax.nn.sigmoid(gz))).astype(BF16)

    stage_rows(0)
    for r in range(2):
        part = slice(r * (MIX_SUBTILE // 2), (r + 1) * (MIX_SUBTILE // 2))
        zbuf[0, part, 0:PROJ_BLOCK] = _dot(hn_sc[0, part, :], win_ref[:, 0:PROJ_BLOCK])
    for p in range(1, IN_TOTAL // PROJ_BLOCK):
        project(0, p)
    for s in range(n_sub):
        fillers = []
        if s > 0:
            fillers += [functools.partial(out_project, s - 1, p)
                        for p in range(D_MODEL // OUT_BLOCK)]
        if s + 1 < n_sub:
            stage_rows(s + 1)
            fillers += [functools.partial(project, s + 1, p)
                        for p in range(IN_TOTAL // PROJ_BLOCK)]
        steps = []
        for c in range(chunks_per_sub):
            steps.append(functools.partial(scores_and_state, s, c))
            if c > 0:
                steps.append(functools.partial(outputs, s, c - 1))
        steps.append(functools.partial(outputs, s, chunks_per_sub - 1))
        n_f, n_s = len(fillers), len(steps)
        placed = 0
        for i, step in enumerate(steps):
            while placed < n_f and placed * n_s <= i * n_f:
                fillers[placed]()
                placed += 1
            step()
            if i % 2 == 0:
                conv_chunk(s, i // 2)
        while placed < n_f:
            fillers[placed]()
            placed += 1
    for p in range(D_MODEL // OUT_BLOCK):
        out_project(n_sub - 1, p)

    tail = cbuf[8 + tm - 2:8 + tm, :]
    cbuf[6:8, :] = tail
    cstate_ref[0, 0] = tail


def _mixer_call(x, norm_mix, w_in, conv_w, w_out, cos, sin, dec, smeta, cmeta, w_up, w_down):
    bsz, seq, _ = x.shape
    tm = TOKEN_TILE
    n_t = seq // tm
    grid = (bsz, n_t)
    ff_slice = D_FF // (bsz * n_t)
    return pl.pallas_call(
        _mixer_kernel,
        grid=grid,
        in_specs=[
            pl.BlockSpec((1, tm, D_MODEL), lambda b, t: (b, t, 0)),
            _full((1, D_MODEL)),
            _resident((D_MODEL, IN_TOTAL)),
            _full((1, CONV_WIDTH, CONV_CH)),
            _resident((D_MODEL, D_MODEL)),
            pl.BlockSpec((tm, HEAD_DIM), lambda b, t: (t, 0)),
            pl.BlockSpec((tm, HEAD_DIM), lambda b, t: (t, 0)),
            _full((3 * RET_HEADS, CHUNK, CHUNK)),
            _full((RET_HEADS, HEAD_DIM, HEAD_DIM)),
            _full((CONV_WIDTH - 1, CONV_CH)),
            pl.BlockSpec((D_MODEL, ff_slice), lambda b, t: (0, b * n_t + t)),
            pl.BlockSpec((ff_slice, D_MODEL), lambda b, t: (b * n_t + t, 0)),
        ],
        out_specs=[
            pl.BlockSpec((1, tm, D_MODEL), lambda b, t: (b, t, 0)),
            pl.BlockSpec((1, 1, CONV_WIDTH - 1, CONV_CH), lambda b, t: (0, b, 0, 0)),
            pl.BlockSpec((1, 1, RET_HEADS, HEAD_DIM, HEAD_DIM), lambda b, t: (0, b, 0, 0, 0)),
            pl.BlockSpec((D_MODEL, ff_slice), lambda b, t: (0, b * n_t + t)),
            pl.BlockSpec((ff_slice, D_MODEL), lambda b, t: (b * n_t + t, 0)),
        ],
        out_shape=(
            jax.ShapeDtypeStruct((bsz, seq, D_MODEL), F32),
            jax.ShapeDtypeStruct((1, bsz, CONV_WIDTH - 1, CONV_CH), F32),
            jax.ShapeDtypeStruct((1, bsz, RET_HEADS, HEAD_DIM, HEAD_DIM), F32),
            jax.ShapeDtypeStruct((D_MODEL, D_FF), BF16),
            jax.ShapeDtypeStruct((D_FF, D_MODEL), BF16),
        ),
        scratch_shapes=[
            pltpu.VMEM((2, MIX_SUBTILE, IN_TOTAL), F32),
            pltpu.VMEM((tm + 8, CONV_CH), F32),
            pltpu.VMEM((2, MIX_SUBTILE, D_MODEL), BF16),
            pltpu.VMEM((2, MIX_SUBTILE, D_MODEL), BF16),
        ],
        compiler_params=pltpu.CompilerParams(
            dimension_semantics=("arbitrary", "arbitrary"), vmem_limit_bytes=VMEM_LIMIT_BYTES),
        name="prompt_mixer",
    )(x, norm_mix, w_in, conv_w, w_out, cos, sin, dec, smeta, cmeta, w_up, w_down)


def _mlp_body(h, nm, wup_ref, wdown_ref, nf):
    hn = _rmsnorm(h, nm).astype(BF16)
    acc = h
    ff_block = 1024
    for c in range(D_FF // ff_block):
        up = _dot(hn, wup_ref[:, c * ff_block:(c + 1) * ff_block])
        act = jnp.square(jnp.maximum(up, 0.0)).astype(BF16)
        acc = acc + _dot(act, wdown_ref[c * ff_block:(c + 1) * ff_block, :])
    return _rmsnorm(acc, nf)


def _prompt_mlp_tile(h_ref, nm_ref, wup_ref, wdown_ref, nf_ref, y_ref, hn_sc, acc_sc):
    n_sub = MLP_TILE // MLP_SUBTILE
    ff_block = 1024
    n_ff = D_FF // ff_block

    def rows(s):
        return slice(s * MLP_SUBTILE, (s + 1) * MLP_SUBTILE)

    hn_sc[0] = _rmsnorm(h_ref[rows(0), :], nm_ref[...]).astype(BF16)
    for s in range(n_sub):
        hn = hn_sc[s % 2]
        acc = h_ref[rows(s), :]
        for c in range(n_ff):
            if s == 0 and c == 0:
                half = MLP_SUBTILE // 2
                up = jnp.concatenate(
                    [_dot(hn[r * half:(r + 1) * half], wup_ref[:, 0:ff_block]) for r in range(2)],
                    axis=0)
            else:
                up = _dot(hn, wup_ref[:, c * ff_block:(c + 1) * ff_block])
            act = jnp.square(jnp.maximum(up, 0.0)).astype(BF16)
            if s + 1 == n_sub and c + 1 == n_ff:
                half = MLP_SUBTILE // 2
                for r in range(2):
                    part = slice(r * half, (r + 1) * half)
                    out = acc[part] + _dot(act[part], wdown_ref[c * ff_block:(c + 1) * ff_block, :])
                    y_ref[s * MLP_SUBTILE + r * half:s * MLP_SUBTILE + (r + 1) * half, :] = (
                        _rmsnorm(out, nf_ref[...]))
                break
            acc = acc + _dot(act, wdown_ref[c * ff_block:(c + 1) * ff_block, :])
            if c == 0 and s > 0:
                y_ref[rows(s - 1), :] = _rmsnorm(acc_sc[...], nf_ref[...])
            if c == 1 and s + 1 < n_sub:
                hn_sc[(s + 1) % 2] = _rmsnorm(h_ref[rows(s + 1), :], nm_ref[...]).astype(BF16)
        if s + 1 < n_sub:
            acc_sc[...] = acc


def _sample_state_block(step, heads, q_ref, kt_ref, v_ref, st_ref, new_ref, cross_sc):
    assert SAMPLE_BLOCK == 8
    lane = lax.broadcasted_iota(jnp.int32, (HEAD_DIM, HEAD_DIM), 1)
    sub = lax.broadcasted_iota(jnp.int32, (SAMPLE_BLOCK, HEAD_DIM), 0)
    row0 = pl.multiple_of(step * SAMPLE_BLOCK, SAMPLE_BLOCK)
    for h in heads:
        cols = slice(h * HEAD_DIM, (h + 1) * HEAD_DIM)
        vb = v_ref[:, cols].astype(BF16)
        q_blk = q_ref[:, cols]
        q_diag = jnp.concatenate(
            [jnp.where(sub == i, q_blk, 0.0) for i in range(SAMPLE_BLOCK)], axis=1).astype(BF16)
        s_stack = st_ref[:, h].reshape(SAMPLE_BLOCK * HEAD_DIM, HEAD_DIM).astype(BF16)
        cross_sc[pl.ds(row0, SAMPLE_BLOCK), cols] = _dot(q_diag, s_stack) * GAMMA[h]
        for i in range(SAMPLE_BLOCK):
            b = step * SAMPLE_BLOCK + i
            kt_b = jnp.where(lane == b, kt_ref[h], 0.0).astype(BF16)
            new_ref[i, h] = st_ref[i, h] * GAMMA[h] + _dot(kt_b, vb)


def _sample_tail(x_ref, yconv_ref, inner_ref, gate_ref, wout_ref, nmlp_ref, wup_ref, wdown_ref,
                 nf_ref, cross_sc, ymix, y_ref):
    ymix[:, 0:CONV_CH] = yconv_ref[...].astype(BF16)
    for h in range(RET_HEADS):
        cols = slice(h * HEAD_DIM, (h + 1) * HEAD_DIM)
        o = inner_ref[:, cols] + cross_sc[:, cols]
        on = o * lax.rsqrt(jnp.mean(o * o, axis=-1, keepdims=True) + EPS)
        ymix[:, CONV_CH + h * HEAD_DIM:CONV_CH + (h + 1) * HEAD_DIM] = (
            on * gate_ref[:, cols]).astype(BF16)
    h1 = x_ref[:, 0, :] + _dot(ymix[...], wout_ref[...])
    y_ref[:, 0, :] = _mlp_body(h1, nmlp_ref[...], wup_ref, wdown_ref, nf_ref[...])


def _mlp_kernel(h_ref, nm_ref, wup_ref, wdown_ref, nf_ref,
                q_ref, kt_ref, v_ref, st_ref, xs_ref, yconv_ref, inner_ref, gate_ref, wout_ref,
                y_ref, new_ref, ys_ref,
                hn_sc, acc_sc, cross_sc, ymix_sc):
    step = pl.program_id(0)
    n_tiles = pl.num_programs(0) - 1

    @pl.when(step < n_tiles)
    def _():
        _sample_state_block(step, (0, 1), q_ref, kt_ref, v_ref, st_ref, new_ref, cross_sc)
        _prompt_mlp_tile(h_ref, nm_ref, wup_ref, wdown_ref, nf_ref, y_ref, hn_sc, acc_sc)
        _sample_state_block(step, (2, 3), q_ref, kt_ref, v_ref, st_ref, new_ref, cross_sc)

    @pl.when(step == n_tiles)
    def _():
        _sample_tail(xs_ref, yconv_ref, inner_ref, gate_ref, wout_ref, nm_ref, wup_ref, wdown_ref,
                     nf_ref, cross_sc, ymix_sc, ys_ref)


def _mlp_call(h, norm_mlp, w_up, w_down, norm_final, q, kt, v, state, xs, yconv, inner, gate, w_out):
    rows = h.shape[0]
    n = xs.shape[0]
    tm = MLP_TILE
    n_tiles = rows // tm
    bb = SAMPLE_BLOCK
    assert n_tiles * bb == n
    last = n_tiles - 1

    def tile(i):
        return jnp.minimum(i, last)

    return pl.pallas_call(
        _mlp_kernel,
        grid=(n_tiles + 1,),
        in_specs=[
            pl.BlockSpec((tm, D_MODEL), lambda i: (tile(i), 0)),
            _full((1, D_MODEL)),
            _resident((D_MODEL, D_FF)),
            _resident((D_FF, D_MODEL)),
            _full((1, D_MODEL)),
            pl.BlockSpec((bb, RET_WIDTH), lambda i: (tile(i), 0)),
            _full((RET_HEADS, HEAD_DIM, n)),
            _full((n, RET_WIDTH)),
            pl.BlockSpec((bb, RET_HEADS, HEAD_DIM, HEAD_DIM), lambda i: (tile(i), 0, 0, 0)),
            _full((n, 1, D_MODEL)),
            _full((n, CONV_CH)),
            _full((n, RET_WIDTH)),
            _full((n, RET_WIDTH)),
            _resident((D_MODEL, D_MODEL)),
        ],
        out_specs=[
            pl.BlockSpec((tm, D_MODEL), lambda i: (tile(i), 0)),
            pl.BlockSpec((bb, RET_HEADS, HEAD_DIM, HEAD_DIM), lambda i: (tile(i), 0, 0, 0)),
            _full((n, 1, D_MODEL)),
        ],
        out_shape=(jax.ShapeDtypeStruct((rows, D_MODEL), F32),
                   jax.ShapeDtypeStruct(state.shape, F32),
                   jax.ShapeDtypeStruct((n, 1, D_MODEL), F32)),
        scratch_shapes=[pltpu.VMEM((2, MLP_SUBTILE, D_MODEL), BF16),
                        pltpu.VMEM((MLP_SUBTILE, D_MODEL), F32),
                        pltpu.VMEM((n, RET_WIDTH), F32),
                        pltpu.VMEM((n, D_MODEL), BF16)],
        compiler_params=pltpu.CompilerParams(
            dimension_semantics=("arbitrary",), vmem_limit_bytes=VMEM_LIMIT_BYTES),
        name="mlp_and_sample",
    )(h, norm_mlp, w_up, w_down, norm_final, q, kt, v, state, xs, yconv, inner, gate, w_out)


def kernel(x_prompt, x_sample, cache_conv, state_ret, meta_tokens, norm_mix, w_in, conv_w, w_out,
           norm_mlp, w_up, w_down, norm_final):
    bsz, seq, _ = x_prompt.shape
    half = jnp.arange(0, HEAD_DIM, 2, dtype=F32) / HEAD_DIM
    inv_half = 1.0 / (ROPE_BASE ** half)
    inv_freq = jnp.concatenate([inv_half, inv_half])[None, :]

    nmix = norm_mix[0][None, :]
    nmlp = norm_mlp[0][None, :]
    nfin = norm_final[None, :]

    (w_in_b, w_out_b, cos, sin, dec, smeta, cmeta,
     q, kt, v, inner, gate, yconv, cnew) = _prologue_call(
        meta_tokens, x_sample, cache_conv[0], nmix, w_in[0], w_out[0], conv_w, inv_freq, seq)
    h1, conv_p, ret_p, w_up_b, w_down_b = _mixer_call(
        x_prompt, nmix, w_in_b, conv_w, w_out_b, cos, sin, dec, smeta, cmeta, w_up[0], w_down[0])
    y_prompt, ret_s, y_sample = _mlp_call(
        h1.reshape(bsz * seq, D_MODEL), nmlp, w_up_b, w_down_b, nfin,
        q, kt, v, state_ret[0], x_sample, yconv, inner, gate, w_out_b)
    y_prompt = y_prompt.reshape(bsz, seq, D_MODEL)

    return (y_prompt, y_sample, conv_p, ret_p, cnew[None], ret_s[None])
```
